```python
import math
import jax
import jax.numpy as jnp
from jax import lax
import numpy as np

D_MODEL = 1024
BATCH = 1
SEQ = 16384
DEPTH = 4

CTX_LEN = 256
GRID_W = 64
N_MIXERS = 2
EPS = 1e-6
ADA_SCALE = 0.5

M_INNER = 2 * D_MODEL
M_HEADS = 4
M_DH_IN = M_INNER // M_HEADS
M_DQK = M_DH_IN // 2
M_DV = M_DH_IN
M_CONV = 3
M_CHUNK = 64

H_ORDER = 2
H_SCONV = 3
H_BANDS = 16
H_EMB = 2 * H_BANDS + 1
H_FH = 64
H_DECAY_MIN = -math.log(1e-2) / 1.5
H_DECAY_MAX = -math.log(1e-2) / 0.3

N_GROUPS = 4
EXP_PER_GROUP = 8
N_EXPERTS = N_GROUPS * EXP_PER_GROUP
TOP_K = 2
D_EXPERT = D_MODEL // 2
MOE_BLOCK = 128

kernel_name = 'hybrid_mlstm_hyena_hmoe_flow_backbone'


def rmsnorm(x, g):
    xf = x.astype(jnp.float32)
    y = xf * lax.rsqrt(jnp.mean(xf * xf, axis=-1, keepdims=True) + EPS)
    return (y * g.astype(jnp.float32)).astype(x.dtype)


def conv_centred(u, w, b):
    K = w.shape[0]
    L = u.shape[1]
    pad = K // 2
    up = jnp.pad(u, ((0, 0), (pad, pad), (0, 0)))
    out = b
    for kk in range(K):
        out = out + up[:, kk:kk + L] * w[kk]
    return out


def to_scan_order(x, col_major):
    if not col_major:
        return x
    B, L, D = x.shape
    rows = L // GRID_W
    return x.reshape(B, rows, GRID_W, D).transpose(0, 2, 1, 3).reshape(B, L, D)


def from_scan_order(x, col_major):
    if not col_major:
        return x
    B, L, D = x.shape
    rows = L // GRID_W
    return x.reshape(B, GRID_W, rows, D).transpose(0, 2, 1, 3).reshape(B, L, D)


def zero_state(B):
    f32 = jnp.float32
    return (jnp.zeros((B, M_HEADS, M_DV, M_DQK), f32),
            jnp.zeros((B, M_HEADS, M_DQK), f32),
            jnp.zeros((B, M_HEADS), f32))


def mlstm_scan(q, k, v, logi, logf, state):
    B, H, L, _ = q.shape
    nc = L // M_CHUNK

    def chunks(a):
        a = a.reshape(B, H, nc, M_CHUNK, *a.shape[3:])
        return jnp.moveaxis(a, 2, 0)

    mask = jnp.tril(jnp.ones((M_CHUNK, M_CHUNK), dtype=bool))

    def step(carry, xs):
        C, n, m = carry
        qc, kc, vc, ic, fc = xs
        b = jnp.cumsum(fc, axis=-1)
        log_d = b[..., :, None] - b[..., None, :] + ic[..., None, :]
        log_d = jnp.where(mask, log_d, -jnp.inf)
        inter = b + m[..., None]
        mj = jnp.maximum(inter, jnp.max(log_d, axis=-1))
        s = jnp.einsum('bhjd,bhsd->bhjs', qc, kc) * jnp.exp(log_d - mj[..., None])
        w_inter = jnp.exp(inter - mj)
        num = jnp.einsum('bhjs,bhsv->bhjv', s, vc) + w_inter[..., None] * jnp.einsum('bhjd,bhvd->bhjv', qc, C)
        den = jnp.sum(s, axis=-1) + w_inter * jnp.einsum('bhjd,bhd->bhj', qc, n)
        h = num / jnp.maximum(jnp.abs(den), jnp.exp(-mj))[..., None]
        b_last = b[..., -1]
        g = b_last[..., None] - b + ic
        m_new = jnp.maximum(b_last + m, jnp.max(g, axis=-1))
        decay = jnp.exp(b_last + m - m_new)
        wg = jnp.exp(g - m_new[..., None])
        C_new = decay[..., None, None] * C + jnp.einsum('bhsv,bhsd->bhvd', vc * wg[..., None], kc)
        n_new = decay[..., None] * n + jnp.einsum('bhs,bhsd->bhd', wg, kc)
        return (C_new, n_new, m_new), h

    state, h = lax.scan(step, state, (chunks(q), chunks(k), chunks(v), chunks(logi), chunks(logf)))
    h = jnp.moveaxis(h, 0, 2).reshape(B, H, L, -1)
    return h, state


def mlstm_mixer(u, in_w, conv_w, conv_b, q_w, k_w, v_w, gate_w, gate_b, norm_g, skip, out_w,
                init_f, init_b, need_out=True):
    f32 = jnp.float32
    B, L, _ = u.shape
    xm, o_pre = jnp.split(u @ in_w, 2, axis=-1)
    xc = jax.nn.silu(conv_centred(xm, conv_w, conv_b))
    xc_h = xc.reshape(B, L, M_HEADS, M_DH_IN)
    xm_h = xm.reshape(B, L, M_HEADS, M_DH_IN)
    q = jnp.einsum('blhi,hid->bhld', xc_h, q_w).astype(f32)
    k = (jnp.einsum('blhi,hid->bhld', xc_h, k_w) * (M_DQK ** -0.5)).astype(f32)
    v = jnp.einsum('blhi,hid->bhld', xm_h, v_w).astype(f32)
    gates = (jnp.concatenate([xc, xm], axis=-1) @ gate_w + gate_b).astype(f32)
    gates = gates.reshape(B, L, 4, M_HEADS).transpose(2, 0, 3, 1)
    h_f, st_f = mlstm_scan(q, k, v, gates[0], jax.nn.log_sigmoid(gates[1]), init_f)
    rev = lambda a: jnp.flip(a, axis=2)
    h_b, st_b = mlstm_scan(rev(q), rev(k), rev(v), rev(gates[2]), rev(jax.nn.log_sigmoid(gates[3])), init_b)
    if not need_out:
        return None, st_f, st_b
    h = h_f + rev(h_b)
    mu = jnp.mean(h, axis=-1, keepdims=True)
    var = jnp.mean(jnp.square(h - mu), axis=-1, keepdims=True)
    h = (h - mu) * lax.rsqrt(var + EPS)
    h = h.transpose(0, 2, 1, 3).reshape(B, L, M_INNER).astype(u.dtype)
    y = (h * norm_g + skip * xc) * jax.nn.sigmoid(o_pre)
    return y @ out_w, st_f, st_b


def hyena_filters(L, w1, b1, freq, w2, b2, w3, decay):
    f32 = jnp.float32
    pos = jnp.arange(L, dtype=f32)
    t = pos / max(L - 1, 1)
    bands = jnp.arange(1, H_BANDS + 1, dtype=f32)
    ang = (2.0 * math.pi / L) * pos[:, None] * bands[None, :]
    feat = jnp.concatenate([t[:, None], jnp.cos(ang), jnp.sin(ang)], axis=-1)
    fr = freq.astype(f32)
    z = jnp.sin(fr * (feat @ w1.astype(f32) + b1.astype(f32)))
    z = jnp.sin(fr * (z @ w2.astype(f32) + b2.astype(f32)))
    z = z @ w3.astype(f32)
    tau = jnp.abs(pos - L // 2) / (L / 2)
    filt = z * jnp.exp(-tau[:, None] * decay.astype(f32)[None, :])
    return filt * lax.rsqrt(jnp.sum(filt * filt, axis=0, keepdims=True) + EPS)


def fft_long_conv(u, filt, bias):
    f32 = jnp.float32
    L = u.shape[1]
    n = 2 * L
    uf = u.astype(f32)
    spec = jnp.fft.rfft(uf, n=n, axis=1) * jnp.fft.rfft(filt, n=n, axis=0)[None]
    y = jnp.fft.irfft(spec, n=n, axis=1)[:, L // 2:L // 2 + L]
    return (y + uf * bias.astype(f32)).astype(u.dtype)


def hyena_mixer(u, in_w, in_b, sc_w, sc_b, f_w1, f_b1, f_freq, f_w2, f_b2, f_w3, decay, fbias, out_w, out_b):
    B, L, D = u.shape
    z = conv_centred(u @ in_w + in_b, sc_w, sc_b)
    x1, x2, v = jnp.split(z, 3, axis=-1)
    filt = hyena_filters(L, f_w1, f_b1, f_freq, f_w2, f_b2, f_w3, decay)
    y = x1 * fft_long_conv(v, filt[:, :D], fbias[:D])
    y = x2 * fft_long_conv(y, filt[:, D:], fbias[D:])
    return y @ out_w + out_b


def moe_ffn(xt, rg_w, rg_b, re_w, re_b, w13, w2):
    f32 = jnp.float32
    T, D = xt.shape
    g_prob = jax.nn.softmax((xt @ rg_w + rg_b).astype(f32), axis=-1)
    g_w, g_idx = lax.top_k(g_prob, 1)
    e_logits = (xt @ re_w + re_b).astype(f32).reshape(T, N_GROUPS, EXP_PER_GROUP)
    e_logits = jnp.take_along_axis(e_logits, g_idx[:, :, None], axis=1)[:, 0]
    e_w, e_idx = lax.top_k(jax.nn.softmax(e_logits, axis=-1), TOP_K)
    e_w = e_w / jnp.sum(e_w, axis=-1, keepdims=True)
    weights = (g_w * e_w).reshape(-1)
    experts = (g_idx * EXP_PER_GROUP + e_idx).reshape(-1).astype(jnp.int32)
    A = T * TOP_K
    tok = jnp.arange(A, dtype=jnp.int32) // TOP_K
    order = jnp.argsort(experts)
    s_exp, s_tok, s_w = experts[order], tok[order], weights[order]
    counts = jnp.zeros((N_EXPERTS,), jnp.int32).at[experts].add(1)
    starts = jnp.cumsum(counts) - counts
    padded = (counts + MOE_BLOCK - 1) // MOE_BLOCK * MOE_BLOCK
    p_ends = jnp.cumsum(padded)
    p_starts = p_ends - padded
    dest = p_starts[s_exp] + jnp.arange(A, dtype=jnp.int32) - starts[s_exp]
    n_blocks = -(-A // MOE_BLOCK) + N_EXPERTS
    R = n_blocks * MOE_BLOCK
    row_tok = jnp.zeros((R,), jnp.int32).at[dest].set(s_tok)
    row_w = jnp.zeros((R,), f32).at[dest].set(s_w)
    blk_start = jnp.arange(n_blocks, dtype=jnp.int32) * MOE_BLOCK
    blk_exp = jnp.minimum(jnp.searchsorted(p_ends, blk_start, side='right'), N_EXPERTS - 1)
    xb = xt[row_tok].reshape(n_blocks, MOE_BLOCK, D)

    def expert_block(args):
        xblk, e = args
        gate, up = jnp.split(xblk @ w13[e], 2, axis=-1)
        return (jax.nn.silu(gate) * up) @ w2[e]

    yb = lax.map(expert_block, (xb, blk_exp)).reshape(R, D)
    return jax.ops.segment_sum(yb * row_w[:, None].astype(yb.dtype), row_tok, num_segments=T)


def setup_inputs(seed: int = 0) -> dict:
    key = jax.random.key(seed)
    ks = iter(jax.random.split(key, 64))
    f32 = jnp.float32

    def nrm(shape, scale):
        return scale * jax.random.normal(next(ks), shape, f32)

    D = D_MODEL
    n_a = len(range(0, DEPTH, N_MIXERS))
    n_b = len(range(1, DEPTH, N_MIXERS))
    fgate = jnp.linspace(3.0, 6.0, M_HEADS, dtype=f32)[None]
    decay = jnp.tile(jnp.linspace(H_DECAY_MIN, H_DECAY_MAX, D, dtype=f32), H_ORDER)[None]
    return {
        'x': nrm((BATCH, SEQ, D), 1.0),
        'c': nrm((BATCH, D), 1.0),
        'ctx': nrm((BATCH, CTX_LEN, D), 1.0),
        'c_ctx': nrm((D,), 1.0),
        'ada_w': nrm((DEPTH, D, 6 * D), ADA_SCALE * D ** -0.5),
        'ada_b': nrm((DEPTH, 6 * D), 0.01),
        'norm1_g': 1.0 + nrm((DEPTH, D), 0.02),
        'norm2_g': 1.0 + nrm((DEPTH, D), 0.02),
        'rg_w': nrm((DEPTH, D, N_GROUPS), D ** -0.5),
        'rg_b': nrm((DEPTH, N_GROUPS), 0.01),
        're_w': nrm((DEPTH, D, N_EXPERTS), D ** -0.5),
        're_b': nrm((DEPTH, N_EXPERTS), 0.01),
        'moe_w13': nrm((DEPTH, N_EXPERTS, D, 2 * D_EXPERT), D ** -0.5),
        'moe_w2': nrm((DEPTH, N_EXPERTS, D_EXPERT, D), D_EXPERT ** -0.5),
        'm_in_w': nrm((n_a, D, 2 * M_INNER), D ** -0.5),
        'm_conv_w': nrm((n_a, M_CONV, M_INNER), M_CONV ** -0.5),
        'm_conv_b': nrm((n_a, M_INNER), 0.01),
        'm_q_w': nrm((n_a, M_HEADS, M_DH_IN, M_DQK), M_DH_IN ** -0.5),
        'm_k_w': nrm((n_a, M_HEADS, M_DH_IN, M_DQK), M_DH_IN ** -0.5),
        'm_v_w': nrm((n_a, M_HEADS, M_DH_IN, M_DV), M_DH_IN ** -0.5),
        'm_gate_w': nrm((n_a, 2 * M_INNER, 4 * M_HEADS), 0.1 * (2 * M_INNER) ** -0.5),
        'm_gate_b': jnp.concatenate([nrm((n_a, M_HEADS), 0.1), fgate + nrm((n_a, M_HEADS), 0.1),
                                     nrm((n_a, M_HEADS), 0.1), fgate + nrm((n_a, M_HEADS), 0.1)], axis=-1),
        'm_norm_g': 1.0 + nrm((n_a, M_INNER), 0.02),
        'm_skip': 1.0 + nrm((n_a, M_INNER), 0.02),
        'm_out_w': nrm((n_a, M_INNER, D), M_INNER ** -0.5),
        'h_in_w': nrm((n_b, D, 3 * D), D ** -0.5),
        'h_in_b': nrm((n_b, 3 * D), 0.01),
        'h_sc_w': nrm((n_b, H_SCONV, 3 * D), H_SCONV ** -0.5),
        'h_sc_b': nrm((n_b, 3 * D), 0.01),
        'h_f_w1': nrm((n_b, H_EMB, H_FH), H_EMB ** -0.5),
        'h_f_b1': nrm((n_b, H_FH), 0.1),
        'h_f_freq': 1.0 + nrm((n_b, H_FH), 0.02),
        'h_f_w2': nrm((n_b, H_FH, H_FH), H_FH ** -0.5),
        'h_f_b2': nrm((n_b, H_FH), 0.1),
        'h_f_w3': nrm((n_b, H_FH, H_ORDER * D), H_FH ** -0.5),
        'h_decay': decay + nrm((n_b, H_ORDER * D), 0.05),
        'h_fbias': nrm((n_b, H_ORDER * D), 0.5),
        'h_out_w': nrm((n_b, D, D), D ** -0.5),
        'h_out_b': nrm((n_b, D), 0.01),
        'final_g': 1.0 + nrm((D,), 0.02),
    }


def reference(x, c, ctx, c_ctx, ada_w, ada_b, norm1_g, norm2_g, rg_w, rg_b, re_w, re_b, moe_w13, moe_w2,
              m_in_w, m_conv_w, m_conv_b, m_q_w, m_k_w, m_v_w, m_gate_w, m_gate_b, m_norm_g, m_skip, m_out_w,
              h_in_w, h_in_b, h_sc_w, h_sc_b, h_f_w1, h_f_b1, h_f_freq, h_f_w2, h_f_b2, h_f_w3, h_decay,
              h_fbias, h_out_w, h_out_b, final_g):
    B, S, D = x.shape
    n_ctx = ctx.shape[1]
    hl, hc = x, ctx
    for i in range(DEPTH):
        kind = i % N_MIXERS
        slot = i // N_MIXERS
        col_major = (slot % 2) == 1
        ctx_live = any(j % N_MIXERS == 0 for j in range(i + 1, DEPTH))
        ml = (jax.nn.silu(c) @ ada_w[i] + ada_b[i])[:, None, :]
        mc = (jax.nn.silu(c_ctx) @ ada_w[i] + ada_b[i])[None, None, :]
        sh1, sc1, g1, sh2, sc2, g2 = jnp.split(ml, 6, axis=-1)
        csh1, csc1, cg1, csh2, csc2, cg2 = jnp.split(mc, 6, axis=-1)

        ul = to_scan_order(rmsnorm(hl, norm1_g[i]) * (1.0 + sc1) + sh1, col_major)
        if kind == 0 or ctx_live:
            uc = rmsnorm(hc, norm1_g[i]) * (1.0 + csc1) + csh1
        if kind == 0:
            mp = (m_in_w[slot], m_conv_w[slot], m_conv_b[slot], m_q_w[slot], m_k_w[slot], m_v_w[slot],
                  m_gate_w[slot], m_gate_b[slot], m_norm_g[slot], m_skip[slot], m_out_w[slot])
            yc, st_f, st_b = mlstm_mixer(uc, *mp, zero_state(B), zero_state(B), need_out=ctx_live)
            yl, _, _ = mlstm_mixer(ul, *mp, st_f, st_b)
        else:
            hp = (h_in_w[slot], h_in_b[slot], h_sc_w[slot], h_sc_b[slot], h_f_w1[slot], h_f_b1[slot],
                  h_f_freq[slot], h_f_w2[slot], h_f_b2[slot], h_f_w3[slot], h_decay[slot], h_fbias[slot],
                  h_out_w[slot], h_out_b[slot])
            yl = hyena_mixer(ul, *hp)
            if ctx_live:
                yc = hyena_mixer(uc, *hp)
        hl = hl + g1 * from_scan_order(yl, col_major)
        if ctx_live:
            hc = hc + cg1 * yc

        mo = (rg_w[i], rg_b[i], re_w[i], re_b[i], moe_w13[i], moe_w2[i])
        vl = rmsnorm(hl, norm2_g[i]) * (1.0 + sc2) + sh2
        if ctx_live:
            vc = rmsnorm(hc, norm2_g[i]) * (1.0 + csc2) + csh2
            toks = jnp.concatenate([vc, vl], axis=1).reshape(-1, D)
            out = moe_ffn(toks, *mo).reshape(B, n_ctx + S, D)
            hc = hc + cg2 * out[:, :n_ctx]
            hl = hl + g2 * out[:, n_ctx:]
        else:
            hl = hl + g2 * moe_ffn(vl.reshape(-1, D), *mo).reshape(B, S, D)
    return rmsnorm(hl, final_g)
```

```python
import functools
import math

import jax
import jax.numpy as jnp
from jax import lax
from jax.experimental import pallas as pl
from jax.experimental.pallas import tpu as pltpu

F32 = jnp.float32
BF16 = jnp.bfloat16
HI = lax.Precision.HIGHEST

EPS = 1e-6
GRID_W = 64
N_MIXERS = 2
M_HEADS = 4
H_BANDS = 16
N_GROUPS = 4
EXP_PER_GROUP = 8
N_EXPERTS = N_GROUPS * EXP_PER_GROUP
MOE_BLOCK = 128
LANES = 128
HALO = 16
SUB = 8
FFT_N2 = 128
H_DECAY_EPS = EPS
VMEM_LIMIT = 56 * 1024 * 1024


def _cparams(*sem):
    return pltpu.CompilerParams(dimension_semantics=sem, vmem_limit_bytes=VMEM_LIMIT)


def _full(shape):
    n = len(shape)
    return pl.BlockSpec(shape, lambda *_: (0,) * n)


def _raster(L, tr_pref=128):
    rows = L // GRID_W
    return rows, min(tr_pref, rows)


def _sigmoid(x):
    return 1.0 / (1.0 + jnp.exp(-x))


def _silu(x):
    return x * _sigmoid(x)


def _norm_mod(x, g, mod, k):
    ms = jnp.mean(x * x, axis=-1, keepdims=True)
    y = x * lax.rsqrt(ms + EPS) * g
    return y * (1.0 + mod[k + 1:k + 2, :]) + mod[k:k + 1, :]


def _ada_kernel(c_ref, w_ref, b_ref, o_ref):
    s = _silu(c_ref[...])
    o_ref[0] = jnp.dot(s, w_ref[0], precision=HI, preferred_element_type=F32) + b_ref[0]


def _ada(cc, ada_w, ada_b):
    depth, D, N = ada_w.shape
    tn = N // 6
    return pl.pallas_call(
        _ada_kernel,
        grid=(depth, N // tn),
        in_specs=[pl.BlockSpec((8, D), lambda l, j: (0, 0)),
                  pl.BlockSpec((1, D, tn), lambda l, j: (l, 0, j)),
                  pl.BlockSpec((1, 1, tn), lambda l, j: (l, 0, j))],
        out_specs=pl.BlockSpec((1, 8, tn), lambda l, j: (l, 0, j)),
        out_shape=jax.ShapeDtypeStruct((depth, 8, N), F32),
        compiler_params=_cparams("parallel", "parallel"),
        name="ada",
    )(cc, ada_w, ada_b.reshape(depth, 1, N))


def _norm_proj_kernel(x_ref, g_ref, mod_ref, w_ref, b_ref, o_ref, *, nchunk, col_major):
    N = o_ref.shape[-1]
    step = N // nchunk
    for c in range(SUB if col_major else 1):
        x = x_ref[:, c, :] if col_major else x_ref[...]
        u = _norm_mod(x, g_ref[...], mod_ref[...], 0).astype(BF16)
        for j in range(nchunk):
            sl = slice(j * step, (j + 1) * step)
            acc = (jnp.dot(u, w_ref[:, sl], preferred_element_type=F32) + b_ref[:, sl]).astype(o_ref.dtype)
            if col_major:
                o_ref[c, :, sl] = acc
            else:
                o_ref[:, sl] = acc


def _norm_proj(x, g, mod, w, b, col_major):
    L, D = x.shape
    N = w.shape[1]
    consts = [_full((1, D)), _full(mod.shape), _full((D, N)), _full((1, N))]
    kern = functools.partial(_norm_proj_kernel, nchunk=max(1, N // 1024), col_major=col_major)
    args = (g.reshape(1, D), mod, w, b.reshape(1, N))
    if col_major:
        rows, tr = _raster(L)
        out = pl.pallas_call(
            kern,
            grid=(GRID_W // SUB, rows // tr),
            in_specs=[pl.BlockSpec((tr, SUB, D), lambda iw, ir: (ir, iw, 0))] + consts,
            out_specs=pl.BlockSpec((SUB, tr, N), lambda iw, ir: (iw, ir, 0)),
            out_shape=jax.ShapeDtypeStruct((GRID_W, rows, N), BF16),
            compiler_params=_cparams("parallel", "parallel"),
            name="norm_proj_colmajor",
        )(x.reshape(rows, GRID_W, D), *args)
        return out.reshape(L, N)
    tm = min(256, L)
    return pl.pallas_call(
        kern,
        grid=(L // tm,),
        in_specs=[pl.BlockSpec((tm, D), lambda i: (i, 0))] + consts,
        out_specs=pl.BlockSpec((tm, N), lambda i: (i, 0)),
        out_shape=jax.ShapeDtypeStruct((L, N), BF16),
        compiler_params=_cparams("parallel"),
        name="norm_proj",
    )(x, *args)


def _out_proj_kernel(y_ref, w_ref, b_ref, h_ref, mod_ref, o_ref, *, gate_row, col_major):
    gate = mod_ref[gate_row:gate_row + 1, :]
    for c in range(SUB if col_major else 1):
        y = y_ref[c] if col_major else y_ref[...]
        acc = jnp.dot(y, w_ref[...], preferred_element_type=F32) + b_ref[...]
        if col_major:
            o_ref[:, c, :] = h_ref[:, c, :] + gate * acc
        else:
            o_ref[...] = h_ref[...] + gate * acc


def _out_proj(y, w, b, h, mod, gate_row, col_major):
    L, K = y.shape
    D = w.shape[1]
    kern = functools.partial(_out_proj_kernel, gate_row=gate_row, col_major=col_major)
    if col_major:
        rows, tr = _raster(L)
        hspec = pl.BlockSpec((tr, SUB, D), lambda iw, ir: (ir, iw, 0))
        out = pl.pallas_call(
            kern,
            grid=(GRID_W // SUB, rows // tr),
            in_specs=[pl.BlockSpec((SUB, tr, K), lambda iw, ir: (iw, ir, 0)), _full((K, D)), _full((1, D)),
                      hspec, _full(mod.shape)],
            out_specs=hspec,
            out_shape=jax.ShapeDtypeStruct((rows, GRID_W, D), F32),
            compiler_params=_cparams("parallel", "parallel"),
            name="out_proj_colmajor",
        )(y.reshape(GRID_W, rows, K), w, b.reshape(1, D), h.reshape(rows, GRID_W, D), mod)
        return out.reshape(L, D)
    tm = min(256, L)
    rows_spec = lambda c: pl.BlockSpec((tm, c), lambda i: (i, 0))
    return pl.pallas_call(
        kern,
        grid=(L // tm,),
        in_specs=[rows_spec(K), _full((K, D)), _full((1, D)), rows_spec(D), _full(mod.shape)],
        out_specs=rows_spec(D),
        out_shape=jax.ShapeDtypeStruct((L, D), F32),
        compiler_params=_cparams("parallel"),
        name="out_proj",
    )(y, w, b.reshape(1, D), h, mod)


def _halo_specs(tm, L, ncol, cblock):
    hb = tm // HALO
    nb = L // HALO
    prev = pl.BlockSpec((HALO, ncol), lambda i: (jnp.maximum(i * hb - 1, 0), cblock))
    nxt = pl.BlockSpec((HALO, ncol), lambda i: (jnp.minimum((i + 1) * hb, nb - 1), cblock))
    return prev, nxt


def _conv3(x, prev_ref, next_ref, w, b):
    tm = x.shape[0]
    i = pl.program_id(0)
    first = i == 0
    last = i == pl.num_programs(0) - 1
    p = jnp.where(first, 0.0, prev_ref[HALO - 1:HALO, :].astype(F32))
    n = jnp.where(last, 0.0, next_ref[0:1, :].astype(F32))
    row = lax.broadcasted_iota(jnp.int32, x.shape, 0)
    xm1 = jnp.where(row == 0, p, pltpu.roll(x, 1, 0))
    xp1 = jnp.where(row == tm - 1, n, pltpu.roll(x, tm - 1, 0))
    return b + xm1 * w[0:1, :] + x * w[1:2, :] + xp1 * w[2:3, :]


def _mlstm_pre_kernel(x_ref, p_ref, n_ref, cw_ref, cb_ref, qw_ref, kw_ref, vw_ref, gw_ref, gb_ref,
                      xc_ref, q_ref, k_ref, v_ref, g_ref, gt_ref, *, heads, kscale):
    xm = x_ref[...]
    xc = _silu(_conv3(xm.astype(F32), p_ref, n_ref, cw_ref[...], cb_ref[...]))
    xcb = xc.astype(BF16)
    xc_ref[...] = xcb
    inner = xm.shape[1]
    dh = inner // heads
    dqk = qw_ref.shape[2]
    dv = vw_ref.shape[2]
    for h in range(heads):
        xs = xcb[:, h * dh:(h + 1) * dh]
        q_ref[:, h * dqk:(h + 1) * dqk] = jnp.dot(xs, qw_ref[h], preferred_element_type=F32).astype(BF16)
        kh = jnp.dot(xs, kw_ref[h], preferred_element_type=F32) * kscale
        k_ref[:, h * dqk:(h + 1) * dqk] = kh.astype(BF16)
        v_ref[:, h * dv:(h + 1) * dv] = jnp.dot(xm[:, h * dh:(h + 1) * dh], vw_ref[h],
                                                 preferred_element_type=F32).astype(BF16)
    g = (jnp.dot(xcb, gw_ref[:inner, :], preferred_element_type=F32)
         + jnp.dot(xm, gw_ref[inner:, :], preferred_element_type=F32) + gb_ref[...])
    lane = lax.broadcasted_iota(jnp.int32, g.shape, 1)
    is_forget = ((lane >= heads) & (lane < 2 * heads)) | ((lane >= 3 * heads) & (lane < 4 * heads))
    logsig = jnp.minimum(g, 0.0) - jnp.log(1.0 + jnp.exp(-jnp.abs(g)))
    g = jnp.where(is_forget, logsig, g)
    g_ref[...] = g
    gt_ref[...] = g.T[:gt_ref.shape[0], :]


def _mlstm_pre(xo, conv_w, conv_b, q_w, k_w, v_w, gate_w, gate_b):
    L = xo.shape[0]
    heads, dh, dqk = q_w.shape
    dv = v_w.shape[2]
    inner = heads * dh
    tm = min(256, L)
    ng = 4 * heads
    gw = jnp.zeros((2 * inner, LANES), BF16).at[:, :ng].set(gate_w.astype(BF16))
    gb = jnp.zeros((1, LANES), F32).at[0, :ng].set(gate_b)
    prev, nxt = _halo_specs(tm, L, inner, 0)
    rows = lambda c: pl.BlockSpec((tm, c), lambda i: (i, 0))
    return pl.pallas_call(
        functools.partial(_mlstm_pre_kernel, heads=heads, kscale=dqk ** -0.5),
        grid=(L // tm,),
        in_specs=[rows(inner), prev, nxt, _full((3, inner)), _full((1, inner)),
                  _full(q_w.shape), _full(k_w.shape), _full(v_w.shape), _full(gw.shape), _full(gb.shape)],
        out_specs=[rows(inner), rows(heads * dqk), rows(heads * dqk), rows(heads * dv), rows(LANES),
                   pl.BlockSpec((ng, tm), lambda i: (0, i))],
        out_shape=[jax.ShapeDtypeStruct((L, inner), BF16), jax.ShapeDtypeStruct((L, heads * dqk), BF16),
                   jax.ShapeDtypeStruct((L, heads * dqk), BF16), jax.ShapeDtypeStruct((L, heads * dv), BF16),
                   jax.ShapeDtypeStruct((L, LANES), F32), jax.ShapeDtypeStruct((ng, L), F32)],
        compiler_params=_cparams("parallel"),
        name="mlstm_pre",
    )(xo, xo, xo, conv_w, conv_b.reshape(1, inner), q_w.astype(BF16), k_w.astype(BF16), v_w.astype(BF16), gw, gb)


def _scan_kernel(*refs, heads, rev, final):
    if final:
        (q_ref, k_ref, v_ref, g_ref, gt_ref, c0_ref, n0_ref, m0_ref, hf_ref, xc_ref, o_ref, ng_ref, sk_ref,
         out_ref, cT_ref, nT_ref, mT_ref, C_sc, n_sc, m_sc) = refs
    else:
        (q_ref, k_ref, v_ref, g_ref, gt_ref, c0_ref, n0_ref, m0_ref,
         out_ref, cT_ref, nT_ref, mT_ref, C_sc, n_sc, m_sc) = refs
    ci = pl.program_id(0)

    @pl.when(ci == 0)
    def _():
        C_sc[...] = c0_ref[...]
        n_sc[...] = n0_ref[...]
        m_sc[...] = m0_ref[...]

    Lc = q_ref.shape[0]
    dqk = q_ref.shape[1] // heads
    dv = v_ref.shape[1] // heads
    G = g_ref[...]
    GT = gt_ref[...]
    r = lax.broadcasted_iota(jnp.int32, (Lc, Lc), 0)
    c = lax.broadcasted_iota(jnp.int32, (Lc, Lc), 1)
    mask = (c >= r) if rev else (c <= r)
    tri = jnp.where(mask, 1.0, 0.0)
    Bc = jnp.dot(tri, G, precision=HI, preferred_element_type=F32)
    BT = lax.dot_general(GT, tri, (((1,), (1,)), ((), ())), precision=HI, preferred_element_type=F32)
    tot = Bc[0:1, :] if rev else Bc[Lc - 1:Lc, :]
    base = 2 * heads if rev else 0
    for h in range(heads):
        ci_, cf_ = base + h, base + heads + h
        b_col = Bc[:, cf_:cf_ + 1]
        i_col = G[:, ci_:ci_ + 1]
        bT = BT[cf_:cf_ + 1, :]
        iT = GT[ci_:ci_ + 1, :]
        b_last = tot[:, cf_:cf_ + 1]
        m = m_sc[h][:, 0:1]
        logd = jnp.where(mask, b_col - bT + iT, -jnp.inf)
        inter = b_col + m
        mj = jnp.maximum(inter, jnp.max(logd, axis=-1, keepdims=True))
        dmat = jnp.exp(logd - mj)
        qh = q_ref[:, h * dqk:(h + 1) * dqk]
        kh = k_ref[:, h * dqk:(h + 1) * dqk]
        vh = v_ref[:, h * dv:(h + 1) * dv]
        s = lax.dot_general(qh, kh, (((1,), (1,)), ((), ())), preferred_element_type=F32) * dmat
        w_inter = jnp.exp(inter - mj)
        Ct = C_sc[h]
        nrow = n_sc[h]
        num = (jnp.dot(s.astype(BF16), vh, preferred_element_type=F32)
               + w_inter * jnp.dot(qh, Ct.astype(BF16), preferred_element_type=F32))
        den = (jnp.sum(s, axis=-1, keepdims=True)
               + w_inter * jnp.sum(qh.astype(F32) * nrow, axis=-1, keepdims=True))
        hout = num / jnp.maximum(jnp.abs(den), jnp.exp(-mj))
        g_col = b_last - b_col + i_col
        g_row = b_last - bT + iT
        m_new = jnp.maximum(b_last + m, jnp.max(g_row, axis=-1, keepdims=True))
        decay = jnp.exp(b_last + m - m_new)
        wg = jnp.exp(g_col - m_new)
        vw = (vh.astype(F32) * wg).astype(BF16)
        C_sc[h] = decay * Ct + lax.dot_general(kh, vw, (((0,), (0,)), ((), ())), preferred_element_type=F32)
        n_sc[h] = decay * nrow + jnp.sum(kh.astype(F32) * wg, axis=0, keepdims=True)
        m_sc[h] = jnp.broadcast_to(m_new, m_sc.shape[1:])
        cs = slice(h * dv, (h + 1) * dv)
        if final:
            hs = hf_ref[:, cs] + hout
            mu = jnp.mean(hs, axis=-1, keepdims=True)
            d = hs - mu
            var = jnp.mean(d * d, axis=-1, keepdims=True)
            hn = d * lax.rsqrt(var + EPS)
            y = (hn * ng_ref[:, cs] + sk_ref[:, cs] * xc_ref[:, cs].astype(F32)) * _sigmoid(o_ref[:, cs].astype(F32))
            out_ref[:, cs] = y.astype(out_ref.dtype)
        else:
            out_ref[:, cs] = hout

    @pl.when(ci == pl.num_programs(0) - 1)
    def _():
        cT_ref[...] = C_sc[...]
        nT_ref[...] = n_sc[...]
        mT_ref[...] = m_sc[...]


def _mlstm_scan(q, k, v, g, gt, state, rev, final_args=None):
    L = q.shape[0]
    heads = M_HEADS
    dqk = q.shape[1] // heads
    dv = v.shape[1] // heads
    Lc = min(256, L)
    nc = L // Lc
    idx = (lambda i: (nc - 1 - i, 0)) if rev else (lambda i: (i, 0))
    idx_t = (lambda i: (0, nc - 1 - i)) if rev else (lambda i: (0, i))
    rows = lambda c, cb=0: pl.BlockSpec((Lc, c), lambda i: (idx(i)[0], cb))
    c0, n0, m0 = state
    final = final_args is not None
    in_specs = [rows(heads * dqk), rows(heads * dqk), rows(heads * dv), rows(LANES),
                pl.BlockSpec((gt.shape[0], Lc), idx_t), _full(c0.shape), _full(n0.shape), _full(m0.shape)]
    args = [q, k, v, g, gt, c0, n0, m0]
    if final:
        hf, xc, xo, norm_g, skip = final_args
        inner = heads * dv
        in_specs += [rows(inner), rows(inner), rows(inner, 1), _full((1, inner)), _full((1, inner))]
        args += [hf, xc, xo, norm_g.reshape(1, inner), skip.reshape(1, inner)]
    out_dtype = BF16 if final else F32
    return pl.pallas_call(
        functools.partial(_scan_kernel, heads=heads, rev=rev, final=final),
        grid=(nc,),
        in_specs=in_specs,
        out_specs=[rows(heads * dv), _full(c0.shape), _full(n0.shape), _full(m0.shape)],
        out_shape=[jax.ShapeDtypeStruct((L, heads * dv), out_dtype), jax.ShapeDtypeStruct(c0.shape, F32),
                   jax.ShapeDtypeStruct(n0.shape, F32), jax.ShapeDtypeStruct(m0.shape, F32)],
        scratch_shapes=[pltpu.VMEM(c0.shape, F32), pltpu.VMEM(n0.shape, F32), pltpu.VMEM(m0.shape, F32)],
        compiler_params=_cparams("arbitrary"),
        name="mlstm_scan_bwd" if rev else "mlstm_scan_fwd",
    )(*args)


def _mlstm_stream(xo, mp, st_f, st_b):
    (conv_w, conv_b, q_w, k_w, v_w, gate_w, gate_b, norm_g, skip) = mp
    xc, q, k, v, g, gt = _mlstm_pre(xo, conv_w, conv_b, q_w, k_w, v_w, gate_w, gate_b)
    hf, *sf = _mlstm_scan(q, k, v, g, gt, st_f, rev=False)
    y, *sb = _mlstm_scan(q, k, v, g, gt, st_b, rev=True, final_args=(hf, xc, xo, norm_g, skip))
    return y, tuple(sf), tuple(sb)


def _sconv_kernel(x_ref, p_ref, n_ref, w_ref, b_ref, o_ref):
    o_ref[...] = _conv3(x_ref[...].astype(F32), p_ref, n_ref, w_ref[...], b_ref[...]).astype(o_ref.dtype)


def _sconv(zp, w, b):
    L, C = zp.shape
    tm = min(256, L)
    tc = C // 3
    hb, nb = tm // HALO, L // HALO
    return pl.pallas_call(
        _sconv_kernel,
        grid=(L // tm, C // tc),
        in_specs=[pl.BlockSpec((tm, tc), lambda i, j: (i, j)),
                  pl.BlockSpec((HALO, tc), lambda i, j: (jnp.maximum(i * hb - 1, 0), j)),
                  pl.BlockSpec((HALO, tc), lambda i, j: (jnp.minimum((i + 1) * hb, nb - 1), j)),
                  pl.BlockSpec((3, tc), lambda i, j: (0, j)), pl.BlockSpec((1, tc), lambda i, j: (0, j))],
        out_specs=pl.BlockSpec((tm, tc), lambda i, j: (i, j)),
        out_shape=jax.ShapeDtypeStruct((L, C), BF16),
        compiler_params=_cparams("parallel", "parallel"),
        name="hyena_sconv",
    )(zp, zp, zp, w, b.reshape(1, C))


def _filt_kernel(w1_ref, b1_ref, fr_ref, w2_ref, b2_ref, w3_ref, dec_ref, f_ref, ss_ref, *, L):
    i = pl.program_id(0)
    tm = f_ref.shape[0]
    emb = w1_ref.shape[0]
    pos = (lax.broadcasted_iota(jnp.int32, (tm, LANES), 0) + i * tm).astype(F32)
    lane = lax.broadcasted_iota(jnp.int32, (tm, LANES), 1)
    band = jnp.where(lane <= H_BANDS, lane, lane - H_BANDS).astype(F32)
    ang = (2.0 * math.pi / L) * pos * band
    feat = jnp.where(lane == 0, pos / max(L - 1, 1),
                     jnp.where(lane <= H_BANDS, jnp.cos(ang), jnp.where(lane < emb, jnp.sin(ang), 0.0)))
    fr = fr_ref[...]
    z = jnp.sin(fr * (jnp.dot(feat, w1_ref[...], precision=HI, preferred_element_type=F32) + b1_ref[...]))
    z = jnp.sin(fr * (jnp.dot(z, w2_ref[...], precision=HI, preferred_element_type=F32) + b2_ref[...]))
    z = jnp.dot(z, w3_ref[...], precision=HI, preferred_element_type=F32)
    tau = jnp.abs(pos[:, 0:1] - (L // 2)) / (L / 2)
    filt = z * jnp.exp(-tau * dec_ref[...])
    f_ref[...] = filt

    @pl.when(i == 0)
    def _():
        ss_ref[...] = jnp.zeros_like(ss_ref)

    ss_ref[...] += jnp.sum(filt * filt, axis=0, keepdims=True)


def _hyena_filters(L, w1, b1, freq, w2, b2, w3, decay):
    emb, fh = w1.shape
    C2 = w3.shape[1]
    tm = min(256, L)
    w1p = jnp.zeros((LANES, fh), F32).at[:emb].set(w1)
    return pl.pallas_call(
        functools.partial(_filt_kernel, L=L),
        grid=(L // tm,),
        in_specs=[_full((LANES, fh)), _full((1, fh)), _full((1, fh)), _full((fh, fh)), _full((1, fh)),
                  _full((fh, C2)), _full((1, C2))],
        out_specs=[pl.BlockSpec((tm, C2), lambda i: (i, 0)), _full((1, C2))],
        out_shape=[jax.ShapeDtypeStruct((L, C2), F32), jax.ShapeDtypeStruct((1, C2), F32)],
        compiler_params=_cparams("arbitrary"),
        name="hyena_filters",
    )(w1p, b1.reshape(1, fh), freq.reshape(1, fh), w2, b2.reshape(1, fh), w3, decay.reshape(1, C2))


def _dft_tables(L):
    N = 2 * L
    N2 = FFT_N2
    N1 = N // N2
    k1 = jnp.arange(N1, dtype=jnp.int32)
    n1 = jnp.arange(N1 // 2, dtype=jnp.int32)
    n2 = jnp.arange(N2, dtype=jnp.int32)
    ph = (k1[None, :, None] * (N2 * n1[None, None, :] + n2[:, None, None])) % N
    ang = ph.astype(F32) * (2.0 * math.pi / N)
    ca, sa = jnp.cos(ang).astype(BF16), jnp.sin(ang).astype(BF16)
    cd, sd = jnp.swapaxes(ca, 1, 2), jnp.swapaxes(sa, 1, 2)
    ph2 = (n2[:, None] * n2[None, :]) % N2
    ang2 = ph2.astype(F32) * (2.0 * math.pi / N2)
    return (ca, sa), (cd, sd), (jnp.cos(ang2).astype(BF16), jnp.sin(ang2).astype(BF16))


def _fft_a_kernel(x_ref, c_ref, s_ref, ar_ref, ai_ref, *, rotate):
    for r in range(SUB):
        x = x_ref[:, r, :].astype(BF16)
        re = jnp.dot(c_ref[r], x, preferred_element_type=F32)
        im = -jnp.dot(s_ref[r], x, preferred_element_type=F32)
        if rotate:
            q = lax.broadcasted_iota(jnp.int32, re.shape, 0) & 3
            re, im = (jnp.where(q == 0, re, jnp.where(q == 1, -im, jnp.where(q == 2, -re, im))),
                      jnp.where(q == 0, im, jnp.where(q == 1, re, jnp.where(q == 2, -im, -re))))
        ar_ref[:, r, :] = re.astype(ar_ref.dtype)
        ai_ref[:, r, :] = im.astype(ai_ref.dtype)


def _fft_a(x, cblock, nblocks, CB, tabs, rotate):
    ca, sa = tabs
    N2, N1, H1 = ca.shape
    x3 = x.reshape(H1, N2, x.shape[1])
    out = jax.ShapeDtypeStruct((N1, N2, nblocks * CB), BF16)
    ospec = pl.BlockSpec((N1, SUB, CB), lambda j, cb: (0, j, cb))
    tspec = pl.BlockSpec((SUB, N1, H1), lambda j, cb: (j, 0, 0))
    return pl.pallas_call(
        functools.partial(_fft_a_kernel, rotate=rotate),
        grid=(N2 // SUB, nblocks),
        in_specs=[pl.BlockSpec((H1, SUB, CB), lambda j, cb: (0, j, cblock + cb)), tspec, tspec],
        out_specs=[ospec, ospec],
        out_shape=[out, out],
        compiler_params=_cparams("parallel", "parallel"),
        name="fft_stage_a",
    )(x3, ca, sa)


def _cmatmul(c, s, xr, xi, sign):
    rr = jnp.dot(c, xr, preferred_element_type=F32)
    si = jnp.dot(s, xi, preferred_element_type=F32)
    ri = jnp.dot(c, xi, preferred_element_type=F32)
    sr = jnp.dot(s, xr, preferred_element_type=F32)
    return rr - sign * si, ri + sign * sr


def _fft_h_kernel(ar_ref, ai_ref, c_ref, s_ref, sc_ref, hr_ref, hi_ref):
    xr, xi = _cmatmul(c_ref[...], s_ref[...], ar_ref[0], ai_ref[0], -1.0)
    hr_ref[0] = xr * sc_ref[...]
    hi_ref[0] = xi * sc_ref[...]


def _fft_h(ar, ai, tab2, scale):
    N1, N2, C = ar.shape
    blk = pl.BlockSpec((1, N2, C), lambda j: (j, 0, 0))
    out = jax.ShapeDtypeStruct((N1, N2, C), F32)
    return pl.pallas_call(
        _fft_h_kernel,
        grid=(N1,),
        in_specs=[blk, blk, _full((N2, N2)), _full((N2, N2)), _full((1, C))],
        out_specs=[blk, blk],
        out_shape=[out, out],
        compiler_params=_cparams("parallel"),
        name="fft_filter_spectrum",
    )(ar, ai, tab2[0], tab2[1], scale)


def _fft_b_kernel(ar_ref, ai_ref, hr_ref, hi_ref, c_ref, s_ref, pr_ref, pi_ref):
    c, s = c_ref[...], s_ref[...]
    xr, xi = _cmatmul(c, s, ar_ref[0], ai_ref[0], -1.0)
    hr, hi = hr_ref[0], hi_ref[0]
    yr = (xr * hr - xi * hi).astype(BF16)
    yi = (xr * hi + xi * hr).astype(BF16)
    pr, pi = _cmatmul(c, s, yr, yi, 1.0)
    pr_ref[0] = pr.astype(pr_ref.dtype)
    pi_ref[0] = pi.astype(pi_ref.dtype)


def _fft_b(ar, ai, hr, hi, hblock, tab2):
    N1, N2, C = ar.shape
    blk = pl.BlockSpec((1, N2, C), lambda j: (j, 0, 0))
    hblk = pl.BlockSpec((1, N2, C), lambda j: (j, 0, hblock))
    out = jax.ShapeDtypeStruct((N1, N2, C), BF16)
    return pl.pallas_call(
        _fft_b_kernel,
        grid=(N1,),
        in_specs=[blk, blk, hblk, hblk, _full((N2, N2)), _full((N2, N2))],
        out_specs=[blk, blk],
        out_shape=[out, out],
        compiler_params=_cparams("parallel"),
        name="fft_stage_b",
    )(ar, ai, hr, hi, tab2[0], tab2[1])


def _fft_d_kernel(pr_ref, pi_ref, c_ref, s_ref, g_ref, v_ref, fb_ref, o_ref):
    for r in range(SUB):
        y = (jnp.dot(c_ref[r], pr_ref[:, r, :], preferred_element_type=F32)
             - jnp.dot(s_ref[r], pi_ref[:, r, :], preferred_element_type=F32))
        v = v_ref[:, r, :].astype(F32)
        o_ref[:, r, :] = (g_ref[:, r, :].astype(F32) * (y + v * fb_ref[...])).astype(o_ref.dtype)


def _fft_d(pr, pi, tabs, gsrc, gblock, vsrc, vblock, fbias):
    cd, sd = tabs
    N2, H1, N1 = cd.shape
    C = pr.shape[2]
    L = H1 * N2
    pspec = pl.BlockSpec((N1, SUB, C), lambda j: (0, j, 0))
    tspec = pl.BlockSpec((SUB, H1, N1), lambda j: (j, 0, 0))
    colspec = lambda cb: pl.BlockSpec((H1, SUB, C), lambda j: (0, j, cb))
    out = pl.pallas_call(
        _fft_d_kernel,
        grid=(N2 // SUB,),
        in_specs=[pspec, pspec, tspec, tspec, colspec(gblock), colspec(vblock), _full((1, C))],
        out_specs=colspec(0),
        out_shape=jax.ShapeDtypeStruct((H1, N2, C), BF16),
        compiler_params=_cparams("parallel"),
        name="fft_stage_d",
    )(pr, pi, cd, sd, gsrc.reshape(H1, N2, gsrc.shape[1]), vsrc.reshape(H1, N2, vsrc.shape[1]), fbias)
    return out.reshape(L, C)


def _hyena_long(z, filt, ssq, fbias):
    L = z.shape[0]
    C = z.shape[1] // 3
    N = 2 * L
    taba, tabd, tab2 = _dft_tables(L)
    scale = lax.rsqrt(ssq + H_DECAY_EPS) * (1.0 / N)
    far, fai = _fft_a(filt, 0, 2, C, taba, rotate=True)
    hr, hi = _fft_h(far, fai, tab2, scale)
    fb = fbias.reshape(1, 2 * C)
    ar, ai = _fft_a(z, 2, 1, C, taba, rotate=False)
    pr, pi = _fft_b(ar, ai, hr, hi, 0, tab2)
    y1 = _fft_d(pr, pi, tabd, z, 0, z, 2, fb[:, :C])
    ar, ai = _fft_a(y1, 0, 1, C, taba, rotate=False)
    pr, pi = _fft_b(ar, ai, hr, hi, 1, tab2)
    return _fft_d(pr, pi, tabd, z, 1, y1, 0, fb[:, C:])


def _short_conv_kernel(z_ref, f_ref, ss_ref, fb_ref, cf_ref, sf_ref, cd_ref, sd_ref, o_ref):
    L = z_ref.shape[0]
    C = o_ref.shape[1]
    N = 2 * L

    def dft(xb):
        return (jnp.dot(cf_ref[...], xb, preferred_element_type=F32),
                -jnp.dot(sf_ref[...], xb, preferred_element_type=F32))

    def conv(u, blk, gate):
        sl = slice(blk * C, (blk + 1) * C)
        ur, ui = dft(u.astype(BF16))
        fr, fi = dft(f_ref[:, sl].astype(BF16))
        q = lax.broadcasted_iota(jnp.int32, fr.shape, 0) & 3
        fr, fi = (jnp.where(q == 0, fr, jnp.where(q == 1, -fi, jnp.where(q == 2, -fr, fi))),
                  jnp.where(q == 0, fi, jnp.where(q == 1, fr, jnp.where(q == 2, -fi, -fr))))
        yr = (ur * fr - ui * fi).astype(BF16)
        yi = (ur * fi + ui * fr).astype(BF16)
        y = (jnp.dot(cd_ref[...], yr, preferred_element_type=F32)
             - jnp.dot(sd_ref[...], yi, preferred_element_type=F32))
        y = y * (lax.rsqrt(ss_ref[:, sl] + H_DECAY_EPS) * (1.0 / N))
        return gate * (y + u * fb_ref[:, sl])

    x1 = z_ref[:, 0:C].astype(F32)
    x2 = z_ref[:, C:2 * C].astype(F32)
    v = z_ref[:, 2 * C:3 * C].astype(F32)
    o_ref[...] = conv(conv(v, 0, x1), 1, x2).astype(o_ref.dtype)


def _hyena_short(z, filt, ssq, fbias):
    L = z.shape[0]
    C = z.shape[1] // 3
    N = 2 * L
    k = jnp.arange(N, dtype=jnp.int32)
    n = jnp.arange(L, dtype=jnp.int32)
    ang = ((k[:, None] * n[None, :]) % N).astype(F32) * (2.0 * math.pi / N)
    cf, sf = jnp.cos(ang).astype(BF16), jnp.sin(ang).astype(BF16)
    args = (z, filt, ssq, fbias.reshape(1, 2 * C), cf, sf, cf.T, sf.T)
    return pl.pallas_call(
        _short_conv_kernel,
        grid=(1,),
        in_specs=[_full(a.shape) for a in args],
        out_specs=_full((L, C)),
        out_shape=jax.ShapeDtypeStruct((L, C), BF16),
        compiler_params=_cparams("arbitrary"),
        name="hyena_short_conv",
    )(*args)


def _router_kernel(x_ref, g_ref, mod_ref, rw_ref, rb_ref, c0_ref, xt_ref, rt_ref, cnt_ref, carry):
    i = pl.program_id(0)

    @pl.when(i == 0)
    def _():
        carry[...] = c0_ref[...]

    u = _norm_mod(x_ref[...], g_ref[...], mod_ref[...], 3)
    xt_ref[...] = u.astype(xt_ref.dtype)
    logits = jnp.dot(u, rw_ref[...], precision=HI, preferred_element_type=F32) + rb_ref[...]
    tm = u.shape[0]
    lane = lax.broadcasted_iota(jnp.int32, logits.shape, 1).astype(F32)
    ninf = -jnp.inf

    def top(vals):
        mx = jnp.max(vals, axis=-1, keepdims=True)
        ix = jnp.min(jnp.where(vals == mx, lane, float(LANES)), axis=-1, keepdims=True)
        return mx, ix

    lg = jnp.where(lane < N_GROUPS, logits, ninf)
    gmax, gidx = top(lg)
    g_w = 1.0 / jnp.sum(jnp.exp(lg - gmax), axis=-1, keepdims=True)
    lo = N_GROUPS + EXP_PER_GROUP * gidx
    le = jnp.where((lane >= lo) & (lane < lo + EXP_PER_GROUP), logits, ninf)
    m1, i1 = top(le)
    m2, i2 = top(jnp.where(lane == i1, ninf, le))
    r = jnp.exp(m2 - m1)
    w1 = g_w / (1.0 + r)
    w2 = w1 * r
    e1 = i1 - N_GROUPS
    e2 = i2 - N_GROUPS
    oh1 = lane == e1
    oh2 = lane == e2
    both = jnp.where(oh1 | oh2, 1.0, 0.0)
    rr = lax.broadcasted_iota(jnp.int32, (tm, tm), 0)
    cc = lax.broadcasted_iota(jnp.int32, (tm, tm), 1)
    earlier = jnp.where(cc < rr, 1.0, 0.0).astype(BF16)
    before = jnp.dot(earlier, both.astype(BF16), preferred_element_type=F32) + carry[...]
    rank1 = jnp.sum(jnp.where(oh1, before, 0.0), axis=-1, keepdims=True)
    rank2 = jnp.sum(jnp.where(oh2, before, 0.0), axis=-1, keepdims=True)
    carry[...] += jnp.sum(both, axis=0, keepdims=True)
    cnt_ref[...] = carry[...]
    out = jnp.zeros(logits.shape, F32)
    for j, col in enumerate((e1, e2, w1, w2, rank1, rank2)):
        out = jnp.where(lane == j, col, out)
    rt_ref[...] = out


def _router(h, g, mod, rw, rb, counts0):
    L, D = h.shape
    tm = min(256, L)
    return pl.pallas_call(
        _router_kernel,
        grid=(L // tm,),
        in_specs=[pl.BlockSpec((tm, D), lambda i: (i, 0)), _full((1, D)), _full(mod.shape),
                  _full(rw.shape), _full(rb.shape), _full((1, LANES))],
        out_specs=[pl.BlockSpec((tm, D), lambda i: (i, 0)), pl.BlockSpec((tm, LANES), lambda i: (i, 0)),
                   _full((1, LANES))],
        out_shape=[jax.ShapeDtypeStruct((L, D), BF16), jax.ShapeDtypeStruct((L, LANES), F32),
                   jax.ShapeDtypeStruct((1, LANES), F32)],
        scratch_shapes=[pltpu.VMEM((1, LANES), F32)],
        compiler_params=_cparams("arbitrary"),
        name="moe_router",
    )(h, g.reshape(1, D), mod, rw, rb, counts0)


def _expert_kernel(be_ref, x_ref, w13_ref, w2_ref, o_ref):
    de = w2_ref.shape[1]
    hcat = jnp.dot(x_ref[...], w13_ref[0], preferred_element_type=F32)
    a = (_silu(hcat[:, :de]) * hcat[:, de:]).astype(BF16)
    o_ref[...] = jnp.dot(a, w2_ref[0], preferred_element_type=F32).astype(o_ref.dtype)


def _experts(xb, blk_exp, w13, w2):
    R, D = xb.shape
    de = w2.shape[1]
    nb = R // MOE_BLOCK
    return pl.pallas_call(
        _expert_kernel,
        grid_spec=pltpu.PrefetchScalarGridSpec(
            num_scalar_prefetch=1,
            grid=(nb,),
            in_specs=[pl.BlockSpec((MOE_BLOCK, D), lambda b, be: (b, 0)),
                      pl.BlockSpec((1, D, 2 * de), lambda b, be: (be[b], 0, 0)),
                      pl.BlockSpec((1, de, D), lambda b, be: (be[b], 0, 0))],
            out_specs=pl.BlockSpec((MOE_BLOCK, D), lambda b, be: (b, 0))),
        out_shape=jax.ShapeDtypeStruct((R, D), BF16),
        compiler_params=_cparams("arbitrary"),
        name="moe_experts",
    )(blk_exp, xb, w13, w2)


def _combine_kernel(h_ref, y_ref, rt_ref, mod_ref, o_ref):
    D = h_ref.shape[1]
    rt = rt_ref[...]
    y = rt[:, 2:3] * y_ref[:, :D].astype(F32) + rt[:, 3:4] * y_ref[:, D:].astype(F32)
    o_ref[...] = h_ref[...] + mod_ref[5:6, :] * y


def _combine(h, y2, row0, route, mod):
    L, D = h.shape
    tm = min(256, L)
    off = row0 // tm
    return pl.pallas_call(
        _combine_kernel,
        grid=(L // tm,),
        in_specs=[pl.BlockSpec((tm, D), lambda i: (i, 0)), pl.BlockSpec((tm, 2 * D), lambda i: (i + off, 0)),
                  pl.BlockSpec((tm, LANES), lambda i: (i, 0)), _full(mod.shape)],
        out_specs=pl.BlockSpec((tm, D), lambda i: (i, 0)),
        out_shape=jax.ShapeDtypeStruct((L, D), F32),
        compiler_params=_cparams("parallel"),
        name="moe_combine",
    )(h, y2, route, mod)


def _moe(streams, g, rw, rb, w13, w2):
    D = streams[0][0].shape[1]
    counts = jnp.zeros((1, LANES), F32)
    xts, routes = [], []
    for h, mod in streams:
        xt, rt, counts = _router(h, g, mod, rw, rb, counts)
        xts.append(xt)
        routes.append(rt)
    xt = jnp.concatenate(xts, axis=0) if len(xts) > 1 else xts[0]
    route = jnp.concatenate(routes, axis=0) if len(routes) > 1 else routes[0]
    T = xt.shape[0]
    A = 2 * T
    cnt = counts[0, :N_EXPERTS].astype(jnp.int32)
    padded = (cnt + MOE_BLOCK - 1) // MOE_BLOCK * MOE_BLOCK
    p_ends = jnp.cumsum(padded)
    p_starts = p_ends - padded
    experts = route[:, 0:2].astype(jnp.int32)
    dest = (p_starts[experts] + route[:, 4:6].astype(jnp.int32)).reshape(A)
    n_blocks = -(-A // MOE_BLOCK) + N_EXPERTS
    R = n_blocks * MOE_BLOCK
    tok = jnp.arange(A, dtype=jnp.int32) // 2
    row_tok = jnp.zeros((R,), jnp.int32).at[dest].set(tok)
    blk_start = jnp.arange(n_blocks, dtype=jnp.int32) * MOE_BLOCK
    blk_exp = jnp.minimum(jnp.searchsorted(p_ends, blk_start, side='right'), N_EXPERTS - 1).astype(jnp.int32)
    yb = _experts(xt[row_tok], blk_exp, w13, w2)
    y2 = yb[dest].reshape(T, 2 * D)
    outs, row0 = [], 0
    for (h, mod), rt in zip(streams, routes):
        outs.append(_combine(h, y2, row0, rt, mod))
        row0 += h.shape[0]
    return outs


def _final_kernel(x_ref, g_ref, o_ref):
    x = x_ref[...]
    o_ref[...] = x * lax.rsqrt(jnp.mean(x * x, axis=-1, keepdims=True) + EPS) * g_ref[...]


def _final_norm(h, g):
    L, D = h.shape
    tm = min(512, L)
    return pl.pallas_call(
        _final_kernel,
        grid=(L // tm,),
        in_specs=[pl.BlockSpec((tm, D), lambda i: (i, 0)), _full((1, D))],
        out_specs=pl.BlockSpec((tm, D), lambda i: (i, 0)),
        out_shape=jax.ShapeDtypeStruct((L, D), F32),
        compiler_params=_cparams("parallel"),
        name="final_norm",
    )(h, g.reshape(1, D))


def _zero_state(heads, dqk, dv):
    return (jnp.zeros((heads, dqk, dv), F32), jnp.zeros((heads, 1, dqk), F32), jnp.zeros((heads, 1, LANES), F32))


def kernel(x, c, ctx, c_ctx, ada_w, ada_b, norm1_g, norm2_g, rg_w, rg_b, re_w, re_b, moe_w13, moe_w2, m_in_w, m_conv_w, m_conv_b, m_q_w, m_k_w, m_v_w, m_gate_w, m_gate_b, m_norm_g, m_skip, m_out_w, h_in_w, h_in_b, h_sc_w, h_sc_b, h_f_w1, h_f_b1, h_f_freq, h_f_w2, h_f_b2, h_f_w3, h_decay, h_fbias, h_out_w, h_out_b, final_g):
    B, S, D = x.shape
    assert B == 1, "kernel is written for the single-sequence problem shape"
    depth = ada_w.shape[0]
    hl, hc = x[0], ctx[0]
    cc = jnp.zeros((8, D), F32).at[0].set(c[0]).at[1].set(c_ctx)
    mods = _ada(cc, ada_w, ada_b).reshape(depth, 8, 6, D)
    pad8 = lambda m: jnp.concatenate([m, jnp.zeros((2, D), F32)], axis=0)
    nrt = N_GROUPS + N_EXPERTS
    zeros_d = jnp.zeros((D,), F32)

    for i in range(depth):
        kind, slot = i % N_MIXERS, i // N_MIXERS
        col_major = (slot % 2) == 1
        ctx_live = any(j % N_MIXERS == 0 for j in range(i + 1, depth))
        ml, mc = pad8(mods[i, 0]), pad8(mods[i, 1])

        if kind == 0:
            w_in = m_in_w[slot].astype(BF16)
            w_out = m_out_w[slot].astype(BF16)
            inner = w_out.shape[0]
            mp = (m_conv_w[slot], m_conv_b[slot], m_q_w[slot], m_k_w[slot], m_v_w[slot], m_gate_w[slot],
                  m_gate_b[slot], m_norm_g[slot], m_skip[slot])
            heads, _, dqk = m_q_w[slot].shape
            dv = m_v_w[slot].shape[2]
            zb = jnp.zeros((w_in.shape[1],), F32)
            xo_c = _norm_proj(hc, norm1_g[i], mc, w_in, zb, False)
            yc, st_f, st_b = _mlstm_stream(xo_c, mp, _zero_state(heads, dqk, dv), _zero_state(heads, dqk, dv))
            xo_l = _norm_proj(hl, norm1_g[i], ml, w_in, zb, col_major)
            yl, _, _ = _mlstm_stream(xo_l, mp, st_f, st_b)
            hl = _out_proj(yl, w_out, zeros_d, hl, ml, 2, col_major)
            if ctx_live:
                hc = _out_proj(yc, w_out, zeros_d, hc, mc, 2, False)
        else:
            w_in = h_in_w[slot].astype(BF16)
            w_out = h_out_w[slot].astype(BF16)
            fp = (h_f_w1[slot], h_f_b1[slot], h_f_freq[slot], h_f_w2[slot], h_f_b2[slot], h_f_w3[slot], h_decay[slot])
            zl = _sconv(_norm_proj(hl, norm1_g[i], ml, w_in, h_in_b[slot], col_major), h_sc_w[slot], h_sc_b[slot])
            filt, ssq = _hyena_filters(S, *fp)
            yl = _hyena_long(zl, filt, ssq, h_fbias[slot])
            hl = _out_proj(yl, w_out, h_out_b[slot], hl, ml, 2, col_major)
            if ctx_live:
                zc = _sconv(_norm_proj(hc, norm1_g[i], mc, w_in, h_in_b[slot], False), h_sc_w[slot], h_sc_b[slot])
                filt_c, ssq_c = _hyena_filters(hc.shape[0], *fp)
                yc = _hyena_short(zc, filt_c, ssq_c, h_fbias[slot])
                hc = _out_proj(yc, w_out, h_out_b[slot], hc, mc, 2, False)

        rw = jnp.zeros((D, LANES), F32).at[:, :N_GROUPS].set(rg_w[i]).at[:, N_GROUPS:nrt].set(re_w[i])
        rb = jnp.zeros((1, LANES), F32).at[0, :N_GROUPS].set(rg_b[i]).at[0, N_GROUPS:nrt].set(re_b[i])
        w13 = moe_w13[i].astype(BF16)
        w2 = moe_w2[i].astype(BF16)
        if ctx_live:
            hc, hl = _moe([(hc, mc), (hl, ml)], norm2_g[i], rw, rb, w13, w2)
        else:
            (hl,) = _moe([(hl, ml)], norm2_g[i], rw, rb, w13, w2)

    return _final_norm(hl, final_g)[None]
```

```python
import functools
import math

import jax
import jax.numpy as jnp
from jax import lax
from jax.experimental import pallas as pl
from jax.experimental.pallas import tpu as pltpu

F32 = jnp.float32
BF16 = jnp.bfloat16
U32 = jnp.uint32
HI = lax.Precision.HIGHEST

EPS = 1e-6
GRID_W = 64
N_MIXERS = 2
M_HEADS = 4
H_BANDS = 16
N_GROUPS = 4
EXP_PER_GROUP = 8
N_EXPERTS = N_GROUPS * EXP_PER_GROUP
MOE_BLOCK = 128
LANES = 128
HALO = 16
SUB = 8
FFT_N2 = 128
H_DECAY_EPS = EPS
VMEM_LIMIT = 56 * 1024 * 1024


def _cparams(*sem):
    return pltpu.CompilerParams(dimension_semantics=sem, vmem_limit_bytes=VMEM_LIMIT)


def _full(shape):
    n = len(shape)
    return pl.BlockSpec(shape, lambda *_: (0,) * n)


def _raster(L, tr_pref=128):
    rows = L // GRID_W
    return rows, min(tr_pref, rows)


def _cg_cat(ref):
    return jnp.concatenate([ref[s] for s in range(ref.shape[0])], axis=1)


def _cg_put(ref, val):
    for s in range(ref.shape[0]):
        ref[s] = val[:, s * LANES:(s + 1) * LANES]


def _cg_load(ref4, r):
    G, n = ref4.shape[0], ref4.shape[1]
    flat = ref4.reshape(G, n * SUB, LANES)
    return jnp.concatenate([flat[s, pl.ds(r, n, stride=SUB), :] for s in range(G)], axis=1)


def _cg_store(ref4, r, val):
    G, n = ref4.shape[0], ref4.shape[1]
    flat = ref4.reshape(G, n * SUB, LANES)
    for s in range(G):
        flat[s, pl.ds(r, n, stride=SUB), :] = val[:, s * LANES:(s + 1) * LANES]


def _pack(re, im):
    hi = pltpu.bitcast(re.astype(BF16).astype(F32), U32)
    lo = pltpu.bitcast(im.astype(BF16).astype(F32), U32)
    return hi | (lo >> 16)


def _unpack(w):
    re = pltpu.bitcast(w & jnp.uint32(0xFFFF0000), F32)
    im = pltpu.bitcast(w << 16, F32)
    return re.astype(BF16), im.astype(BF16)


def _sigmoid(x):
    return 1.0 / (1.0 + jnp.exp(-x))


def _silu(x):
    return x * _sigmoid(x)


def _norm_mod(x, g, mod, k):
    ms = jnp.mean(x * x, axis=-1, keepdims=True)
    y = x * lax.rsqrt(ms + EPS) * g
    return y * (1.0 + mod[k + 1:k + 2, :]) + mod[k:k + 1, :]


def _ada_kernel(c_ref, w_ref, b_ref, o_ref):
    s = _silu(c_ref[...])
    o_ref[0] = jnp.dot(s, w_ref[0], precision=HI, preferred_element_type=F32) + b_ref[0]


def _ada(cc, ada_w, ada_b):
    depth, D, N = ada_w.shape
    tn = N // 6
    return pl.pallas_call(
        _ada_kernel,
        grid=(depth, N // tn),
        in_specs=[pl.BlockSpec((8, D), lambda l, j: (0, 0)),
                  pl.BlockSpec((1, D, tn), lambda l, j: (l, 0, j)),
                  pl.BlockSpec((1, 1, tn), lambda l, j: (l, 0, j))],
        out_specs=pl.BlockSpec((1, 8, tn), lambda l, j: (l, 0, j)),
        out_shape=jax.ShapeDtypeStruct((depth, 8, N), F32),
        compiler_params=_cparams("parallel", "parallel"),
        name="ada",
    )(cc, ada_w, ada_b.reshape(depth, 1, N))


def _norm_proj_kernel(x_ref, g_ref, mod_ref, w_ref, b_ref, o_ref, *, nchunk, col_major):
    N = o_ref.shape[-1]
    step = N // nchunk
    for c in range(SUB if col_major else 1):
        x = x_ref[:, c, :] if col_major else x_ref[...]
        u = _norm_mod(x, g_ref[...], mod_ref[...], 0).astype(BF16)
        for j in range(nchunk):
            sl = slice(j * step, (j + 1) * step)
            acc = (jnp.dot(u, w_ref[:, sl], preferred_element_type=F32) + b_ref[:, sl]).astype(o_ref.dtype)
            if col_major:
                o_ref[c, :, sl] = acc
            else:
                o_ref[:, sl] = acc


def _norm_proj(x, g, mod, w, b, col_major):
    L, D = x.shape
    N = w.shape[1]
    consts = [_full((1, D)), _full(mod.shape), _full((D, N)), _full((1, N))]
    kern = functools.partial(_norm_proj_kernel, nchunk=max(1, N // 1024), col_major=col_major)
    args = (g.reshape(1, D), mod, w, b.reshape(1, N))
    if col_major:
        rows, tr = _raster(L)
        out = pl.pallas_call(
            kern,
            grid=(GRID_W // SUB, rows // tr),
            in_specs=[pl.BlockSpec((tr, SUB, D), lambda iw, ir: (ir, iw, 0))] + consts,
            out_specs=pl.BlockSpec((SUB, tr, N), lambda iw, ir: (iw, ir, 0)),
            out_shape=jax.ShapeDtypeStruct((GRID_W, rows, N), BF16),
            compiler_params=_cparams("parallel", "parallel"),
            name="norm_proj_colmajor",
        )(x.reshape(rows, GRID_W, D), *args)
        return out.reshape(L, N)
    tm = min(256, L)
    return pl.pallas_call(
        kern,
        grid=(L // tm,),
        in_specs=[pl.BlockSpec((tm, D), lambda i: (i, 0))] + consts,
        out_specs=pl.BlockSpec((tm, N), lambda i: (i, 0)),
        out_shape=jax.ShapeDtypeStruct((L, N), BF16),
        compiler_params=_cparams("parallel"),
        name="norm_proj",
    )(x, *args)


def _out_proj_kernel(y_ref, w_ref, b_ref, h_ref, mod_ref, o_ref, *, gate_row, col_major, y_cg):
    gate = mod_ref[gate_row:gate_row + 1, :]
    for c in range(SUB if col_major else 1):
        if y_cg:
            y = jnp.concatenate([y_ref[s, c] if col_major else y_ref[s] for s in range(y_ref.shape[0])],
                                axis=1).astype(BF16)
        else:
            y = y_ref[c] if col_major else y_ref[...]
        acc = jnp.dot(y, w_ref[...], preferred_element_type=F32) + b_ref[...]
        if col_major:
            o_ref[:, c, :] = h_ref[:, c, :] + gate * acc
        else:
            o_ref[...] = h_ref[...] + gate * acc


def _out_proj(y, w, b, h, mod, gate_row, col_major):
    L, D = h.shape
    K = w.shape[0]
    y_cg = y.ndim == 3
    G = K // LANES
    kern = functools.partial(_out_proj_kernel, gate_row=gate_row, col_major=col_major, y_cg=y_cg)
    if col_major:
        rows, tr = _raster(L)
        hspec = pl.BlockSpec((tr, SUB, D), lambda iw, ir: (ir, iw, 0))
        if y_cg:
            yspec = pl.BlockSpec((G, SUB, tr, LANES), lambda iw, ir: (0, iw, ir, 0))
            yv = y.reshape(G, GRID_W, rows, LANES)
        else:
            yspec = pl.BlockSpec((SUB, tr, K), lambda iw, ir: (iw, ir, 0))
            yv = y.reshape(GRID_W, rows, K)
        out = pl.pallas_call(
            kern,
            grid=(GRID_W // SUB, rows // tr),
            in_specs=[yspec, _full((K, D)), _full((1, D)), hspec, _full(mod.shape)],
            out_specs=hspec,
            out_shape=jax.ShapeDtypeStruct((rows, GRID_W, D), F32),
            compiler_params=_cparams("parallel", "parallel"),
            name="out_proj_colmajor",
        )(yv, w, b.reshape(1, D), h.reshape(rows, GRID_W, D), mod)
        return out.reshape(L, D)
    tm = min(256, L)
    rows_spec = lambda c: pl.BlockSpec((tm, c), lambda i: (i, 0))
    yspec = pl.BlockSpec((G, tm, LANES), lambda i: (0, i, 0)) if y_cg else rows_spec(K)
    return pl.pallas_call(
        kern,
        grid=(L // tm,),
        in_specs=[yspec, _full((K, D)), _full((1, D)), rows_spec(D), _full(mod.shape)],
        out_specs=rows_spec(D),
        out_shape=jax.ShapeDtypeStruct((L, D), F32),
        compiler_params=_cparams("parallel"),
        name="out_proj",
    )(y, w, b.reshape(1, D), h, mod)


def _halo_specs(tm, L, ncol, cblock):
    hb = tm // HALO
    nb = L // HALO
    prev = pl.BlockSpec((HALO, ncol), lambda i: (jnp.maximum(i * hb - 1, 0), cblock))
    nxt = pl.BlockSpec((HALO, ncol), lambda i: (jnp.minimum((i + 1) * hb, nb - 1), cblock))
    return prev, nxt


def _conv3(x, prev_ref, next_ref, w, b):
    tm = x.shape[0]
    i = pl.program_id(0)
    first = i == 0
    last = i == pl.num_programs(0) - 1
    p = jnp.where(first, 0.0, prev_ref[HALO - 1:HALO, :].astype(F32))
    n = jnp.where(last, 0.0, next_ref[0:1, :].astype(F32))
    row = lax.broadcasted_iota(jnp.int32, x.shape, 0)
    xm1 = jnp.where(row == 0, p, pltpu.roll(x, 1, 0))
    xp1 = jnp.where(row == tm - 1, n, pltpu.roll(x, tm - 1, 0))
    return b + xm1 * w[0:1, :] + x * w[1:2, :] + xp1 * w[2:3, :]


def _mlstm_pre_kernel(x_ref, p_ref, n_ref, cw_ref, cb_ref, qw_ref, kw_ref, vw_ref, gw_ref, gb_ref,
                      xc_ref, q_ref, k_ref, v_ref, g_ref, gt_ref, *, heads, kscale):
    xm = x_ref[...]
    xc = _silu(_conv3(xm.astype(F32), p_ref, n_ref, cw_ref[...], cb_ref[...]))
    xcb = xc.astype(BF16)
    xc_ref[...] = xcb
    inner = xm.shape[1]
    dh = inner // heads
    dqk = qw_ref.shape[2]
    dv = vw_ref.shape[2]
    for h in range(heads):
        xs = xcb[:, h * dh:(h + 1) * dh]
        q_ref[:, h * dqk:(h + 1) * dqk] = jnp.dot(xs, qw_ref[h], preferred_element_type=F32).astype(BF16)
        kh = jnp.dot(xs, kw_ref[h], preferred_element_type=F32) * kscale
        k_ref[:, h * dqk:(h + 1) * dqk] = kh.astype(BF16)
        v_ref[:, h * dv:(h + 1) * dv] = jnp.dot(xm[:, h * dh:(h + 1) * dh], vw_ref[h],
                                                 preferred_element_type=F32).astype(BF16)
    g = (jnp.dot(xcb, gw_ref[:inner, :], preferred_element_type=F32)
         + jnp.dot(xm, gw_ref[inner:, :], preferred_element_type=F32) + gb_ref[...])
    lane = lax.broadcasted_iota(jnp.int32, g.shape, 1)
    is_forget = ((lane >= heads) & (lane < 2 * heads)) | ((lane >= 3 * heads) & (lane < 4 * heads))
    logsig = jnp.minimum(g, 0.0) - jnp.log(1.0 + jnp.exp(-jnp.abs(g)))
    g = jnp.where(is_forget, logsig, g)
    g_ref[...] = g
    gt_ref[...] = g.T[:gt_ref.shape[0], :]


def _mlstm_pre(xo, conv_w, conv_b, q_w, k_w, v_w, gate_w, gate_b):
    L = xo.shape[0]
    heads, dh, dqk = q_w.shape
    dv = v_w.shape[2]
    inner = heads * dh
    tm = min(256, L)
    ng = 4 * heads
    gw = jnp.zeros((2 * inner, LANES), BF16).at[:, :ng].set(gate_w.astype(BF16))
    gb = jnp.zeros((1, LANES), F32).at[0, :ng].set(gate_b)
    prev, nxt = _halo_specs(tm, L, inner, 0)
    rows = lambda c: pl.BlockSpec((tm, c), lambda i: (i, 0))
    return pl.pallas_call(
        functools.partial(_mlstm_pre_kernel, heads=heads, kscale=dqk ** -0.5),
        grid=(L // tm,),
        in_specs=[rows(inner), prev, nxt, _full((3, inner)), _full((1, inner)),
                  _full(q_w.shape), _full(k_w.shape), _full(v_w.shape), _full(gw.shape), _full(gb.shape)],
        out_specs=[rows(inner), rows(heads * dqk), rows(heads * dqk), rows(heads * dv), rows(LANES),
                   pl.BlockSpec((ng, tm), lambda i: (0, i))],
        out_shape=[jax.ShapeDtypeStruct((L, inner), BF16), jax.ShapeDtypeStruct((L, heads * dqk), BF16),
                   jax.ShapeDtypeStruct((L, heads * dqk), BF16), jax.ShapeDtypeStruct((L, heads * dv), BF16),
                   jax.ShapeDtypeStruct((L, LANES), F32), jax.ShapeDtypeStruct((ng, L), F32)],
        compiler_params=_cparams("parallel"),
        name="mlstm_pre",
    )(xo, xo, xo, conv_w, conv_b.reshape(1, inner), q_w.astype(BF16), k_w.astype(BF16), v_w.astype(BF16), gw, gb)


def _scan_kernel(*refs, heads, rev, final):
    if final:
        (q_ref, k_ref, v_ref, g_ref, gt_ref, c0_ref, n0_ref, m0_ref, hf_ref, xc_ref, o_ref, ng_ref, sk_ref,
         out_ref, cT_ref, nT_ref, mT_ref, C_sc, n_sc, m_sc) = refs
    else:
        (q_ref, k_ref, v_ref, g_ref, gt_ref, c0_ref, n0_ref, m0_ref,
         out_ref, cT_ref, nT_ref, mT_ref, C_sc, n_sc, m_sc) = refs
    ci = pl.program_id(0)

    @pl.when(ci == 0)
    def _():
        C_sc[...] = c0_ref[...]
        n_sc[...] = n0_ref[...]
        m_sc[...] = m0_ref[...]

    Lc = q_ref.shape[0]
    dqk = q_ref.shape[1] // heads
    dv = v_ref.shape[1] // heads
    G = g_ref[...]
    GT = gt_ref[...]
    r = lax.broadcasted_iota(jnp.int32, (Lc, Lc), 0)
    c = lax.broadcasted_iota(jnp.int32, (Lc, Lc), 1)
    mask = (c >= r) if rev else (c <= r)
    tri = jnp.where(mask, 1.0, 0.0)
    Bc = jnp.dot(tri, G, precision=HI, preferred_element_type=F32)
    BT = lax.dot_general(GT, tri, (((1,), (1,)), ((), ())), precision=HI, preferred_element_type=F32)
    tot = Bc[0:1, :] if rev else Bc[Lc - 1:Lc, :]
    base = 2 * heads if rev else 0
    for h in range(heads):
        ci_, cf_ = base + h, base + heads + h
        b_col = Bc[:, cf_:cf_ + 1]
        i_col = G[:, ci_:ci_ + 1]
        bT = BT[cf_:cf_ + 1, :]
        iT = GT[ci_:ci_ + 1, :]
        b_last = tot[:, cf_:cf_ + 1]
        m = m_sc[h][:, 0:1]
        logd = jnp.where(mask, b_col - bT + iT, -jnp.inf)
        inter = b_col + m
        mj = jnp.maximum(inter, jnp.max(logd, axis=-1, keepdims=True))
        dmat = jnp.exp(logd - mj)
        qh = q_ref[:, h * dqk:(h + 1) * dqk]
        kh = k_ref[:, h * dqk:(h + 1) * dqk]
        vh = v_ref[:, h * dv:(h + 1) * dv]
        s = lax.dot_general(qh, kh, (((1,), (1,)), ((), ())), preferred_element_type=F32) * dmat
        w_inter = jnp.exp(inter - mj)
        Ct = C_sc[h]
        nrow = n_sc[h]
        num = (jnp.dot(s.astype(BF16), vh, preferred_element_type=F32)
               + w_inter * jnp.dot(qh, Ct.astype(BF16), preferred_element_type=F32))
        den = (jnp.sum(s, axis=-1, keepdims=True)
               + w_inter * jnp.sum(qh.astype(F32) * nrow, axis=-1, keepdims=True))
        hout = num / jnp.maximum(jnp.abs(den), jnp.exp(-mj))
        g_col = b_last - b_col + i_col
        g_row = b_last - bT + iT
        m_new = jnp.maximum(b_last + m, jnp.max(g_row, axis=-1, keepdims=True))
        decay = jnp.exp(b_last + m - m_new)
        wg = jnp.exp(g_col - m_new)
        vw = (vh.astype(F32) * wg).astype(BF16)
        C_sc[h] = decay * Ct + lax.dot_general(kh, vw, (((0,), (0,)), ((), ())), preferred_element_type=F32)
        n_sc[h] = decay * nrow + jnp.sum(kh.astype(F32) * wg, axis=0, keepdims=True)
        m_sc[h] = jnp.broadcast_to(m_new, m_sc.shape[1:])
        cs = slice(h * dv, (h + 1) * dv)
        if final:
            hs = hf_ref[:, cs] + hout
            mu = jnp.mean(hs, axis=-1, keepdims=True)
            d = hs - mu
            var = jnp.mean(d * d, axis=-1, keepdims=True)
            hn = d * lax.rsqrt(var + EPS)
            y = (hn * ng_ref[:, cs] + sk_ref[:, cs] * xc_ref[:, cs].astype(F32)) * _sigmoid(o_ref[:, cs].astype(F32))
            out_ref[:, cs] = y.astype(out_ref.dtype)
        else:
            out_ref[:, cs] = hout

    @pl.when(ci == pl.num_programs(0) - 1)
    def _():
        cT_ref[...] = C_sc[...]
        nT_ref[...] = n_sc[...]
        mT_ref[...] = m_sc[...]


def _mlstm_scan(q, k, v, g, gt, state, rev, final_args=None):
    L = q.shape[0]
    heads = M_HEADS
    dqk = q.shape[1] // heads
    dv = v.shape[1] // heads
    Lc = min(256, L)
    nc = L // Lc
    idx = (lambda i: (nc - 1 - i, 0)) if rev else (lambda i: (i, 0))
    idx_t = (lambda i: (0, nc - 1 - i)) if rev else (lambda i: (0, i))
    rows = lambda c, cb=0: pl.BlockSpec((Lc, c), lambda i: (idx(i)[0], cb))
    c0, n0, m0 = state
    final = final_args is not None
    in_specs = [rows(heads * dqk), rows(heads * dqk), rows(heads * dv), rows(LANES),
                pl.BlockSpec((gt.shape[0], Lc), idx_t), _full(c0.shape), _full(n0.shape), _full(m0.shape)]
    args = [q, k, v, g, gt, c0, n0, m0]
    if final:
        hf, xc, xo, norm_g, skip = final_args
        inner = heads * dv
        in_specs += [rows(inner), rows(inner), rows(inner, 1), _full((1, inner)), _full((1, inner))]
        args += [hf, xc, xo, norm_g.reshape(1, inner), skip.reshape(1, inner)]
    out_dtype = BF16 if final else F32
    return pl.pallas_call(
        functools.partial(_scan_kernel, heads=heads, rev=rev, final=final),
        grid=(nc,),
        in_specs=in_specs,
        out_specs=[rows(heads * dv), _full(c0.shape), _full(n0.shape), _full(m0.shape)],
        out_shape=[jax.ShapeDtypeStruct((L, heads * dv), out_dtype), jax.ShapeDtypeStruct(c0.shape, F32),
                   jax.ShapeDtypeStruct(n0.shape, F32), jax.ShapeDtypeStruct(m0.shape, F32)],
        scratch_shapes=[pltpu.VMEM(c0.shape, F32), pltpu.VMEM(n0.shape, F32), pltpu.VMEM(m0.shape, F32)],
        compiler_params=_cparams("arbitrary"),
        name="mlstm_scan_bwd" if rev else "mlstm_scan_fwd",
    )(*args)


def _mlstm_stream(xo, mp, st_f, st_b):
    (conv_w, conv_b, q_w, k_w, v_w, gate_w, gate_b, norm_g, skip) = mp
    xc, q, k, v, g, gt = _mlstm_pre(xo, conv_w, conv_b, q_w, k_w, v_w, gate_w, gate_b)
    hf, *sf = _mlstm_scan(q, k, v, g, gt, st_f, rev=False)
    y, *sb = _mlstm_scan(q, k, v, g, gt, st_b, rev=True, final_args=(hf, xc, xo, norm_g, skip))
    return y, tuple(sf), tuple(sb)


def _sconv_kernel(x_ref, p_ref, n_ref, w_ref, b_ref, o_ref):
    _cg_put(o_ref, _conv3(x_ref[...].astype(F32), p_ref, n_ref, w_ref[...], b_ref[...]))


def _sconv(zp, w, b):
    L, C = zp.shape
    tm = min(256, L)
    tc = C // 3
    hb, nb = tm // HALO, L // HALO
    return pl.pallas_call(
        _sconv_kernel,
        grid=(L // tm, C // tc),
        in_specs=[pl.BlockSpec((tm, tc), lambda i, j: (i, j)),
                  pl.BlockSpec((HALO, tc), lambda i, j: (jnp.maximum(i * hb - 1, 0), j)),
                  pl.BlockSpec((HALO, tc), lambda i, j: (jnp.minimum((i + 1) * hb, nb - 1), j)),
                  pl.BlockSpec((3, tc), lambda i, j: (0, j)), pl.BlockSpec((1, tc), lambda i, j: (0, j))],
        out_specs=pl.BlockSpec((tc // LANES, tm, LANES), lambda i, j: (j, i, 0)),
        out_shape=jax.ShapeDtypeStruct((C // LANES, L, LANES), F32),
        compiler_params=_cparams("parallel", "parallel"),
        name="hyena_sconv",
    )(zp, zp, zp, w, b.reshape(1, C))


def _filt_kernel(w1_ref, b1_ref, fr_ref, w2_ref, b2_ref, w3_ref, dec_ref, f_ref, ss_ref, *, L):
    i = pl.program_id(0)
    tm = f_ref.shape[1]
    emb = w1_ref.shape[0]
    pos = (lax.broadcasted_iota(jnp.int32, (tm, LANES), 0) + i * tm).astype(F32)
    lane = lax.broadcasted_iota(jnp.int32, (tm, LANES), 1)
    band = jnp.where(lane <= H_BANDS, lane, lane - H_BANDS).astype(F32)
    ang = (2.0 * math.pi / L) * pos * band
    feat = jnp.where(lane == 0, pos / max(L - 1, 1),
                     jnp.where(lane <= H_BANDS, jnp.cos(ang), jnp.where(lane < emb, jnp.sin(ang), 0.0)))
    fr = fr_ref[...]
    z = jnp.sin(fr * (jnp.dot(feat, w1_ref[...], precision=HI, preferred_element_type=F32) + b1_ref[...]))
    z = jnp.sin(fr * (jnp.dot(z, w2_ref[...], precision=HI, preferred_element_type=F32) + b2_ref[...]))
    z = jnp.dot(z, w3_ref[...], precision=HI, preferred_element_type=F32)
    tau = jnp.abs(pos[:, 0:1] - (L // 2)) / (L / 2)
    filt = z * jnp.exp(-tau * dec_ref[...])
    _cg_put(f_ref, filt)

    @pl.when(i == 0)
    def _():
        ss_ref[...] = jnp.zeros_like(ss_ref)

    ss_ref[...] += jnp.sum(filt * filt, axis=0, keepdims=True)


def _hyena_filters(L, w1, b1, freq, w2, b2, w3, decay):
    emb, fh = w1.shape
    C2 = w3.shape[1]
    tm = min(256, L)
    w1p = jnp.zeros((LANES, fh), F32).at[:emb].set(w1)
    return pl.pallas_call(
        functools.partial(_filt_kernel, L=L),
        grid=(L // tm,),
        in_specs=[_full((LANES, fh)), _full((1, fh)), _full((1, fh)), _full((fh, fh)), _full((1, fh)),
                  _full((fh, C2)), _full((1, C2))],
        out_specs=[pl.BlockSpec((C2 // LANES, tm, LANES), lambda i: (0, i, 0)), _full((1, C2))],
        out_shape=[jax.ShapeDtypeStruct((C2 // LANES, L, LANES), F32), jax.ShapeDtypeStruct((1, C2), F32)],
        compiler_params=_cparams("arbitrary"),
        name="hyena_filters",
    )(w1p, b1.reshape(1, fh), freq.reshape(1, fh), w2, b2.reshape(1, fh), w3, decay.reshape(1, C2))


def _dft_tables(L):
    N = 2 * L
    N2 = FFT_N2
    N1 = N // N2
    k1 = jnp.arange(N1, dtype=jnp.int32)
    n1 = jnp.arange(N1 // 2, dtype=jnp.int32)
    n2 = jnp.arange(N2, dtype=jnp.int32)
    ph = (k1[None, :, None] * (N2 * n1[None, None, :] + n2[:, None, None])) % N
    ang = ph.astype(F32) * (2.0 * math.pi / N)
    ca, sa = jnp.cos(ang).astype(BF16), jnp.sin(ang).astype(BF16)
    cd, sd = jnp.swapaxes(ca, 1, 2), jnp.swapaxes(sa, 1, 2)
    ph2 = (n2[:, None] * n2[None, :]) % N2
    ang2 = ph2.astype(F32) * (2.0 * math.pi / N2)
    return (ca, sa), (cd, sd), (jnp.cos(ang2).astype(BF16), jnp.sin(ang2).astype(BF16))


def _fft_a_kernel(x_ref, c_ref, s_ref, a_ref, *, rotate):
    for r in range(SUB):
        x = _cg_load(x_ref, r).astype(BF16)
        re = jnp.dot(c_ref[r], x, preferred_element_type=F32)
        im = -jnp.dot(s_ref[r], x, preferred_element_type=F32)
        if rotate:
            q = lax.broadcasted_iota(jnp.int32, re.shape, 0) & 3
            re, im = (jnp.where(q == 0, re, jnp.where(q == 1, -im, jnp.where(q == 2, -re, im))),
                      jnp.where(q == 0, im, jnp.where(q == 1, re, jnp.where(q == 2, -im, -re))))
        _cg_store(a_ref, r, _pack(re, im))


def _fft_a(x, gc, gblock, nblocks, tabs, rotate):
    ca, sa = tabs
    N2, N1, H1 = ca.shape
    x4 = x.reshape(x.shape[0], H1, N2, LANES)
    tspec = pl.BlockSpec((SUB, N1, H1), lambda j, cb: (j, 0, 0))
    return pl.pallas_call(
        functools.partial(_fft_a_kernel, rotate=rotate),
        grid=(N2 // SUB, nblocks),
        in_specs=[pl.BlockSpec((gc, H1, SUB, LANES), lambda j, cb: (gblock + cb, 0, j, 0)), tspec, tspec],
        out_specs=pl.BlockSpec((gc, N1, SUB, LANES), lambda j, cb: (cb, 0, j, 0)),
        out_shape=jax.ShapeDtypeStruct((gc * nblocks, N1, N2, LANES), U32),
        compiler_params=_cparams("parallel", "parallel"),
        name="fft_stage_a",
    )(x4, ca, sa)


def _cmatmul(c, s, xr, xi, sign):
    rr = jnp.dot(c, xr, preferred_element_type=F32)
    si = jnp.dot(s, xi, preferred_element_type=F32)
    ri = jnp.dot(c, xi, preferred_element_type=F32)
    sr = jnp.dot(s, xr, preferred_element_type=F32)
    return rr - sign * si, ri + sign * sr


def _fft_h_kernel(a_ref, c_ref, s_ref, sc_ref, hr_ref, hi_ref):
    ar, ai = _unpack(_cg_cat(a_ref))
    xr, xi = _cmatmul(c_ref[...], s_ref[...], ar, ai, -1.0)
    hr_ref[0] = xr * sc_ref[...]
    hi_ref[0] = xi * sc_ref[...]


def _fft_h(a, gc, tab2, scale):
    G, N1, N2, _ = a.shape
    nblocks = G // gc
    C = gc * LANES
    hblk = pl.BlockSpec((1, N2, C), lambda j, cb: (j, 0, cb))
    out = jax.ShapeDtypeStruct((N1, N2, nblocks * C), F32)
    return pl.pallas_call(
        _fft_h_kernel,
        grid=(N1, nblocks),
        in_specs=[pl.BlockSpec((gc, None, N2, LANES), lambda j, cb: (cb, j, 0, 0)), _full((N2, N2)), _full((N2, N2)),
                  pl.BlockSpec((1, C), lambda j, cb: (0, cb))],
        out_specs=[hblk, hblk],
        out_shape=[out, out],
        compiler_params=_cparams("parallel", "parallel"),
        name="fft_filter_spectrum",
    )(a, tab2[0], tab2[1], scale)


def _fft_b_kernel(a_ref, hr_ref, hi_ref, c_ref, s_ref, p_ref):
    c, s = c_ref[...], s_ref[...]
    ar, ai = _unpack(_cg_cat(a_ref))
    xr, xi = _cmatmul(c, s, ar, ai, -1.0)
    hr, hi = hr_ref[0], hi_ref[0]
    yr = (xr * hr - xi * hi).astype(BF16)
    yi = (xr * hi + xi * hr).astype(BF16)
    pr, pi = _cmatmul(c, s, yr, yi, 1.0)
    _cg_put(p_ref, _pack(pr, pi))


def _fft_b(a, hr, hi, hblock, tab2):
    G, N1, N2, _ = a.shape
    C = G * LANES
    blk = pl.BlockSpec((G, None, N2, LANES), lambda j: (0, j, 0, 0))
    hblk = pl.BlockSpec((1, N2, C), lambda j: (j, 0, hblock))
    return pl.pallas_call(
        _fft_b_kernel,
        grid=(N1,),
        in_specs=[blk, hblk, hblk, _full((N2, N2)), _full((N2, N2))],
        out_specs=blk,
        out_shape=jax.ShapeDtypeStruct(a.shape, U32),
        compiler_params=_cparams("parallel"),
        name="fft_stage_b",
    )(a, hr, hi, tab2[0], tab2[1])


def _fft_d_kernel(p_ref, c_ref, s_ref, g_ref, v_ref, fb_ref, o_ref):
    for r in range(SUB):
        pr, pi = _unpack(_cg_load(p_ref, r))
        y = (jnp.dot(c_ref[r], pr, preferred_element_type=F32)
             - jnp.dot(s_ref[r], pi, preferred_element_type=F32))
        v = _cg_load(v_ref, r)
        _cg_store(o_ref, r, _cg_load(g_ref, r) * (y + v * fb_ref[...]))


def _fft_d(p, tabs, gsrc, gblock, vsrc, vblock, fbias):
    cd, sd = tabs
    N2, H1, N1 = cd.shape
    G = p.shape[0]
    L = H1 * N2
    tspec = pl.BlockSpec((SUB, H1, N1), lambda j: (j, 0, 0))
    sig = lambda gb: pl.BlockSpec((G, H1, SUB, LANES), lambda j: (gb, 0, j, 0))
    out = pl.pallas_call(
        _fft_d_kernel,
        grid=(N2 // SUB,),
        in_specs=[pl.BlockSpec((G, N1, SUB, LANES), lambda j: (0, 0, j, 0)), tspec, tspec, sig(gblock), sig(vblock),
                  _full((1, G * LANES))],
        out_specs=sig(0),
        out_shape=jax.ShapeDtypeStruct((G, H1, N2, LANES), F32),
        compiler_params=_cparams("parallel"),
        name="fft_stage_d",
    )(p, cd, sd, gsrc.reshape(gsrc.shape[0], H1, N2, LANES), vsrc.reshape(vsrc.shape[0], H1, N2, LANES), fbias)
    return out.reshape(G, L, LANES)


def _hyena_long(z, filt, ssq, fbias, tables):
    L = z.shape[1]
    gc = z.shape[0] // 3
    C = gc * LANES
    N = 2 * L
    taba, tabd, tab2 = tables
    scale = lax.rsqrt(ssq + H_DECAY_EPS) * (1.0 / N)
    hr, hi = _fft_h(_fft_a(filt, gc, 0, 2, taba, rotate=True), gc, tab2, scale)
    fb = fbias.reshape(1, 2 * C)
    p = _fft_b(_fft_a(z, gc, 2, 1, taba, rotate=False), hr, hi, 0, tab2)
    y1 = _fft_d(p, tabd, z, 0, z, 2, fb[:, :C])
    p = _fft_b(_fft_a(y1, gc, 0, 1, taba, rotate=False), hr, hi, 1, tab2)
    return _fft_d(p, tabd, z, 1, y1, 0, fb[:, C:])


def _short_conv_kernel(z_ref, f_ref, ss_ref, fb_ref, cf_ref, sf_ref, cd_ref, sd_ref, o_ref):
    L = z_ref.shape[1]
    G = o_ref.shape[0]
    C = G * LANES
    N = 2 * L

    def dft(xb):
        return (jnp.dot(cf_ref[...], xb, preferred_element_type=F32),
                -jnp.dot(sf_ref[...], xb, preferred_element_type=F32))

    def conv(u, blk, gate):
        sl = slice(blk * C, (blk + 1) * C)
        ur, ui = dft(u.astype(BF16))
        fr, fi = dft(_cg_cat(f_ref.at[blk * G:(blk + 1) * G]).astype(BF16))
        q = lax.broadcasted_iota(jnp.int32, fr.shape, 0) & 3
        fr, fi = (jnp.where(q == 0, fr, jnp.where(q == 1, -fi, jnp.where(q == 2, -fr, fi))),
                  jnp.where(q == 0, fi, jnp.where(q == 1, fr, jnp.where(q == 2, -fi, -fr))))
        yr = (ur * fr - ui * fi).astype(BF16)
        yi = (ur * fi + ui * fr).astype(BF16)
        y = (jnp.dot(cd_ref[...], yr, preferred_element_type=F32)
             - jnp.dot(sd_ref[...], yi, preferred_element_type=F32))
        y = y * (lax.rsqrt(ss_ref[:, sl] + H_DECAY_EPS) * (1.0 / N))
        return gate * (y + u * fb_ref[:, sl])

    x1 = _cg_cat(z_ref.at[0:G])
    x2 = _cg_cat(z_ref.at[G:2 * G])
    v = _cg_cat(z_ref.at[2 * G:3 * G])
    _cg_put(o_ref, conv(conv(v, 0, x1), 1, x2))


def _hyena_short(z, filt, ssq, fbias):
    L = z.shape[1]
    G = z.shape[0] // 3
    C = G * LANES
    N = 2 * L
    k = jnp.arange(N, dtype=jnp.int32)
    n = jnp.arange(L, dtype=jnp.int32)
    ang = ((k[:, None] * n[None, :]) % N).astype(F32) * (2.0 * math.pi / N)
    cf, sf = jnp.cos(ang).astype(BF16), jnp.sin(ang).astype(BF16)
    args = (z, filt, ssq, fbias.reshape(1, 2 * C), cf, sf, cf.T, sf.T)
    return pl.pallas_call(
        _short_conv_kernel,
        grid=(1,),
        in_specs=[_full(a.shape) for a in args],
        out_specs=_full((G, L, LANES)),
        out_shape=jax.ShapeDtypeStruct((G, L, LANES), F32),
        compiler_params=_cparams("arbitrary"),
        name="hyena_short_conv",
    )(*args)


def _router_kernel(x_ref, g_ref, mod_ref, rw_ref, rb_ref, c0_ref, xt_ref, rt_ref, cnt_ref, carry):
    i = pl.program_id(0)

    @pl.when(i == 0)
    def _():
        carry[...] = c0_ref[...]

    u = _norm_mod(x_ref[...], g_ref[...], mod_ref[...], 3)
    xt_ref[...] = u.astype(xt_ref.dtype)
    logits = jnp.dot(u, rw_ref[...], precision=HI, preferred_element_type=F32) + rb_ref[...]
    tm = u.shape[0]
    lane = lax.broadcasted_iota(jnp.int32, logits.shape, 1).astype(F32)
    ninf = -jnp.inf

    def top(vals):
        mx = jnp.max(vals, axis=-1, keepdims=True)
        ix = jnp.min(jnp.where(vals == mx, lane, float(LANES)), axis=-1, keepdims=True)
        return mx, ix

    lg = jnp.where(lane < N_GROUPS, logits, ninf)
    gmax, gidx = top(lg)
    g_w = 1.0 / jnp.sum(jnp.exp(lg - gmax), axis=-1, keepdims=True)
    lo = N_GROUPS + EXP_PER_GROUP * gidx
    le = jnp.where((lane >= lo) & (lane < lo + EXP_PER_GROUP), logits, ninf)
    m1, i1 = top(le)
    m2, i2 = top(jnp.where(lane == i1, ninf, le))
    r = jnp.exp(m2 - m1)
    w1 = g_w / (1.0 + r)
    w2 = w1 * r
    e1 = i1 - N_GROUPS
    e2 = i2 - N_GROUPS
    oh1 = lane == e1
    oh2 = lane == e2
    both = jnp.where(oh1 | oh2, 1.0, 0.0)
    rr = lax.broadcasted_iota(jnp.int32, (tm, tm), 0)
    cc = lax.broadcasted_iota(jnp.int32, (tm, tm), 1)
    earlier = jnp.where(cc < rr, 1.0, 0.0).astype(BF16)
    before = jnp.dot(earlier, both.astype(BF16), preferred_element_type=F32) + carry[...]
    rank1 = jnp.sum(jnp.where(oh1, before, 0.0), axis=-1, keepdims=True)
    rank2 = jnp.sum(jnp.where(oh2, before, 0.0), axis=-1, keepdims=True)
    carry[...] += jnp.sum(both, axis=0, keepdims=True)
    cnt_ref[...] = carry[...]
    out = jnp.zeros(logits.shape, F32)
    for j, col in enumerate((e1, e2, w1, w2, rank1, rank2)):
        out = jnp.where(lane == j, col, out)
    rt_ref[...] = out


def _router(h, g, mod, rw, rb, counts0):
    L, D = h.shape
    tm = min(256, L)
    return pl.pallas_call(
        _router_kernel,
        grid=(L // tm,),
        in_specs=[pl.BlockSpec((tm, D), lambda i: (i, 0)), _full((1, D)), _full(mod.shape),
                  _full(rw.shape), _full(rb.shape), _full((1, LANES))],
        out_specs=[pl.BlockSpec((tm, D), lambda i: (i, 0)), pl.BlockSpec((tm, LANES), lambda i: (i, 0)),
                   _full((1, LANES))],
        out_shape=[jax.ShapeDtypeStruct((L, D), BF16), jax.ShapeDtypeStruct((L, LANES), F32),
                   jax.ShapeDtypeStruct((1, LANES), F32)],
        scratch_shapes=[pltpu.VMEM((1, LANES), F32)],
        compiler_params=_cparams("arbitrary"),
        name="moe_router",
    )(h, g.reshape(1, D), mod, rw, rb, counts0)


def _expert_kernel(be_ref, x_ref, w13_ref, w2_ref, o_ref):
    de = w2_ref.shape[1]
    hcat = jnp.dot(x_ref[...], w13_ref[0], preferred_element_type=F32)
    a = (_silu(hcat[:, :de]) * hcat[:, de:]).astype(BF16)
    o_ref[...] = jnp.dot(a, w2_ref[0], preferred_element_type=F32).astype(o_ref.dtype)


def _experts(xb, blk_exp, w13, w2):
    R, D = xb.shape
    de = w2.shape[1]
    nb = R // MOE_BLOCK
    return pl.pallas_call(
        _expert_kernel,
        grid_spec=pltpu.PrefetchScalarGridSpec(
            num_scalar_prefetch=1,
            grid=(nb,),
            in_specs=[pl.BlockSpec((MOE_BLOCK, D), lambda b, be: (b, 0)),
                      pl.BlockSpec((1, D, 2 * de), lambda b, be: (be[b], 0, 0)),
                      pl.BlockSpec((1, de, D), lambda b, be: (be[b], 0, 0))],
            out_specs=pl.BlockSpec((MOE_BLOCK, D), lambda b, be: (b, 0))),
        out_shape=jax.ShapeDtypeStruct((R, D), BF16),
        compiler_params=_cparams("arbitrary"),
        name="moe_experts",
    )(blk_exp, xb, w13, w2)


def _combine_kernel(h_ref, y_ref, rt_ref, mod_ref, o_ref):
    D = h_ref.shape[1]
    rt = rt_ref[...]
    y = rt[:, 2:3] * y_ref[:, :D].astype(F32) + rt[:, 3:4] * y_ref[:, D:].astype(F32)
    o_ref[...] = h_ref[...] + mod_ref[5:6, :] * y


def _combine(h, y2, row0, route, mod):
    L, D = h.shape
    tm = min(256, L)
    off = row0 // tm
    return pl.pallas_call(
        _combine_kernel,
        grid=(L // tm,),
        in_specs=[pl.BlockSpec((tm, D), lambda i: (i, 0)), pl.BlockSpec((tm, 2 * D), lambda i: (i + off, 0)),
                  pl.BlockSpec((tm, LANES), lambda i: (i, 0)), _full(mod.shape)],
        out_specs=pl.BlockSpec((tm, D), lambda i: (i, 0)),
        out_shape=jax.ShapeDtypeStruct((L, D), F32),
        compiler_params=_cparams("parallel"),
        name="moe_combine",
    )(h, y2, route, mod)


def _moe(streams, g, rw, rb, w13, w2):
    D = streams[0][0].shape[1]
    counts = jnp.zeros((1, LANES), F32)
    xts, routes = [], []
    for h, mod in streams:
        xt, rt, counts = _router(h, g, mod, rw, rb, counts)
        xts.append(xt)
        routes.append(rt)
    xt = jnp.concatenate(xts, axis=0) if len(xts) > 1 else xts[0]
    route = jnp.concatenate(routes, axis=0) if len(routes) > 1 else routes[0]
    T = xt.shape[0]
    A = 2 * T
    cnt = counts[0, :N_EXPERTS].astype(jnp.int32)
    padded = (cnt + MOE_BLOCK - 1) // MOE_BLOCK * MOE_BLOCK
    p_ends = jnp.cumsum(padded)
    p_starts = p_ends - padded
    experts = route[:, 0:2].astype(jnp.int32)
    dest = (p_starts[experts] + route[:, 4:6].astype(jnp.int32)).reshape(A)
    n_blocks = -(-A // MOE_BLOCK) + N_EXPERTS
    R = n_blocks * MOE_BLOCK
    tok = jnp.arange(A, dtype=jnp.int32) // 2
    row_tok = jnp.zeros((R,), jnp.int32).at[dest].set(tok)
    blk_start = jnp.arange(n_blocks, dtype=jnp.int32) * MOE_BLOCK
    blk_exp = jnp.minimum(jnp.sum(p_ends[None, :] <= blk_start[:, None], axis=1), N_EXPERTS - 1).astype(jnp.int32)
    yb = _experts(xt[row_tok], blk_exp, w13, w2)
    y2 = yb[dest].reshape(T, 2 * D)
    outs, row0 = [], 0
    for (h, mod), rt in zip(streams, routes):
        outs.append(_combine(h, y2, row0, rt, mod))
        row0 += h.shape[0]
    return outs


def _final_kernel(x_ref, g_ref, o_ref):
    x = x_ref[...]
    o_ref[...] = x * lax.rsqrt(jnp.mean(x * x, axis=-1, keepdims=True) + EPS) * g_ref[...]


def _final_norm(h, g):
    L, D = h.shape
    tm = min(512, L)
    return pl.pallas_call(
        _final_kernel,
        grid=(L // tm,),
        in_specs=[pl.BlockSpec((tm, D), lambda i: (i, 0)), _full((1, D))],
        out_specs=pl.BlockSpec((tm, D), lambda i: (i, 0)),
        out_shape=jax.ShapeDtypeStruct((L, D), F32),
        compiler_params=_cparams("parallel"),
        name="final_norm",
    )(h, g.reshape(1, D))


def _zero_state(heads, dqk, dv):
    return (jnp.zeros((heads, dqk, dv), F32), jnp.zeros((heads, 1, dqk), F32), jnp.zeros((heads, 1, LANES), F32))


def kernel(x, c, ctx, c_ctx, ada_w, ada_b, norm1_g, norm2_g, rg_w, rg_b, re_w, re_b, moe_w13, moe_w2, m_in_w, m_conv_w, m_conv_b, m_q_w, m_k_w, m_v_w, m_gate_w, m_gate_b, m_norm_g, m_skip, m_out_w, h_in_w, h_in_b, h_sc_w, h_sc_b, h_f_w1, h_f_b1, h_f_freq, h_f_w2, h_f_b2, h_f_w3, h_decay, h_fbias, h_out_w, h_out_b, final_g):
    B, S, D = x.shape
    assert B == 1, "kernel is written for the single-sequence problem shape"
    depth = ada_w.shape[0]
    hl, hc = x[0], ctx[0]
    cc = jnp.zeros((8, D), F32).at[0].set(c[0]).at[1].set(c_ctx)
    mods = _ada(cc, ada_w, ada_b).reshape(depth, 8, 6, D)
    pad8 = lambda m: jnp.concatenate([m, jnp.zeros((2, D), F32)], axis=0)
    nrt = N_GROUPS + N_EXPERTS
    zeros_d = jnp.zeros((D,), F32)
    tables = _dft_tables(S) if depth > 1 else None

    for i in range(depth):
        kind, slot = i % N_MIXERS, i // N_MIXERS
        col_major = (slot % 2) == 1
        ctx_live = any(j % N_MIXERS == 0 for j in range(i + 1, depth))
        ml, mc = pad8(mods[i, 0]), pad8(mods[i, 1])

        if kind == 0:
            w_in = m_in_w[slot].astype(BF16)
            w_out = m_out_w[slot].astype(BF16)
            inner = w_out.shape[0]
            mp = (m_conv_w[slot], m_conv_b[slot], m_q_w[slot], m_k_w[slot], m_v_w[slot], m_gate_w[slot],
                  m_gate_b[slot], m_norm_g[slot], m_skip[slot])
            heads, _, dqk = m_q_w[slot].shape
            dv = m_v_w[slot].shape[2]
            zb = jnp.zeros((w_in.shape[1],), F32)
            xo_c = _norm_proj(hc, norm1_g[i], mc, w_in, zb, False)
            yc, st_f, st_b = _mlstm_stream(xo_c, mp, _zero_state(heads, dqk, dv), _zero_state(heads, dqk, dv))
            xo_l = _norm_proj(hl, norm1_g[i], ml, w_in, zb, col_major)
            yl, _, _ = _mlstm_stream(xo_l, mp, st_f, st_b)
            hl = _out_proj(yl, w_out, zeros_d, hl, ml, 2, col_major)
            if ctx_live:
                hc = _out_proj(yc, w_out, zeros_d, hc, mc, 2, False)
        else:
            w_in = h_in_w[slot].astype(BF16)
            w_out = h_out_w[slot].astype(BF16)
            fp = (h_f_w1[slot], h_f_b1[slot], h_f_freq[slot], h_f_w2[slot], h_f_b2[slot], h_f_w3[slot], h_decay[slot])
            zl = _sconv(_norm_proj(hl, norm1_g[i], ml, w_in, h_in_b[slot], col_major), h_sc_w[slot], h_sc_b[slot])
            filt, ssq = _hyena_filters(S, *fp)
            yl = _hyena_long(zl, filt, ssq, h_fbias[slot], tables)
            hl = _out_proj(yl, w_out, h_out_b[slot], hl, ml, 2, col_major)
            if ctx_live:
                zc = _sconv(_norm_proj(hc, norm1_g[i], mc, w_in, h_in_b[slot], False), h_sc_w[slot], h_sc_b[slot])
                filt_c, ssq_c = _hyena_filters(hc.shape[0], *fp)
                yc = _hyena_short(zc, filt_c, ssq_c, h_fbias[slot])
                hc = _out_proj(yc, w_out, h_out_b[slot], hc, mc, 2, False)

        rw = jnp.zeros((D, LANES), F32).at[:, :N_GROUPS].set(rg_w[i]).at[:, N_GROUPS:nrt].set(re_w[i])
        rb = jnp.zeros((1, LANES), F32).at[0, :N_GROUPS].set(rg_b[i]).at[0, N_GROUPS:nrt].set(re_b[i])
        w13 = moe_w13[i].astype(BF16)
        w2 = moe_w2[i].astype(BF16)
        if ctx_live:
            hc, hl = _moe([(hc, mc), (hl, ml)], norm2_g[i], rw, rb, w13, w2)
        else:
            (hl,) = _moe([(hl, ml)], norm2_g[i], rw, rb, w13, w2)

    return _final_norm(hl, final_g)[None]
```

```python
import functools
import math

import jax
import jax.numpy as jnp
from jax import lax
from jax.experimental import pallas as pl
from jax.experimental.pallas import tpu as pltpu

F32 = jnp.float32
BF16 = jnp.bfloat16
U32 = jnp.uint32
HI = lax.Precision.HIGHEST

EPS = 1e-6
GRID_W = 64
N_MIXERS = 2
M_HEADS = 4
H_BANDS = 16
N_GROUPS = 4
EXP_PER_GROUP = 8
N_EXPERTS = N_GROUPS * EXP_PER_GROUP
MOE_BLOCK = 256
LANES = 128
HALO = 16
SUB = 8
FFT_N2 = 128
FFT_KB = 2
H_DECAY_EPS = EPS
VMEM_LIMIT = 56 * 1024 * 1024


def _cparams(*sem):
    return pltpu.CompilerParams(dimension_semantics=sem, vmem_limit_bytes=VMEM_LIMIT)


def _full(shape):
    n = len(shape)
    return pl.BlockSpec(shape, lambda *_: (0,) * n)


def _raster(L, tr_pref=128):
    rows = L // GRID_W
    return rows, min(tr_pref, rows)


def _cg_cat(ref):
    return jnp.concatenate([ref[s] for s in range(ref.shape[0])], axis=1)


def _cg_put(ref, val):
    for s in range(ref.shape[0]):
        ref[s] = val[:, s * LANES:(s + 1) * LANES]


def _cg_load(ref4, r):
    G, n = ref4.shape[0], ref4.shape[1]
    flat = ref4.reshape(G, n * SUB, LANES)
    return jnp.concatenate([flat[s, pl.ds(r, n, stride=SUB), :] for s in range(G)], axis=1)


def _cg_store(ref4, r, val):
    G, n = ref4.shape[0], ref4.shape[1]
    flat = ref4.reshape(G, n * SUB, LANES)
    for s in range(G):
        flat[s, pl.ds(r, n, stride=SUB), :] = val[:, s * LANES:(s + 1) * LANES]


def _pack(re, im):
    hi = pltpu.bitcast(re.astype(BF16).astype(F32), U32)
    lo = pltpu.bitcast(im.astype(BF16).astype(F32), U32)
    return (hi & jnp.uint32(0xFFFF0000)) | (lo >> 16)


def _unpack(w):
    re = pltpu.bitcast(w & jnp.uint32(0xFFFF0000), F32)
    im = pltpu.bitcast(w << 16, F32)
    return re.astype(BF16), im.astype(BF16)


def _sigmoid(x):
    return 1.0 / (1.0 + jnp.exp(-x))


def _silu(x):
    return x * _sigmoid(x)


def _norm_mod(x, g, mod, k):
    ms = jnp.mean(x * x, axis=-1, keepdims=True)
    y = x * lax.rsqrt(ms + EPS) * g
    return y * (1.0 + mod[k + 1:k + 2, :]) + mod[k:k + 1, :]


def _ada_kernel(c_ref, w_ref, b_ref, o_ref):
    s = _silu(c_ref[...])
    o_ref[0] = jnp.dot(s, w_ref[0], precision=HI, preferred_element_type=F32) + b_ref[0]


def _ada(cc, ada_w, ada_b):
    depth, D, N = ada_w.shape
    tn = N // 6
    return pl.pallas_call(
        _ada_kernel,
        grid=(depth, N // tn),
        in_specs=[pl.BlockSpec((8, D), lambda l, j: (0, 0)),
                  pl.BlockSpec((1, D, tn), lambda l, j: (l, 0, j)),
                  pl.BlockSpec((1, 1, tn), lambda l, j: (l, 0, j))],
        out_specs=pl.BlockSpec((1, 8, tn), lambda l, j: (l, 0, j)),
        out_shape=jax.ShapeDtypeStruct((depth, 8, N), F32),
        compiler_params=_cparams("parallel", "parallel"),
        name="ada",
    )(cc, ada_w, ada_b.reshape(depth, 1, N))


def _norm_proj_kernel(x_ref, g_ref, mod_ref, w_ref, b_ref, o_ref, *, nchunk, col_major):
    N = o_ref.shape[-1]
    step = N // nchunk
    for c in range(SUB if col_major else 1):
        x = x_ref[:, c, :] if col_major else x_ref[...]
        u = _norm_mod(x, g_ref[...], mod_ref[...], 0).astype(BF16)
        for j in range(nchunk):
            sl = slice(j * step, (j + 1) * step)
            acc = (jnp.dot(u, w_ref[:, sl], preferred_element_type=F32) + b_ref[:, sl]).astype(o_ref.dtype)
            if col_major:
                o_ref[c, :, sl] = acc
            else:
                o_ref[:, sl] = acc


def _norm_proj(x, g, mod, w, b, col_major):
    L, D = x.shape
    N = w.shape[1]
    consts = [_full((1, D)), _full(mod.shape), _full((D, N)), _full((1, N))]
    kern = functools.partial(_norm_proj_kernel, nchunk=max(1, N // 1024), col_major=col_major)
    args = (g.reshape(1, D), mod, w, b.reshape(1, N))
    if col_major:
        rows, tr = _raster(L)
        out = pl.pallas_call(
            kern,
            grid=(GRID_W // SUB, rows // tr),
            in_specs=[pl.BlockSpec((tr, SUB, D), lambda iw, ir: (ir, iw, 0))] + consts,
            out_specs=pl.BlockSpec((SUB, tr, N), lambda iw, ir: (iw, ir, 0)),
            out_shape=jax.ShapeDtypeStruct((GRID_W, rows, N), BF16),
            compiler_params=_cparams("parallel", "parallel"),
            name="norm_proj_colmajor",
        )(x.reshape(rows, GRID_W, D), *args)
        return out.reshape(L, N)
    tm = min(256, L)
    return pl.pallas_call(
        kern,
        grid=(L // tm,),
        in_specs=[pl.BlockSpec((tm, D), lambda i: (i, 0))] + consts,
        out_specs=pl.BlockSpec((tm, N), lambda i: (i, 0)),
        out_shape=jax.ShapeDtypeStruct((L, N), BF16),
        compiler_params=_cparams("parallel"),
        name="norm_proj",
    )(x, *args)


def _out_proj_kernel(y_ref, w_ref, b_ref, h_ref, mod_ref, o_ref, *, gate_row, col_major, y_cg):
    gate = mod_ref[gate_row:gate_row + 1, :]
    for c in range(SUB if col_major else 1):
        if y_cg:
            y = jnp.concatenate([y_ref[s, c] if col_major else y_ref[s] for s in range(y_ref.shape[0])],
                                axis=1).astype(BF16)
        else:
            y = y_ref[c] if col_major else y_ref[...]
        acc = jnp.dot(y, w_ref[...], preferred_element_type=F32) + b_ref[...]
        if col_major:
            o_ref[:, c, :] = h_ref[:, c, :] + gate * acc
        else:
            o_ref[...] = h_ref[...] + gate * acc


def _out_proj(y, w, b, h, mod, gate_row, col_major):
    L, D = h.shape
    K = w.shape[0]
    y_cg = y.ndim == 3
    G = K // LANES
    kern = functools.partial(_out_proj_kernel, gate_row=gate_row, col_major=col_major, y_cg=y_cg)
    if col_major:
        rows, tr = _raster(L)
        hspec = pl.BlockSpec((tr, SUB, D), lambda iw, ir: (ir, iw, 0))
        if y_cg:
            yspec = pl.BlockSpec((G, SUB, tr, LANES), lambda iw, ir: (0, iw, ir, 0))
            yv = y.reshape(G, GRID_W, rows, LANES)
        else:
            yspec = pl.BlockSpec((SUB, tr, K), lambda iw, ir: (iw, ir, 0))
            yv = y.reshape(GRID_W, rows, K)
        out = pl.pallas_call(
            kern,
            grid=(GRID_W // SUB, rows // tr),
            in_specs=[yspec, _full((K, D)), _full((1, D)), hspec, _full(mod.shape)],
            out_specs=hspec,
            out_shape=jax.ShapeDtypeStruct((rows, GRID_W, D), F32),
            compiler_params=_cparams("parallel", "parallel"),
            name="out_proj_colmajor",
        )(yv, w, b.reshape(1, D), h.reshape(rows, GRID_W, D), mod)
        return out.reshape(L, D)
    tm = min(256, L)
    rows_spec = lambda c: pl.BlockSpec((tm, c), lambda i: (i, 0))
    yspec = pl.BlockSpec((G, tm, LANES), lambda i: (0, i, 0)) if y_cg else rows_spec(K)
    return pl.pallas_call(
        kern,
        grid=(L // tm,),
        in_specs=[yspec, _full((K, D)), _full((1, D)), rows_spec(D), _full(mod.shape)],
        out_specs=rows_spec(D),
        out_shape=jax.ShapeDtypeStruct((L, D), F32),
        compiler_params=_cparams("parallel"),
        name="out_proj",
    )(y, w, b.reshape(1, D), h, mod)


def _halo_specs(tm, L, ncol, cblock):
    hb = tm // HALO
    nb = L // HALO
    prev = pl.BlockSpec((HALO, ncol), lambda i: (jnp.maximum(i * hb - 1, 0), cblock))
    nxt = pl.BlockSpec((HALO, ncol), lambda i: (jnp.minimum((i + 1) * hb, nb - 1), cblock))
    return prev, nxt


def _conv3(x, prev_ref, next_ref, w, b):
    tm = x.shape[0]
    i = pl.program_id(0)
    first = i == 0
    last = i == pl.num_programs(0) - 1
    p = jnp.where(first, 0.0, prev_ref[HALO - 1:HALO, :].astype(F32))
    n = jnp.where(last, 0.0, next_ref[0:1, :].astype(F32))
    row = lax.broadcasted_iota(jnp.int32, x.shape, 0)
    xm1 = jnp.where(row == 0, p, pltpu.roll(x, 1, 0))
    xp1 = jnp.where(row == tm - 1, n, pltpu.roll(x, tm - 1, 0))
    return b + xm1 * w[0:1, :] + x * w[1:2, :] + xp1 * w[2:3, :]


def _mlstm_pre_kernel(x_ref, p_ref, n_ref, cw_ref, cb_ref, qw_ref, kw_ref, vw_ref, gw_ref, gb_ref,
                      xc_ref, q_ref, k_ref, v_ref, g_ref, gt_ref, *, heads, kscale):
    xm = x_ref[...]
    xc = _silu(_conv3(xm.astype(F32), p_ref, n_ref, cw_ref[...], cb_ref[...]))
    xcb = xc.astype(BF16)
    xc_ref[...] = xcb
    inner = xm.shape[1]
    dh = inner // heads
    dqk = qw_ref.shape[2]
    dv = vw_ref.shape[2]
    for h in range(heads):
        xs = xcb[:, h * dh:(h + 1) * dh]
        q_ref[:, h * dqk:(h + 1) * dqk] = jnp.dot(xs, qw_ref[h], preferred_element_type=F32).astype(BF16)
        kh = jnp.dot(xs, kw_ref[h], preferred_element_type=F32) * kscale
        k_ref[:, h * dqk:(h + 1) * dqk] = kh.astype(BF16)
        v_ref[:, h * dv:(h + 1) * dv] = jnp.dot(xm[:, h * dh:(h + 1) * dh], vw_ref[h],
                                                 preferred_element_type=F32).astype(BF16)
    g = (jnp.dot(xcb, gw_ref[:inner, :], preferred_element_type=F32)
         + jnp.dot(xm, gw_ref[inner:, :], preferred_element_type=F32) + gb_ref[...])
    lane = lax.broadcasted_iota(jnp.int32, g.shape, 1)
    is_forget = ((lane >= heads) & (lane < 2 * heads)) | ((lane >= 3 * heads) & (lane < 4 * heads))
    logsig = jnp.minimum(g, 0.0) - jnp.log(1.0 + jnp.exp(-jnp.abs(g)))
    g = jnp.where(is_forget, logsig, g)
    g_ref[...] = g
    gt_ref[...] = g.T[:gt_ref.shape[0], :]


def _mlstm_pre(xo, conv_w, conv_b, q_w, k_w, v_w, gate_w, gate_b):
    L = xo.shape[0]
    heads, dh, dqk = q_w.shape
    dv = v_w.shape[2]
    inner = heads * dh
    tm = min(256, L)
    ng = 4 * heads
    gw = jnp.zeros((2 * inner, LANES), BF16).at[:, :ng].set(gate_w.astype(BF16))
    gb = jnp.zeros((1, LANES), F32).at[0, :ng].set(gate_b)
    prev, nxt = _halo_specs(tm, L, inner, 0)
    rows = lambda c: pl.BlockSpec((tm, c), lambda i: (i, 0))
    return pl.pallas_call(
        functools.partial(_mlstm_pre_kernel, heads=heads, kscale=dqk ** -0.5),
        grid=(L // tm,),
        in_specs=[rows(inner), prev, nxt, _full((3, inner)), _full((1, inner)),
                  _full(q_w.shape), _full(k_w.shape), _full(v_w.shape), _full(gw.shape), _full(gb.shape)],
        out_specs=[rows(inner), rows(heads * dqk), rows(heads * dqk), rows(heads * dv), rows(LANES),
                   pl.BlockSpec((ng, tm), lambda i: (0, i))],
        out_shape=[jax.ShapeDtypeStruct((L, inner), BF16), jax.ShapeDtypeStruct((L, heads * dqk), BF16),
                   jax.ShapeDtypeStruct((L, heads * dqk), BF16), jax.ShapeDtypeStruct((L, heads * dv), BF16),
                   jax.ShapeDtypeStruct((L, LANES), F32), jax.ShapeDtypeStruct((ng, L), F32)],
        compiler_params=_cparams("parallel"),
        name="mlstm_pre",
    )(xo, xo, xo, conv_w, conv_b.reshape(1, inner), q_w.astype(BF16), k_w.astype(BF16), v_w.astype(BF16), gw, gb)


def _scan_kernel(*refs, heads, rev, final):
    if final:
        (q_ref, k_ref, v_ref, g_ref, gt_ref, c0_ref, n0_ref, m0_ref, hf_ref, xc_ref, o_ref, ng_ref, sk_ref,
         out_ref, cT_ref, nT_ref, mT_ref, C_sc, n_sc, m_sc) = refs
    else:
        (q_ref, k_ref, v_ref, g_ref, gt_ref, c0_ref, n0_ref, m0_ref,
         out_ref, cT_ref, nT_ref, mT_ref, C_sc, n_sc, m_sc) = refs
    ci = pl.program_id(0)

    @pl.when(ci == 0)
    def _():
        C_sc[...] = c0_ref[...]
        n_sc[...] = n0_ref[...]
        m_sc[...] = m0_ref[...]

    Lc = q_ref.shape[0]
    dqk = q_ref.shape[1] // heads
    dv = v_ref.shape[1] // heads
    G = g_ref[...]
    GT = gt_ref[...]
    r = lax.broadcasted_iota(jnp.int32, (Lc, Lc), 0)
    c = lax.broadcasted_iota(jnp.int32, (Lc, Lc), 1)
    mask = (c >= r) if rev else (c <= r)
    tri = jnp.where(mask, 1.0, 0.0)
    Bc = jnp.dot(tri, G, precision=HI, preferred_element_type=F32)
    BT = lax.dot_general(GT, tri, (((1,), (1,)), ((), ())), precision=HI, preferred_element_type=F32)
    tot = Bc[0:1, :] if rev else Bc[Lc - 1:Lc, :]
    base = 2 * heads if rev else 0
    for h in range(heads):
        ci_, cf_ = base + h, base + heads + h
        b_col = Bc[:, cf_:cf_ + 1]
        i_col = G[:, ci_:ci_ + 1]
        bT = BT[cf_:cf_ + 1, :]
        iT = GT[ci_:ci_ + 1, :]
        b_last = tot[:, cf_:cf_ + 1]
        m = m_sc[h][:, 0:1]
        logd = jnp.where(mask, b_col - bT + iT, -jnp.inf)
        inter = b_col + m
        mj = jnp.maximum(inter, jnp.max(logd, axis=-1, keepdims=True))
        dmat = jnp.exp(logd - mj)
        qh = q_ref[:, h * dqk:(h + 1) * dqk]
        kh = k_ref[:, h * dqk:(h + 1) * dqk]
        vh = v_ref[:, h * dv:(h + 1) * dv]
        s = lax.dot_general(qh, kh, (((1,), (1,)), ((), ())), preferred_element_type=F32) * dmat
        w_inter = jnp.exp(inter - mj)
        Ct = C_sc[h]
        nrow = n_sc[h]
        num = (jnp.dot(s.astype(BF16), vh, preferred_element_type=F32)
               + w_inter * jnp.dot(qh, Ct.astype(BF16), preferred_element_type=F32))
        den = (jnp.sum(s, axis=-1, keepdims=True)
               + w_inter * jnp.sum(qh.astype(F32) * nrow, axis=-1, keepdims=True))
        hout = num / jnp.maximum(jnp.abs(den), jnp.exp(-mj))
        g_col = b_last - b_col + i_col
        g_row = b_last - bT + iT
        m_new = jnp.maximum(b_last + m, jnp.max(g_row, axis=-1, keepdims=True))
        decay = jnp.exp(b_last + m - m_new)
        wg = jnp.exp(g_col - m_new)
        vw = (vh.astype(F32) * wg).astype(BF16)
        C_sc[h] = decay * Ct + lax.dot_general(kh, vw, (((0,), (0,)), ((), ())), preferred_element_type=F32)
        n_sc[h] = decay * nrow + jnp.sum(kh.astype(F32) * wg, axis=0, keepdims=True)
        m_sc[h] = jnp.broadcast_to(m_new, m_sc.shape[1:])
        cs = slice(h * dv, (h + 1) * dv)
        if final:
            hs = hf_ref[:, cs] + hout
            mu = jnp.mean(hs, axis=-1, keepdims=True)
            d = hs - mu
            var = jnp.mean(d * d, axis=-1, keepdims=True)
            hn = d * lax.rsqrt(var + EPS)
            y = (hn * ng_ref[:, cs] + sk_ref[:, cs] * xc_ref[:, cs].astype(F32)) * _sigmoid(o_ref[:, cs].astype(F32))
            out_ref[:, cs] = y.astype(out_ref.dtype)
        else:
            out_ref[:, cs] = hout

    @pl.when(ci == pl.num_programs(0) - 1)
    def _():
        cT_ref[...] = C_sc[...]
        nT_ref[...] = n_sc[...]
        mT_ref[...] = m_sc[...]


def _mlstm_scan(q, k, v, g, gt, state, rev, final_args=None):
    L = q.shape[0]
    heads = M_HEADS
    dqk = q.shape[1] // heads
    dv = v.shape[1] // heads
    Lc = min(256, L)
    nc = L // Lc
    idx = (lambda i: (nc - 1 - i, 0)) if rev else (lambda i: (i, 0))
    idx_t = (lambda i: (0, nc - 1 - i)) if rev else (lambda i: (0, i))
    rows = lambda c, cb=0: pl.BlockSpec((Lc, c), lambda i: (idx(i)[0], cb))
    c0, n0, m0 = state
    final = final_args is not None
    in_specs = [rows(heads * dqk), rows(heads * dqk), rows(heads * dv), rows(LANES),
                pl.BlockSpec((gt.shape[0], Lc), idx_t), _full(c0.shape), _full(n0.shape), _full(m0.shape)]
    args = [q, k, v, g, gt, c0, n0, m0]
    if final:
        hf, xc, xo, norm_g, skip = final_args
        inner = heads * dv
        in_specs += [rows(inner), rows(inner), rows(inner, 1), _full((1, inner)), _full((1, inner))]
        args += [hf, xc, xo, norm_g.reshape(1, inner), skip.reshape(1, inner)]
    out_dtype = BF16 if final else F32
    return pl.pallas_call(
        functools.partial(_scan_kernel, heads=heads, rev=rev, final=final),
        grid=(nc,),
        in_specs=in_specs,
        out_specs=[rows(heads * dv), _full(c0.shape), _full(n0.shape), _full(m0.shape)],
        out_shape=[jax.ShapeDtypeStruct((L, heads * dv), out_dtype), jax.ShapeDtypeStruct(c0.shape, F32),
                   jax.ShapeDtypeStruct(n0.shape, F32), jax.ShapeDtypeStruct(m0.shape, F32)],
        scratch_shapes=[pltpu.VMEM(c0.shape, F32), pltpu.VMEM(n0.shape, F32), pltpu.VMEM(m0.shape, F32)],
        compiler_params=_cparams("arbitrary"),
        name="mlstm_scan_bwd" if rev else "mlstm_scan_fwd",
    )(*args)


def _mlstm_stream(xo, mp, st_f, st_b):
    (conv_w, conv_b, q_w, k_w, v_w, gate_w, gate_b, norm_g, skip) = mp
    xc, q, k, v, g, gt = _mlstm_pre(xo, conv_w, conv_b, q_w, k_w, v_w, gate_w, gate_b)
    hf, *sf = _mlstm_scan(q, k, v, g, gt, st_f, rev=False)
    y, *sb = _mlstm_scan(q, k, v, g, gt, st_b, rev=True, final_args=(hf, xc, xo, norm_g, skip))
    return y, tuple(sf), tuple(sb)


def _sconv_kernel(x_ref, p_ref, n_ref, w_ref, b_ref, o_ref):
    _cg_put(o_ref, _conv3(x_ref[...].astype(F32), p_ref, n_ref, w_ref[...], b_ref[...]))


def _sconv(zp, w, b):
    L, C = zp.shape
    tm = min(256, L)
    tc = C // 3
    hb, nb = tm // HALO, L // HALO
    return pl.pallas_call(
        _sconv_kernel,
        grid=(L // tm, C // tc),
        in_specs=[pl.BlockSpec((tm, tc), lambda i, j: (i, j)),
                  pl.BlockSpec((HALO, tc), lambda i, j: (jnp.maximum(i * hb - 1, 0), j)),
                  pl.BlockSpec((HALO, tc), lambda i, j: (jnp.minimum((i + 1) * hb, nb - 1), j)),
                  pl.BlockSpec((3, tc), lambda i, j: (0, j)), pl.BlockSpec((1, tc), lambda i, j: (0, j))],
        out_specs=pl.BlockSpec((tc // LANES, tm, LANES), lambda i, j: (j, i, 0)),
        out_shape=jax.ShapeDtypeStruct((C // LANES, L, LANES), F32),
        compiler_params=_cparams("parallel", "parallel"),
        name="hyena_sconv",
    )(zp, zp, zp, w, b.reshape(1, C))


def _filt_kernel(w1_ref, b1_ref, fr_ref, w2_ref, b2_ref, w3_ref, dec_ref, f_ref, ss_ref, *, L):
    i = pl.program_id(0)
    tm = f_ref.shape[1]
    emb = w1_ref.shape[0]
    pos = (lax.broadcasted_iota(jnp.int32, (tm, LANES), 0) + i * tm).astype(F32)
    lane = lax.broadcasted_iota(jnp.int32, (tm, LANES), 1)
    band = jnp.where(lane <= H_BANDS, lane, lane - H_BANDS).astype(F32)
    ang = (2.0 * math.pi / L) * pos * band
    feat = jnp.where(lane == 0, pos / max(L - 1, 1),
                     jnp.where(lane <= H_BANDS, jnp.cos(ang), jnp.where(lane < emb, jnp.sin(ang), 0.0)))
    fr = fr_ref[...]
    z = jnp.sin(fr * (jnp.dot(feat, w1_ref[...], precision=HI, preferred_element_type=F32) + b1_ref[...]))
    z = jnp.sin(fr * (jnp.dot(z, w2_ref[...], precision=HI, preferred_element_type=F32) + b2_ref[...]))
    z = jnp.dot(z, w3_ref[...], precision=HI, preferred_element_type=F32)
    tau = jnp.abs(pos[:, 0:1] - (L // 2)) / (L / 2)
    filt = z * jnp.exp(-tau * dec_ref[...])
    _cg_put(f_ref, filt)

    @pl.when(i == 0)
    def _():
        ss_ref[...] = jnp.zeros_like(ss_ref)

    ss_ref[...] += jnp.sum(filt * filt, axis=0, keepdims=True)


def _hyena_filters(L, w1, b1, freq, w2, b2, w3, decay):
    emb, fh = w1.shape
    C2 = w3.shape[1]
    tm = min(256, L)
    w1p = jnp.zeros((LANES, fh), F32).at[:emb].set(w1)
    return pl.pallas_call(
        functools.partial(_filt_kernel, L=L),
        grid=(L // tm,),
        in_specs=[_full((LANES, fh)), _full((1, fh)), _full((1, fh)), _full((fh, fh)), _full((1, fh)),
                  _full((fh, C2)), _full((1, C2))],
        out_specs=[pl.BlockSpec((C2 // LANES, tm, LANES), lambda i: (0, i, 0)), _full((1, C2))],
        out_shape=[jax.ShapeDtypeStruct((C2 // LANES, L, LANES), F32), jax.ShapeDtypeStruct((1, C2), F32)],
        compiler_params=_cparams("arbitrary"),
        name="hyena_filters",
    )(w1p, b1.reshape(1, fh), freq.reshape(1, fh), w2, b2.reshape(1, fh), w3, decay.reshape(1, C2))


def _dft_tables(L):
    N = 2 * L
    N2 = FFT_N2
    N1 = N // N2
    k1 = jnp.arange(N1, dtype=jnp.int32)
    n1 = jnp.arange(N1 // 2, dtype=jnp.int32)
    n2 = jnp.arange(N2, dtype=jnp.int32)
    ph = (k1[None, :, None] * (N2 * n1[None, None, :] + n2[:, None, None])) % N
    ang = ph.astype(F32) * (2.0 * math.pi / N)
    ca, sa = jnp.cos(ang).astype(BF16), jnp.sin(ang).astype(BF16)
    cd, sd = jnp.swapaxes(ca, 1, 2), jnp.swapaxes(sa, 1, 2)
    ph2 = (n2[:, None] * n2[None, :]) % N2
    ang2 = ph2.astype(F32) * (2.0 * math.pi / N2)
    return (ca, sa), (cd, sd), (jnp.cos(ang2).astype(BF16), jnp.sin(ang2).astype(BF16))


def _fft_a_kernel(x_ref, c_ref, s_ref, a_ref):
    for r in range(SUB):
        x = _cg_load(x_ref, r).astype(BF16)
        re = jnp.dot(c_ref[r], x, preferred_element_type=F32)
        im = -jnp.dot(s_ref[r], x, preferred_element_type=F32)
        _cg_store(a_ref, r, _pack(re, im))


def _fft_a(x, gc, gblock, nblocks, tabs):
    ca, sa = tabs
    N2, N1, H1 = ca.shape
    x4 = x.reshape(x.shape[0], H1, N2, LANES)
    tspec = pl.BlockSpec((SUB, N1, H1), lambda j, cb: (j, 0, 0))
    return pl.pallas_call(
        _fft_a_kernel,
        grid=(N2 // SUB, nblocks),
        in_specs=[pl.BlockSpec((gc, H1, SUB, LANES), lambda j, cb: (gblock + cb, 0, j, 0)), tspec, tspec],
        out_specs=pl.BlockSpec((gc, N1, SUB, LANES), lambda j, cb: (cb, 0, j, 0)),
        out_shape=jax.ShapeDtypeStruct((gc * nblocks, N1, N2, LANES), U32),
        compiler_params=_cparams("parallel", "parallel"),
        name="fft_stage_a",
    )(x4, ca, sa)


def _cmatmul(c, s, xr, xi, sign):
    rr = jnp.dot(c, xr, preferred_element_type=F32)
    si = jnp.dot(s, xi, preferred_element_type=F32)
    ri = jnp.dot(c, xi, preferred_element_type=F32)
    sr = jnp.dot(s, xr, preferred_element_type=F32)
    return rr - sign * si, ri + sign * sr


def _fft_h_kernel(a_ref, c_ref, s_ref, sc_ref, h_ref):
    for t in range(FFT_KB):
        ar, ai = _unpack(_cg_cat(a_ref.at[:, t]))
        xr, xi = _cmatmul(c_ref[...], s_ref[...], ar, ai, -1.0)
        q = (pl.program_id(0) * FFT_KB + t) & 3
        xr, xi = (jnp.where(q == 0, xr, jnp.where(q == 1, -xi, jnp.where(q == 2, -xr, xi))),
                  jnp.where(q == 0, xi, jnp.where(q == 1, xr, jnp.where(q == 2, -xi, -xr))))
        _cg_put(h_ref.at[:, t], _pack(xr * sc_ref[...], xi * sc_ref[...]))


def _fft_h(a, gc, tab2, scale):
    G, N1, N2, _ = a.shape
    blk = pl.BlockSpec((gc, FFT_KB, N2, LANES), lambda j, cb: (cb, j, 0, 0))
    return pl.pallas_call(
        _fft_h_kernel,
        grid=(N1 // FFT_KB, G // gc),
        in_specs=[blk, _full((N2, N2)), _full((N2, N2)), pl.BlockSpec((1, gc * LANES), lambda j, cb: (0, cb))],
        out_specs=blk,
        out_shape=jax.ShapeDtypeStruct(a.shape, U32),
        compiler_params=_cparams("parallel", "parallel"),
        name="fft_filter_spectrum",
    )(a, tab2[0], tab2[1], scale)


def _fft_b_kernel(a_ref, h_ref, c_ref, s_ref, p_ref):
    c, s = c_ref[...], s_ref[...]
    for t in range(FFT_KB):
        ar, ai = _unpack(_cg_cat(a_ref.at[:, t]))
        xr, xi = _cmatmul(c, s, ar, ai, -1.0)
        hr, hi = _unpack(_cg_cat(h_ref.at[:, t]))
        hr, hi = hr.astype(F32), hi.astype(F32)
        yr = (xr * hr - xi * hi).astype(BF16)
        yi = (xr * hi + xi * hr).astype(BF16)
        pr, pi = _cmatmul(c, s, yr, yi, 1.0)
        _cg_put(p_ref.at[:, t], _pack(pr, pi))


def _fft_b(a, h, hblock, tab2):
    G, N1, N2, _ = a.shape
    blk = pl.BlockSpec((G, FFT_KB, N2, LANES), lambda j: (0, j, 0, 0))
    hblk = pl.BlockSpec((G, FFT_KB, N2, LANES), lambda j: (hblock, j, 0, 0))
    return pl.pallas_call(
        _fft_b_kernel,
        grid=(N1 // FFT_KB,),
        in_specs=[blk, hblk, _full((N2, N2)), _full((N2, N2))],
        out_specs=blk,
        out_shape=jax.ShapeDtypeStruct(a.shape, U32),
        compiler_params=_cparams("parallel"),
        name="fft_stage_b",
    )(a, h, tab2[0], tab2[1])


def _fft_d_kernel(p_ref, c_ref, s_ref, g_ref, v_ref, fb_ref, o_ref):
    for r in range(SUB):
        pr, pi = _unpack(_cg_load(p_ref, r))
        y = (jnp.dot(c_ref[r], pr, preferred_element_type=F32)
             - jnp.dot(s_ref[r], pi, preferred_element_type=F32))
        v = _cg_load(v_ref, r)
        _cg_store(o_ref, r, _cg_load(g_ref, r) * (y + v * fb_ref[...]))


def _fft_d(p, tabs, gsrc, gblock, vsrc, vblock, fbias):
    cd, sd = tabs
    N2, H1, N1 = cd.shape
    G = p.shape[0]
    L = H1 * N2
    tspec = pl.BlockSpec((SUB, H1, N1), lambda j: (j, 0, 0))
    sig = lambda gb: pl.BlockSpec((G, H1, SUB, LANES), lambda j: (gb, 0, j, 0))
    out = pl.pallas_call(
        _fft_d_kernel,
        grid=(N2 // SUB,),
        in_specs=[pl.BlockSpec((G, N1, SUB, LANES), lambda j: (0, 0, j, 0)), tspec, tspec, sig(gblock), sig(vblock),
                  _full((1, G * LANES))],
        out_specs=sig(0),
        out_shape=jax.ShapeDtypeStruct((G, H1, N2, LANES), F32),
        compiler_params=_cparams("parallel"),
        name="fft_stage_d",
    )(p, cd, sd, gsrc.reshape(gsrc.shape[0], H1, N2, LANES), vsrc.reshape(vsrc.shape[0], H1, N2, LANES), fbias)
    return out.reshape(G, L, LANES)


def _hyena_long(z, filt, ssq, fbias, tables):
    L = z.shape[1]
    gc = z.shape[0] // 3
    C = gc * LANES
    N = 2 * L
    taba, tabd, tab2 = tables
    scale = lax.rsqrt(ssq + H_DECAY_EPS) * (1.0 / N)
    h = _fft_h(_fft_a(filt, gc, 0, 2, taba), gc, tab2, scale)
    fb = fbias.reshape(1, 2 * C)
    p = _fft_b(_fft_a(z, gc, 2, 1, taba), h, 0, tab2)
    y1 = _fft_d(p, tabd, z, 0, z, 2, fb[:, :C])
    p = _fft_b(_fft_a(y1, gc, 0, 1, taba), h, 1, tab2)
    return _fft_d(p, tabd, z, 1, y1, 0, fb[:, C:])


def _short_conv_kernel(z_ref, f_ref, ss_ref, fb_ref, cf_ref, sf_ref, cd_ref, sd_ref, o_ref):
    L = z_ref.shape[1]
    G = o_ref.shape[0]
    C = G * LANES
    N = 2 * L

    def dft(xb):
        return (jnp.dot(cf_ref[...], xb, preferred_element_type=F32),
                -jnp.dot(sf_ref[...], xb, preferred_element_type=F32))

    def conv(u, blk, gate):
        sl = slice(blk * C, (blk + 1) * C)
        ur, ui = dft(u.astype(BF16))
        fr, fi = dft(_cg_cat(f_ref.at[blk * G:(blk + 1) * G]).astype(BF16))
        q = lax.broadcasted_iota(jnp.int32, fr.shape, 0) & 3
        fr, fi = (jnp.where(q == 0, fr, jnp.where(q == 1, -fi, jnp.where(q == 2, -fr, fi))),
                  jnp.where(q == 0, fi, jnp.where(q == 1, fr, jnp.where(q == 2, -fi, -fr))))
        yr = (ur * fr - ui * fi).astype(BF16)
        yi = (ur * fi + ui * fr).astype(BF16)
        y = (jnp.dot(cd_ref[...], yr, preferred_element_type=F32)
             - jnp.dot(sd_ref[...], yi, preferred_element_type=F32))
        y = y * (lax.rsqrt(ss_ref[:, sl] + H_DECAY_EPS) * (1.0 / N))
        return gate * (y + u * fb_ref[:, sl])

    x1 = _cg_cat(z_ref.at[0:G])
    x2 = _cg_cat(z_ref.at[G:2 * G])
    v = _cg_cat(z_ref.at[2 * G:3 * G])
    _cg_put(o_ref, conv(conv(v, 0, x1), 1, x2))


def _hyena_short(z, filt, ssq, fbias):
    L = z.shape[1]
    G = z.shape[0] // 3
    C = G * LANES
    N = 2 * L
    k = jnp.arange(N, dtype=jnp.int32)
    n = jnp.arange(L, dtype=jnp.int32)
    ang = ((k[:, None] * n[None, :]) % N).astype(F32) * (2.0 * math.pi / N)
    cf, sf = jnp.cos(ang).astype(BF16), jnp.sin(ang).astype(BF16)
    args = (z, filt, ssq, fbias.reshape(1, 2 * C), cf, sf, cf.T, sf.T)
    return pl.pallas_call(
        _short_conv_kernel,
        grid=(1,),
        in_specs=[_full(a.shape) for a in args],
        out_specs=_full((G, L, LANES)),
        out_shape=jax.ShapeDtypeStruct((G, L, LANES), F32),
        compiler_params=_cparams("arbitrary"),
        name="hyena_short_conv",
    )(*args)


def _router_kernel(x_ref, g_ref, mod_ref, rw_ref, rb_ref, c0_ref, xt_ref, rt_ref, cnt_ref, carry):
    i = pl.program_id(0)

    @pl.when(i == 0)
    def _():
        carry[...] = c0_ref[...]

    u = _norm_mod(x_ref[...], g_ref[...], mod_ref[...], 3)
    half = u.shape[1] // 2
    xt_ref[...] = _pack(u[:, :half], u[:, half:])
    logits = jnp.dot(u, rw_ref[...], precision=HI, preferred_element_type=F32) + rb_ref[...]
    tm = u.shape[0]
    lane = lax.broadcasted_iota(jnp.int32, logits.shape, 1).astype(F32)
    ninf = -jnp.inf

    def top(vals):
        mx = jnp.max(vals, axis=-1, keepdims=True)
        ix = jnp.min(jnp.where(vals == mx, lane, float(LANES)), axis=-1, keepdims=True)
        return mx, ix

    lg = jnp.where(lane < N_GROUPS, logits, ninf)
    gmax, gidx = top(lg)
    g_w = 1.0 / jnp.sum(jnp.exp(lg - gmax), axis=-1, keepdims=True)
    lo = N_GROUPS + EXP_PER_GROUP * gidx
    le = jnp.where((lane >= lo) & (lane < lo + EXP_PER_GROUP), logits, ninf)
    m1, i1 = top(le)
    m2, i2 = top(jnp.where(lane == i1, ninf, le))
    r = jnp.exp(m2 - m1)
    w1 = g_w / (1.0 + r)
    w2 = w1 * r
    e1 = i1 - N_GROUPS
    e2 = i2 - N_GROUPS
    oh1 = lane == e1
    oh2 = lane == e2
    both = jnp.where(oh1 | oh2, 1.0, 0.0)
    rr = lax.broadcasted_iota(jnp.int32, (tm, tm), 0)
    cc = lax.broadcasted_iota(jnp.int32, (tm, tm), 1)
    earlier = jnp.where(cc < rr, 1.0, 0.0).astype(BF16)
    before = jnp.dot(earlier, both.astype(BF16), preferred_element_type=F32) + carry[...]
    rank1 = jnp.sum(jnp.where(oh1, before, 0.0), axis=-1, keepdims=True)
    rank2 = jnp.sum(jnp.where(oh2, before, 0.0), axis=-1, keepdims=True)
    carry[...] += jnp.sum(both, axis=0, keepdims=True)
    cnt_ref[...] = carry[...]
    out = jnp.zeros(logits.shape, F32)
    for j, col in enumerate((e1, e2, w1, w2, rank1, rank2)):
        out = jnp.where(lane == j, col, out)
    rt_ref[...] = out


def _router(h, g, mod, rw, rb, counts0):
    L, D = h.shape
    tm = min(256, L)
    return pl.pallas_call(
        _router_kernel,
        grid=(L // tm,),
        in_specs=[pl.BlockSpec((tm, D), lambda i: (i, 0)), _full((1, D)), _full(mod.shape),
                  _full(rw.shape), _full(rb.shape), _full((1, LANES))],
        out_specs=[pl.BlockSpec((tm, D // 2), lambda i: (i, 0)), pl.BlockSpec((tm, LANES), lambda i: (i, 0)),
                   _full((1, LANES))],
        out_shape=[jax.ShapeDtypeStruct((L, D // 2), U32), jax.ShapeDtypeStruct((L, LANES), F32),
                   jax.ShapeDtypeStruct((1, LANES), F32)],
        scratch_shapes=[pltpu.VMEM((1, LANES), F32)],
        compiler_params=_cparams("arbitrary"),
        name="moe_router",
    )(h, g.reshape(1, D), mod, rw, rb, counts0)


def _row_copy(src, i, dst, j, sem):
    return pltpu.make_async_copy(src.at[pl.ds(i, 1), :], dst.at[pl.ds(j, 1), :], sem)


def _dispatch_kernel(dest_ref, x_ref, xb_in_ref, xb_ref, sem):
    del xb_in_ref
    tm = x_ref.shape[0]
    a0 = 2 * tm * pl.program_id(0)

    def issue(r, carry):
        for k in range(2):
            _row_copy(x_ref, r, xb_ref, dest_ref[a0 + 2 * r + k], sem).start()
        return carry

    def wait(r, carry):
        _row_copy(x_ref, 0, xb_ref, 0, sem).wait()
        return carry

    lax.fori_loop(0, tm, issue, 0)
    lax.fori_loop(0, 2 * tm, wait, 0)


def _dispatch(xt, dest, R):
    T, W = xt.shape
    tm = min(256, T)
    return pl.pallas_call(
        _dispatch_kernel,
        grid_spec=pltpu.PrefetchScalarGridSpec(
            num_scalar_prefetch=1,
            grid=(T // tm,),
            in_specs=[pl.BlockSpec((tm, W), lambda i, d: (i, 0)), pl.BlockSpec(memory_space=pl.ANY)],
            out_specs=pl.BlockSpec(memory_space=pl.ANY),
            scratch_shapes=[pltpu.SemaphoreType.DMA(())]),
        out_shape=jax.ShapeDtypeStruct((R, W), U32),
        input_output_aliases={2: 0},
        compiler_params=_cparams("arbitrary"),
        name="moe_dispatch",
    )(dest, xt, jnp.zeros((R, W), U32))


def _expert_kernel(be_ref, nb_ref, x_ref, w13_ref, w2_ref, o_ref):
    live = pl.program_id(0) < nb_ref[0]

    @pl.when(jnp.logical_not(live))
    def _():
        o_ref[...] = jnp.zeros_like(o_ref)

    @pl.when(live)
    def _():
        de = w2_ref.shape[1]
        lo, hi = _unpack(x_ref[...])
        x = jnp.concatenate([lo, hi], axis=1)
        hcat = jnp.dot(x, w13_ref[0], preferred_element_type=F32)
        a = (_silu(hcat[:, :de]) * hcat[:, de:]).astype(BF16)
        y = jnp.dot(a, w2_ref[0], preferred_element_type=F32)
        half = y.shape[1] // 2
        o_ref[...] = _pack(y[:, :half], y[:, half:])


def _experts(xb, blk_exp, n_used, w13, w2):
    R, W = xb.shape
    D = 2 * W
    de = w2.shape[1]
    nb = R // MOE_BLOCK
    return pl.pallas_call(
        _expert_kernel,
        grid_spec=pltpu.PrefetchScalarGridSpec(
            num_scalar_prefetch=2,
            grid=(nb,),
            in_specs=[pl.BlockSpec((MOE_BLOCK, W), lambda b, be, nu: (b, 0)),
                      pl.BlockSpec((1, D, 2 * de), lambda b, be, nu: (be[b], 0, 0)),
                      pl.BlockSpec((1, de, D), lambda b, be, nu: (be[b], 0, 0))],
            out_specs=pl.BlockSpec((MOE_BLOCK, W), lambda b, be, nu: (b, 0))),
        out_shape=jax.ShapeDtypeStruct((R, W), U32),
        compiler_params=_cparams("arbitrary"),
        name="moe_experts",
    )(blk_exp, n_used, xb, w13, w2)


def _combine_kernel(dest_ref, h_ref, rt_ref, mod_ref, yb_ref, o_ref, buf, sem, *, a_base):
    tm = h_ref.shape[0]
    a0 = a_base + 2 * tm * pl.program_id(0)

    def issue(r, carry):
        for k in range(2):
            _row_copy(yb_ref, dest_ref[a0 + 2 * r + k], buf.at[k], r, sem).start()
        return carry

    def wait(r, carry):
        _row_copy(yb_ref, 0, buf.at[0], 0, sem).wait()
        return carry

    lax.fori_loop(0, tm, issue, 0)
    lax.fori_loop(0, 2 * tm, wait, 0)
    rt = rt_ref[...]
    ys = []
    for k in range(2):
        lo, hi = _unpack(buf[k])
        ys.append(jnp.concatenate([lo, hi], axis=1).astype(F32))
    y = rt[:, 2:3] * ys[0] + rt[:, 3:4] * ys[1]
    o_ref[...] = h_ref[...] + mod_ref[5:6, :] * y


def _combine(h, yb, dest, row0, route, mod):
    L, D = h.shape
    tm = min(256, L)
    return pl.pallas_call(
        functools.partial(_combine_kernel, a_base=2 * row0),
        grid_spec=pltpu.PrefetchScalarGridSpec(
            num_scalar_prefetch=1,
            grid=(L // tm,),
            in_specs=[pl.BlockSpec((tm, D), lambda i, d: (i, 0)), pl.BlockSpec((tm, LANES), lambda i, d: (i, 0)),
                      pl.BlockSpec(mod.shape, lambda i, d: (0, 0)), pl.BlockSpec(memory_space=pl.ANY)],
            out_specs=pl.BlockSpec((tm, D), lambda i, d: (i, 0)),
            scratch_shapes=[pltpu.VMEM((2, tm, D // 2), U32), pltpu.SemaphoreType.DMA(())]),
        out_shape=jax.ShapeDtypeStruct((L, D), F32),
        compiler_params=_cparams("arbitrary"),
        name="moe_combine",
    )(dest, h, route, mod, yb)


def _moe(streams, g, rw, rb, w13, w2):
    counts = jnp.zeros((1, LANES), F32)
    xts, routes = [], []
    for h, mod in streams:
        xt, rt, counts = _router(h, g, mod, rw, rb, counts)
        xts.append(xt)
        routes.append(rt)
    xt = jnp.concatenate(xts, axis=0) if len(xts) > 1 else xts[0]
    route = jnp.concatenate(routes, axis=0) if len(routes) > 1 else routes[0]
    T = xt.shape[0]
    A = 2 * T
    cnt = counts[0, :N_EXPERTS].astype(jnp.int32)
    padded = (cnt + MOE_BLOCK - 1) // MOE_BLOCK * MOE_BLOCK
    p_ends = jnp.cumsum(padded)
    p_starts = p_ends - padded
    experts = route[:, 0:2].astype(jnp.int32)
    onehot = experts[:, :, None] == jnp.arange(N_EXPERTS, dtype=jnp.int32)
    dest = (jnp.sum(jnp.where(onehot, p_starts, 0), axis=-1) + route[:, 4:6].astype(jnp.int32)).reshape(A)
    n_blocks = -(-A // MOE_BLOCK) + N_EXPERTS
    blk_start = jnp.arange(n_blocks, dtype=jnp.int32) * MOE_BLOCK
    blk_exp = jnp.minimum(jnp.sum(p_ends[None, :] <= blk_start[:, None], axis=1), N_EXPERTS - 1).astype(jnp.int32)
    n_used = (p_ends[-1:] // MOE_BLOCK).astype(jnp.int32)
    yb = _experts(_dispatch(xt, dest, n_blocks * MOE_BLOCK), blk_exp, n_used, w13, w2)
    outs, row0 = [], 0
    for (h, mod), rt in zip(streams, routes):
        outs.append(_combine(h, yb, dest, row0, rt, mod))
        row0 += h.shape[0]
    return outs


def _final_kernel(x_ref, g_ref, o_ref):
    x = x_ref[...]
    o_ref[...] = x * lax.rsqrt(jnp.mean(x * x, axis=-1, keepdims=True) + EPS) * g_ref[...]


def _final_norm(h, g):
    L, D = h.shape
    tm = min(512, L)
    return pl.pallas_call(
        _final_kernel,
        grid=(L // tm,),
        in_specs=[pl.BlockSpec((tm, D), lambda i: (i, 0)), _full((1, D))],
        out_specs=pl.BlockSpec((tm, D), lambda i: (i, 0)),
        out_shape=jax.ShapeDtypeStruct((L, D), F32),
        compiler_params=_cparams("parallel"),
        name="final_norm",
    )(h, g.reshape(1, D))


def _zero_state(heads, dqk, dv):
    return (jnp.zeros((heads, dqk, dv), F32), jnp.zeros((heads, 1, dqk), F32), jnp.zeros((heads, 1, LANES), F32))


def kernel(x, c, ctx, c_ctx, ada_w, ada_b, norm1_g, norm2_g, rg_w, rg_b, re_w, re_b, moe_w13, moe_w2, m_in_w, m_conv_w, m_conv_b, m_q_w, m_k_w, m_v_w, m_gate_w, m_gate_b, m_norm_g, m_skip, m_out_w, h_in_w, h_in_b, h_sc_w, h_sc_b, h_f_w1, h_f_b1, h_f_freq, h_f_w2, h_f_b2, h_f_w3, h_decay, h_fbias, h_out_w, h_out_b, final_g):
    B, S, D = x.shape
    assert B == 1, "kernel is written for the single-sequence problem shape"
    depth = ada_w.shape[0]
    hl, hc = x[0], ctx[0]
    cc = jnp.zeros((8, D), F32).at[0].set(c[0]).at[1].set(c_ctx)
    mods = _ada(cc, ada_w, ada_b).reshape(depth, 8, 6, D)
    pad8 = lambda m: jnp.concatenate([m, jnp.zeros((2, D), F32)], axis=0)
    nrt = N_GROUPS + N_EXPERTS
    zeros_d = jnp.zeros((D,), F32)
    tables = _dft_tables(S) if depth > 1 else None

    for i in range(depth):
        kind, slot = i % N_MIXERS, i // N_MIXERS
        col_major = (slot % 2) == 1
        ctx_live = any(j % N_MIXERS == 0 for j in range(i + 1, depth))
        ml, mc = pad8(mods[i, 0]), pad8(mods[i, 1])

        if kind == 0:
            w_in = m_in_w[slot].astype(BF16)
            w_out = m_out_w[slot].astype(BF16)
            inner = w_out.shape[0]
            mp = (m_conv_w[slot], m_conv_b[slot], m_q_w[slot], m_k_w[slot], m_v_w[slot], m_gate_w[slot],
                  m_gate_b[slot], m_norm_g[slot], m_skip[slot])
            heads, _, dqk = m_q_w[slot].shape
            dv = m_v_w[slot].shape[2]
            zb = jnp.zeros((w_in.shape[1],), F32)
            xo_c = _norm_proj(hc, norm1_g[i], mc, w_in, zb, False)
            yc, st_f, st_b = _mlstm_stream(xo_c, mp, _zero_state(heads, dqk, dv), _zero_state(heads, dqk, dv))
            xo_l = _norm_proj(hl, norm1_g[i], ml, w_in, zb, col_major)
            yl, _, _ = _mlstm_stream(xo_l, mp, st_f, st_b)
            hl = _out_proj(yl, w_out, zeros_d, hl, ml, 2, col_major)
            if ctx_live:
                hc = _out_proj(yc, w_out, zeros_d, hc, mc, 2, False)
        else:
            w_in = h_in_w[slot].astype(BF16)
            w_out = h_out_w[slot].astype(BF16)
            fp = (h_f_w1[slot], h_f_b1[slot], h_f_freq[slot], h_f_w2[slot], h_f_b2[slot], h_f_w3[slot], h_decay[slot])
            zl = _sconv(_norm_proj(hl, norm1_g[i], ml, w_in, h_in_b[slot], col_major), h_sc_w[slot], h_sc_b[slot])
            filt, ssq = _hyena_filters(S, *fp)
            yl = _hyena_long(zl, filt, ssq, h_fbias[slot], tables)
            hl = _out_proj(yl, w_out, h_out_b[slot], hl, ml, 2, col_major)
            if ctx_live:
                zc = _sconv(_norm_proj(hc, norm1_g[i], mc, w_in, h_in_b[slot], False), h_sc_w[slot], h_sc_b[slot])
                filt_c, ssq_c = _hyena_filters(hc.shape[0], *fp)
                yc = _hyena_short(zc, filt_c, ssq_c, h_fbias[slot])
                hc = _out_proj(yc, w_out, h_out_b[slot], hc, mc, 2, False)

        rw = jnp.zeros((D, LANES), F32).at[:, :N_GROUPS].set(rg_w[i]).at[:, N_GROUPS:nrt].set(re_w[i])
        rb = jnp.zeros((1, LANES), F32).at[0, :N_GROUPS].set(rg_b[i]).at[0, N_GROUPS:nrt].set(re_b[i])
        w13 = moe_w13[i].astype(BF16)
        w2 = moe_w2[i].astype(BF16)
        if ctx_live:
            hc, hl = _moe([(hc, mc), (hl, ml)], norm2_g[i], rw, rb, w13, w2)
        else:
            (hl,) = _moe([(hl, ml)], norm2_g[i], rw, rb, w13, w2)

    return _final_norm(hl, final_g)[None]
```

```python
import functools
import math

import jax
import jax.numpy as jnp
from jax import lax
from jax.experimental import pallas as pl
from jax.experimental.pallas import tpu as pltpu

F32 = jnp.float32
BF16 = jnp.bfloat16
U32 = jnp.uint32
HI = lax.Precision.HIGHEST

EPS = 1e-6
GRID_W = 64
N_MIXERS = 2
M_HEADS = 4
H_BANDS = 16
N_GROUPS = 4
EXP_PER_GROUP = 8
N_EXPERTS = N_GROUPS * EXP_PER_GROUP
MOE_BLOCK = 256
DMA_UNROLL = 8
LANES = 128
HALO = 16
SUB = 8
FFT_N2 = 128
FFT_KB = 2
H_DECAY_EPS = EPS
VMEM_LIMIT = 56 * 1024 * 1024


def _cparams(*sem):
    return pltpu.CompilerParams(dimension_semantics=sem, vmem_limit_bytes=VMEM_LIMIT)


def _full(shape):
    n = len(shape)
    return pl.BlockSpec(shape, lambda *_: (0,) * n)


def _raster(L, tr_pref=128):
    rows = L // GRID_W
    return rows, min(tr_pref, rows)


def _cg_cat(ref):
    return jnp.concatenate([ref[s] for s in range(ref.shape[0])], axis=1)


def _cg_put(ref, val):
    for s in range(ref.shape[0]):
        ref[s] = val[:, s * LANES:(s + 1) * LANES]


def _cg_load(ref4, r):
    G, n = ref4.shape[0], ref4.shape[1]
    flat = ref4.reshape(G, n * SUB, LANES)
    return jnp.concatenate([flat[s, pl.ds(r, n, stride=SUB), :] for s in range(G)], axis=1)


def _cg_store(ref4, r, val):
    G, n = ref4.shape[0], ref4.shape[1]
    flat = ref4.reshape(G, n * SUB, LANES)
    for s in range(G):
        flat[s, pl.ds(r, n, stride=SUB), :] = val[:, s * LANES:(s + 1) * LANES]


def _tt_load(ref3):
    n, G = ref3.shape[0], ref3.shape[1]
    flat = ref3.reshape(n * G, LANES)
    return jnp.concatenate([flat[pl.ds(s, n, stride=G), :] for s in range(G)], axis=1)


def _tt_store(ref3, val):
    n, G = ref3.shape[0], ref3.shape[1]
    flat = ref3.reshape(n * G, LANES)
    for s in range(G):
        flat[pl.ds(s, n, stride=G), :] = val[:, s * LANES:(s + 1) * LANES]


def _pack(re, im):
    hi = pltpu.bitcast(re.astype(BF16).astype(F32), U32)
    lo = pltpu.bitcast(im.astype(BF16).astype(F32), U32)
    return (hi & jnp.uint32(0xFFFF0000)) | (lo >> 16)


def _unpack(w):
    re = pltpu.bitcast(w & jnp.uint32(0xFFFF0000), F32)
    im = pltpu.bitcast(w << 16, F32)
    return re.astype(BF16), im.astype(BF16)


def _sigmoid(x):
    return 1.0 / (1.0 + jnp.exp(-x))


def _silu(x):
    return x * _sigmoid(x)


def _norm_mod(x, g, mod, k):
    ms = jnp.mean(x * x, axis=-1, keepdims=True)
    y = x * lax.rsqrt(ms + EPS) * g
    return y * (1.0 + mod[k + 1:k + 2, :]) + mod[k:k + 1, :]


def _ada_kernel(c_ref, w_ref, b_ref, o_ref):
    s = _silu(c_ref[...])
    o_ref[0] = jnp.dot(s, w_ref[0], precision=HI, preferred_element_type=F32) + b_ref[0]


def _ada(cc, ada_w, ada_b):
    depth, D, N = ada_w.shape
    tn = N // 6
    return pl.pallas_call(
        _ada_kernel,
        grid=(depth, N // tn),
        in_specs=[pl.BlockSpec((8, D), lambda l, j: (0, 0)),
                  pl.BlockSpec((1, D, tn), lambda l, j: (l, 0, j)),
                  pl.BlockSpec((1, 1, tn), lambda l, j: (l, 0, j))],
        out_specs=pl.BlockSpec((1, 8, tn), lambda l, j: (l, 0, j)),
        out_shape=jax.ShapeDtypeStruct((depth, 8, N), F32),
        compiler_params=_cparams("parallel", "parallel"),
        name="ada",
    )(cc, ada_w, ada_b.reshape(depth, 1, N))


def _norm_proj_kernel(x_ref, g_ref, mod_ref, w_ref, b_ref, o_ref, *, nchunk, col_major):
    N = o_ref.shape[-1]
    step = N // nchunk
    for c in range(SUB if col_major else 1):
        x = x_ref[:, c, :] if col_major else x_ref[...]
        u = _norm_mod(x, g_ref[...], mod_ref[...], 0).astype(BF16)
        for j in range(nchunk):
            sl = slice(j * step, (j + 1) * step)
            acc = (jnp.dot(u, w_ref[:, sl], preferred_element_type=F32) + b_ref[:, sl]).astype(o_ref.dtype)
            if col_major:
                o_ref[c, :, sl] = acc
            else:
                o_ref[:, sl] = acc


def _norm_proj(x, g, mod, w, b, col_major):
    L, D = x.shape
    N = w.shape[1]
    consts = [_full((1, D)), _full(mod.shape), _full((D, N)), _full((1, N))]
    kern = functools.partial(_norm_proj_kernel, nchunk=max(1, N // 1024), col_major=col_major)
    args = (g.reshape(1, D), mod, w, b.reshape(1, N))
    if col_major:
        rows, tr = _raster(L)
        out = pl.pallas_call(
            kern,
            grid=(GRID_W // SUB, rows // tr),
            in_specs=[pl.BlockSpec((tr, SUB, D), lambda iw, ir: (ir, iw, 0))] + consts,
            out_specs=pl.BlockSpec((SUB, tr, N), lambda iw, ir: (iw, ir, 0)),
            out_shape=jax.ShapeDtypeStruct((GRID_W, rows, N), BF16),
            compiler_params=_cparams("parallel", "parallel"),
            name="norm_proj_colmajor",
        )(x.reshape(rows, GRID_W, D), *args)
        return out.reshape(L, N)
    tm = min(256, L)
    return pl.pallas_call(
        kern,
        grid=(L // tm,),
        in_specs=[pl.BlockSpec((tm, D), lambda i: (i, 0))] + consts,
        out_specs=pl.BlockSpec((tm, N), lambda i: (i, 0)),
        out_shape=jax.ShapeDtypeStruct((L, N), BF16),
        compiler_params=_cparams("parallel"),
        name="norm_proj",
    )(x, *args)


def _out_proj_kernel(y_ref, w_ref, b_ref, h_ref, mod_ref, o_ref, *, gate_row, col_major, y_cg):
    gate = mod_ref[gate_row:gate_row + 1, :]
    for c in range(SUB if col_major else 1):
        if y_cg:
            y = jnp.concatenate([y_ref[s, c] if col_major else y_ref[s] for s in range(y_ref.shape[0])],
                                axis=1).astype(BF16)
        else:
            y = y_ref[c] if col_major else y_ref[...]
        acc = jnp.dot(y, w_ref[...], preferred_element_type=F32) + b_ref[...]
        if col_major:
            o_ref[:, c, :] = h_ref[:, c, :] + gate * acc
        else:
            o_ref[...] = h_ref[...] + gate * acc


def _out_proj(y, w, b, h, mod, gate_row, col_major):
    L, D = h.shape
    K = w.shape[0]
    y_cg = y.ndim == 3
    G = K // LANES
    kern = functools.partial(_out_proj_kernel, gate_row=gate_row, col_major=col_major, y_cg=y_cg)
    if col_major:
        rows, tr = _raster(L)
        hspec = pl.BlockSpec((tr, SUB, D), lambda iw, ir: (ir, iw, 0))
        if y_cg:
            yspec = pl.BlockSpec((G, SUB, tr, LANES), lambda iw, ir: (0, iw, ir, 0))
            yv = y.reshape(G, GRID_W, rows, LANES)
        else:
            yspec = pl.BlockSpec((SUB, tr, K), lambda iw, ir: (iw, ir, 0))
            yv = y.reshape(GRID_W, rows, K)
        out = pl.pallas_call(
            kern,
            grid=(GRID_W // SUB, rows // tr),
            in_specs=[yspec, _full((K, D)), _full((1, D)), hspec, _full(mod.shape)],
            out_specs=hspec,
            out_shape=jax.ShapeDtypeStruct((rows, GRID_W, D), F32),
            compiler_params=_cparams("parallel", "parallel"),
            name="out_proj_colmajor",
        )(yv, w, b.reshape(1, D), h.reshape(rows, GRID_W, D), mod)
        return out.reshape(L, D)
    tm = min(256, L)
    rows_spec = lambda c: pl.BlockSpec((tm, c), lambda i: (i, 0))
    yspec = pl.BlockSpec((G, tm, LANES), lambda i: (0, i, 0)) if y_cg else rows_spec(K)
    return pl.pallas_call(
        kern,
        grid=(L // tm,),
        in_specs=[yspec, _full((K, D)), _full((1, D)), rows_spec(D), _full(mod.shape)],
        out_specs=rows_spec(D),
        out_shape=jax.ShapeDtypeStruct((L, D), F32),
        compiler_params=_cparams("parallel"),
        name="out_proj",
    )(y, w, b.reshape(1, D), h, mod)


def _halo_specs(tm, L, ncol, cblock):
    hb = tm // HALO
    nb = L // HALO
    prev = pl.BlockSpec((HALO, ncol), lambda i: (jnp.maximum(i * hb - 1, 0), cblock))
    nxt = pl.BlockSpec((HALO, ncol), lambda i: (jnp.minimum((i + 1) * hb, nb - 1), cblock))
    return prev, nxt


def _conv3(x, prev_ref, next_ref, w, b):
    tm = x.shape[0]
    i = pl.program_id(0)
    first = i == 0
    last = i == pl.num_programs(0) - 1
    p = jnp.where(first, 0.0, prev_ref[HALO - 1:HALO, :].astype(F32))
    n = jnp.where(last, 0.0, next_ref[0:1, :].astype(F32))
    row = lax.broadcasted_iota(jnp.int32, x.shape, 0)
    xm1 = jnp.where(row == 0, p, pltpu.roll(x, 1, 0))
    xp1 = jnp.where(row == tm - 1, n, pltpu.roll(x, tm - 1, 0))
    return b + xm1 * w[0:1, :] + x * w[1:2, :] + xp1 * w[2:3, :]


def _mlstm_pre_kernel(x_ref, p_ref, n_ref, cw_ref, cb_ref, qw_ref, kw_ref, vw_ref, gw_ref, gb_ref,
                      xc_ref, q_ref, k_ref, v_ref, g_ref, gt_ref, *, heads, kscale):
    xm = x_ref[...]
    xc = _silu(_conv3(xm.astype(F32), p_ref, n_ref, cw_ref[...], cb_ref[...]))
    xcb = xc.astype(BF16)
    xc_ref[...] = xcb
    inner = xm.shape[1]
    dh = inner // heads
    dqk = qw_ref.shape[2]
    dv = vw_ref.shape[2]
    for h in range(heads):
        xs = xcb[:, h * dh:(h + 1) * dh]
        q_ref[:, h * dqk:(h + 1) * dqk] = jnp.dot(xs, qw_ref[h], preferred_element_type=F32).astype(BF16)
        kh = jnp.dot(xs, kw_ref[h], preferred_element_type=F32) * kscale
        k_ref[:, h * dqk:(h + 1) * dqk] = kh.astype(BF16)
        v_ref[:, h * dv:(h + 1) * dv] = jnp.dot(xm[:, h * dh:(h + 1) * dh], vw_ref[h],
                                                 preferred_element_type=F32).astype(BF16)
    g = (jnp.dot(xcb, gw_ref[:inner, :], preferred_element_type=F32)
         + jnp.dot(xm, gw_ref[inner:, :], preferred_element_type=F32) + gb_ref[...])
    lane = lax.broadcasted_iota(jnp.int32, g.shape, 1)
    is_forget = ((lane >= heads) & (lane < 2 * heads)) | ((lane >= 3 * heads) & (lane < 4 * heads))
    logsig = jnp.minimum(g, 0.0) - jnp.log(1.0 + jnp.exp(-jnp.abs(g)))
    g = jnp.where(is_forget, logsig, g)
    g_ref[...] = g
    gt_ref[...] = g.T[:gt_ref.shape[0], :]


def _mlstm_pre(xo, conv_w, conv_b, q_w, k_w, v_w, gate_w, gate_b):
    L = xo.shape[0]
    heads, dh, dqk = q_w.shape
    dv = v_w.shape[2]
    inner = heads * dh
    tm = min(256, L)
    ng = 4 * heads
    gw = jnp.zeros((2 * inner, LANES), BF16).at[:, :ng].set(gate_w.astype(BF16))
    gb = jnp.zeros((1, LANES), F32).at[0, :ng].set(gate_b)
    prev, nxt = _halo_specs(tm, L, inner, 0)
    rows = lambda c: pl.BlockSpec((tm, c), lambda i: (i, 0))
    return pl.pallas_call(
        functools.partial(_mlstm_pre_kernel, heads=heads, kscale=dqk ** -0.5),
        grid=(L // tm,),
        in_specs=[rows(inner), prev, nxt, _full((3, inner)), _full((1, inner)),
                  _full(q_w.shape), _full(k_w.shape), _full(v_w.shape), _full(gw.shape), _full(gb.shape)],
        out_specs=[rows(inner), rows(heads * dqk), rows(heads * dqk), rows(heads * dv), rows(LANES),
                   pl.BlockSpec((ng, tm), lambda i: (0, i))],
        out_shape=[jax.ShapeDtypeStruct((L, inner), BF16), jax.ShapeDtypeStruct((L, heads * dqk), BF16),
                   jax.ShapeDtypeStruct((L, heads * dqk), BF16), jax.ShapeDtypeStruct((L, heads * dv), BF16),
                   jax.ShapeDtypeStruct((L, LANES), F32), jax.ShapeDtypeStruct((ng, L), F32)],
        compiler_params=_cparams("parallel"),
        name="mlstm_pre",
    )(xo, xo, xo, conv_w, conv_b.reshape(1, inner), q_w.astype(BF16), k_w.astype(BF16), v_w.astype(BF16), gw, gb)


def _scan_kernel(*refs, heads, rev, final):
    if final:
        (q_ref, k_ref, v_ref, g_ref, gt_ref, c0_ref, n0_ref, m0_ref, hf_ref, xc_ref, o_ref, ng_ref, sk_ref,
         out_ref, cT_ref, nT_ref, mT_ref, C_sc, n_sc, m_sc) = refs
    else:
        (q_ref, k_ref, v_ref, g_ref, gt_ref, c0_ref, n0_ref, m0_ref,
         out_ref, cT_ref, nT_ref, mT_ref, C_sc, n_sc, m_sc) = refs
    ci = pl.program_id(0)

    @pl.when(ci == 0)
    def _():
        C_sc[...] = c0_ref[...]
        n_sc[...] = n0_ref[...]
        m_sc[...] = m0_ref[...]

    Lc = q_ref.shape[0]
    dqk = q_ref.shape[1] // heads
    dv = v_ref.shape[1] // heads
    G = g_ref[...]
    GT = gt_ref[...]
    r = lax.broadcasted_iota(jnp.int32, (Lc, Lc), 0)
    c = lax.broadcasted_iota(jnp.int32, (Lc, Lc), 1)
    mask = (c >= r) if rev else (c <= r)
    tri = jnp.where(mask, 1.0, 0.0)
    Bc = jnp.dot(tri, G, precision=HI, preferred_element_type=F32)
    BT = lax.dot_general(GT, tri, (((1,), (1,)), ((), ())), precision=HI, preferred_element_type=F32)
    tot = Bc[0:1, :] if rev else Bc[Lc - 1:Lc, :]
    base = 2 * heads if rev else 0
    for h in range(heads):
        ci_, cf_ = base + h, base + heads + h
        b_col = Bc[:, cf_:cf_ + 1]
        i_col = G[:, ci_:ci_ + 1]
        bT = BT[cf_:cf_ + 1, :]
        iT = GT[ci_:ci_ + 1, :]
        b_last = tot[:, cf_:cf_ + 1]
        m = m_sc[h][:, 0:1]
        logd = jnp.where(mask, b_col - bT + iT, -jnp.inf)
        inter = b_col + m
        mj = jnp.maximum(inter, jnp.max(logd, axis=-1, keepdims=True))
        dmat = jnp.exp(logd - mj)
        qh = q_ref[:, h * dqk:(h + 1) * dqk]
        kh = k_ref[:, h * dqk:(h + 1) * dqk]
        vh = v_ref[:, h * dv:(h + 1) * dv]
        s = lax.dot_general(qh, kh, (((1,), (1,)), ((), ())), preferred_element_type=F32) * dmat
        w_inter = jnp.exp(inter - mj)
        Ct = C_sc[h]
        nrow = n_sc[h]
        num = (jnp.dot(s.astype(BF16), vh, preferred_element_type=F32)
               + w_inter * jnp.dot(qh, Ct.astype(BF16), preferred_element_type=F32))
        den = (jnp.sum(s, axis=-1, keepdims=True)
               + w_inter * jnp.sum(qh.astype(F32) * nrow, axis=-1, keepdims=True))
        hout = num / jnp.maximum(jnp.abs(den), jnp.exp(-mj))
        g_col = b_last - b_col + i_col
        g_row = b_last - bT + iT
        m_new = jnp.maximum(b_last + m, jnp.max(g_row, axis=-1, keepdims=True))
        decay = jnp.exp(b_last + m - m_new)
        wg = jnp.exp(g_col - m_new)
        vw = (vh.astype(F32) * wg).astype(BF16)
        C_sc[h] = decay * Ct + lax.dot_general(kh, vw, (((0,), (0,)), ((), ())), preferred_element_type=F32)
        n_sc[h] = decay * nrow + jnp.sum(kh.astype(F32) * wg, axis=0, keepdims=True)
        m_sc[h] = jnp.broadcast_to(m_new, m_sc.shape[1:])
        cs = slice(h * dv, (h + 1) * dv)
        if final:
            hs = hf_ref[:, cs] + hout
            mu = jnp.mean(hs, axis=-1, keepdims=True)
            d = hs - mu
            var = jnp.mean(d * d, axis=-1, keepdims=True)
            hn = d * lax.rsqrt(var + EPS)
            y = (hn * ng_ref[:, cs] + sk_ref[:, cs] * xc_ref[:, cs].astype(F32)) * _sigmoid(o_ref[:, cs].astype(F32))
            out_ref[:, cs] = y.astype(out_ref.dtype)
        else:
            out_ref[:, cs] = hout

    @pl.when(ci == pl.num_programs(0) - 1)
    def _():
        cT_ref[...] = C_sc[...]
        nT_ref[...] = n_sc[...]
        mT_ref[...] = m_sc[...]


def _mlstm_scan(q, k, v, g, gt, state, rev, final_args=None):
    L = q.shape[0]
    heads = M_HEADS
    dqk = q.shape[1] // heads
    dv = v.shape[1] // heads
    Lc = min(256, L)
    nc = L // Lc
    idx = (lambda i: (nc - 1 - i, 0)) if rev else (lambda i: (i, 0))
    idx_t = (lambda i: (0, nc - 1 - i)) if rev else (lambda i: (0, i))
    rows = lambda c, cb=0: pl.BlockSpec((Lc, c), lambda i: (idx(i)[0], cb))
    c0, n0, m0 = state
    final = final_args is not None
    in_specs = [rows(heads * dqk), rows(heads * dqk), rows(heads * dv), rows(LANES),
                pl.BlockSpec((gt.shape[0], Lc), idx_t), _full(c0.shape), _full(n0.shape), _full(m0.shape)]
    args = [q, k, v, g, gt, c0, n0, m0]
    if final:
        hf, xc, xo, norm_g, skip = final_args
        inner = heads * dv
        in_specs += [rows(inner), rows(inner), rows(inner, 1), _full((1, inner)), _full((1, inner))]
        args += [hf, xc, xo, norm_g.reshape(1, inner), skip.reshape(1, inner)]
    out_dtype = BF16 if final else F32
    return pl.pallas_call(
        functools.partial(_scan_kernel, heads=heads, rev=rev, final=final),
        grid=(nc,),
        in_specs=in_specs,
        out_specs=[rows(heads * dv), _full(c0.shape), _full(n0.shape), _full(m0.shape)],
        out_shape=[jax.ShapeDtypeStruct((L, heads * dv), out_dtype), jax.ShapeDtypeStruct(c0.shape, F32),
                   jax.ShapeDtypeStruct(n0.shape, F32), jax.ShapeDtypeStruct(m0.shape, F32)],
        scratch_shapes=[pltpu.VMEM(c0.shape, F32), pltpu.VMEM(n0.shape, F32), pltpu.VMEM(m0.shape, F32)],
        compiler_params=_cparams("arbitrary"),
        name="mlstm_scan_bwd" if rev else "mlstm_scan_fwd",
    )(*args)


def _mlstm_stream(xo, mp, st_f, st_b):
    (conv_w, conv_b, q_w, k_w, v_w, gate_w, gate_b, norm_g, skip) = mp
    xc, q, k, v, g, gt = _mlstm_pre(xo, conv_w, conv_b, q_w, k_w, v_w, gate_w, gate_b)
    hf, *sf = _mlstm_scan(q, k, v, g, gt, st_f, rev=False)
    y, *sb = _mlstm_scan(q, k, v, g, gt, st_b, rev=True, final_args=(hf, xc, xo, norm_g, skip))
    return y, tuple(sf), tuple(sb)


def _sconv_kernel(x_ref, p_ref, n_ref, w_ref, b_ref, o_ref):
    _cg_put(o_ref, _conv3(x_ref[...].astype(F32), p_ref, n_ref, w_ref[...], b_ref[...]))


def _sconv(zp, w, b):
    L, C = zp.shape
    tm = min(256, L)
    tc = C // 3
    hb, nb = tm // HALO, L // HALO
    return pl.pallas_call(
        _sconv_kernel,
        grid=(L // tm, C // tc),
        in_specs=[pl.BlockSpec((tm, tc), lambda i, j: (i, j)),
                  pl.BlockSpec((HALO, tc), lambda i, j: (jnp.maximum(i * hb - 1, 0), j)),
                  pl.BlockSpec((HALO, tc), lambda i, j: (jnp.minimum((i + 1) * hb, nb - 1), j)),
                  pl.BlockSpec((3, tc), lambda i, j: (0, j)), pl.BlockSpec((1, tc), lambda i, j: (0, j))],
        out_specs=pl.BlockSpec((tc // LANES, tm, LANES), lambda i, j: (j, i, 0)),
        out_shape=jax.ShapeDtypeStruct((C // LANES, L, LANES), F32),
        compiler_params=_cparams("parallel", "parallel"),
        name="hyena_sconv",
    )(zp, zp, zp, w, b.reshape(1, C))


def _filt_kernel(w1_ref, b1_ref, fr_ref, w2_ref, b2_ref, w3_ref, dec_ref, f_ref, ss_ref, *, L):
    i = pl.program_id(0)
    tm = f_ref.shape[1]
    emb = w1_ref.shape[0]
    pos = (lax.broadcasted_iota(jnp.int32, (tm, LANES), 0) + i * tm).astype(F32)
    lane = lax.broadcasted_iota(jnp.int32, (tm, LANES), 1)
    band = jnp.where(lane <= H_BANDS, lane, lane - H_BANDS).astype(F32)
    ang = (2.0 * math.pi / L) * pos * band
    quarter = jnp.where(lane <= H_BANDS, 0.5 * math.pi, 0.0)
    feat = jnp.where(lane == 0, pos / max(L - 1, 1), jnp.where(lane < emb, jnp.sin(ang + quarter), 0.0))
    fr = fr_ref[...]
    z = jnp.sin(fr * (jnp.dot(feat, w1_ref[...], precision=HI, preferred_element_type=F32) + b1_ref[...]))
    z = jnp.sin(fr * (jnp.dot(z, w2_ref[...], precision=HI, preferred_element_type=F32) + b2_ref[...]))
    z = jnp.dot(z.astype(BF16), w3_ref[...].astype(BF16), preferred_element_type=F32)
    tau = jnp.abs(pos[:, 0:1] - (L // 2)) / (L / 2)
    filt = z * jnp.exp(-tau * dec_ref[...])
    _cg_put(f_ref, filt)

    @pl.when(i == 0)
    def _():
        ss_ref[...] = jnp.zeros_like(ss_ref)

    ss_ref[...] += jnp.sum(filt * filt, axis=0, keepdims=True)


def _hyena_filters(L, w1, b1, freq, w2, b2, w3, decay):
    emb, fh = w1.shape
    C2 = w3.shape[1]
    tm = min(256, L)
    w1p = jnp.zeros((LANES, fh), F32).at[:emb].set(w1)
    return pl.pallas_call(
        functools.partial(_filt_kernel, L=L),
        grid=(L // tm,),
        in_specs=[_full((LANES, fh)), _full((1, fh)), _full((1, fh)), _full((fh, fh)), _full((1, fh)),
                  _full((fh, C2)), _full((1, C2))],
        out_specs=[pl.BlockSpec((C2 // LANES, tm, LANES), lambda i: (0, i, 0)), _full((1, C2))],
        out_shape=[jax.ShapeDtypeStruct((C2 // LANES, L, LANES), F32), jax.ShapeDtypeStruct((1, C2), F32)],
        compiler_params=_cparams("arbitrary"),
        name="hyena_filters",
    )(w1p, b1.reshape(1, fh), freq.reshape(1, fh), w2, b2.reshape(1, fh), w3, decay.reshape(1, C2))


def _dft_tables(L):
    N = 2 * L
    N2 = FFT_N2
    N1 = N // N2
    k1 = jnp.arange(N1, dtype=jnp.int32)
    n1 = jnp.arange(N1 // 2, dtype=jnp.int32)
    n2 = jnp.arange(N2, dtype=jnp.int32)
    a = ((k1[:, None] * n1[None, :]) % N1).astype(F32) * (2.0 * math.pi / N1)
    b = ((n2[:, None] * k1[None, :]) % N).astype(F32) * (2.0 * math.pi / N)
    cos_a, sin_a, cos_b, sin_b = jnp.cos(a)[None], jnp.sin(a)[None], jnp.cos(b)[:, :, None], jnp.sin(b)[:, :, None]
    ca = (cos_b * cos_a - sin_b * sin_a).astype(BF16)
    sa = (sin_b * cos_a + cos_b * sin_a).astype(BF16)
    cd, sd = jnp.swapaxes(ca, 1, 2), jnp.swapaxes(sa, 1, 2)
    ph2 = (n2[:, None] * n2[None, :]) % N2
    ang2 = ph2.astype(F32) * (2.0 * math.pi / N2)
    return (ca, sa), (cd, sd), (jnp.cos(ang2).astype(BF16), jnp.sin(ang2).astype(BF16))


def _fft_a_kernel(x_ref, c_ref, s_ref, a_ref):
    for r in range(SUB):
        x = _cg_load(x_ref, r).astype(BF16)
        re = jnp.dot(c_ref[r], x, preferred_element_type=F32)
        im = -jnp.dot(s_ref[r], x, preferred_element_type=F32)
        _cg_store(a_ref, r, _pack(re, im))


def _fft_a(x, gc, gblock, nblocks, tabs):
    ca, sa = tabs
    N2, N1, H1 = ca.shape
    x4 = x.reshape(x.shape[0], H1, N2, LANES)
    tspec = pl.BlockSpec((SUB, N1, H1), lambda j, cb: (j, 0, 0))
    return pl.pallas_call(
        _fft_a_kernel,
        grid=(N2 // SUB, nblocks),
        in_specs=[pl.BlockSpec((gc, H1, SUB, LANES), lambda j, cb: (gblock + cb, 0, j, 0)), tspec, tspec],
        out_specs=pl.BlockSpec((gc, N1, SUB, LANES), lambda j, cb: (cb, 0, j, 0)),
        out_shape=jax.ShapeDtypeStruct((gc * nblocks, N1, N2, LANES), U32),
        compiler_params=_cparams("parallel", "parallel"),
        name="fft_stage_a",
    )(x4, ca, sa)


def _cmatmul(c, s, xr, xi, sign):
    rr = jnp.dot(c, xr, preferred_element_type=F32)
    si = jnp.dot(s, xi, preferred_element_type=F32)
    ri = jnp.dot(c, xi, preferred_element_type=F32)
    sr = jnp.dot(s, xr, preferred_element_type=F32)
    return rr - sign * si, ri + sign * sr


def _fft_h_kernel(a_ref, c_ref, s_ref, sc_ref, h_ref):
    for t in range(FFT_KB):
        ar, ai = _unpack(_cg_cat(a_ref.at[:, t]))
        xr, xi = _cmatmul(c_ref[...], s_ref[...], ar, ai, -1.0)
        q = (pl.program_id(0) * FFT_KB + t) & 3
        xr, xi = (jnp.where(q == 0, xr, jnp.where(q == 1, -xi, jnp.where(q == 2, -xr, xi))),
                  jnp.where(q == 0, xi, jnp.where(q == 1, xr, jnp.where(q == 2, -xi, -xr))))
        _cg_put(h_ref.at[:, t], _pack(xr * sc_ref[...], xi * sc_ref[...]))


def _fft_h(a, gc, tab2, scale):
    G, N1, N2, _ = a.shape
    blk = pl.BlockSpec((gc, FFT_KB, N2, LANES), lambda j, cb: (cb, j, 0, 0))
    return pl.pallas_call(
        _fft_h_kernel,
        grid=(N1 // FFT_KB, G // gc),
        in_specs=[blk, _full((N2, N2)), _full((N2, N2)), pl.BlockSpec((1, gc * LANES), lambda j, cb: (0, cb))],
        out_specs=blk,
        out_shape=jax.ShapeDtypeStruct(a.shape, U32),
        compiler_params=_cparams("parallel", "parallel"),
        name="fft_filter_spectrum",
    )(a, tab2[0], tab2[1], scale)


def _fft_b_kernel(a_ref, h_ref, c_ref, s_ref, p_ref):
    c, s = c_ref[...], s_ref[...]
    for t in range(FFT_KB):
        ar, ai = _unpack(_cg_cat(a_ref.at[:, t]))
        xr, xi = _cmatmul(c, s, ar, ai, -1.0)
        hr, hi = _unpack(_cg_cat(h_ref.at[:, t]))
        hr, hi = hr.astype(F32), hi.astype(F32)
        yr = (xr * hr - xi * hi).astype(BF16)
        yi = (xr * hi + xi * hr).astype(BF16)
        pr, pi = _cmatmul(c, s, yr, yi, 1.0)
        _cg_put(p_ref.at[:, t], _pack(pr, pi))


def _fft_b(a, h, hblock, tab2):
    G, N1, N2, _ = a.shape
    blk = pl.BlockSpec((G, FFT_KB, N2, LANES), lambda j: (0, j, 0, 0))
    hblk = pl.BlockSpec((G, FFT_KB, N2, LANES), lambda j: (hblock, j, 0, 0))
    return pl.pallas_call(
        _fft_b_kernel,
        grid=(N1 // FFT_KB,),
        in_specs=[blk, hblk, _full((N2, N2)), _full((N2, N2))],
        out_specs=blk,
        out_shape=jax.ShapeDtypeStruct(a.shape, U32),
        compiler_params=_cparams("parallel"),
        name="fft_stage_b",
    )(a, h, tab2[0], tab2[1])


def _fft_d_kernel(p_ref, c_ref, s_ref, g_ref, v_ref, fb_ref, o_ref):
    for r in range(SUB):
        pr, pi = _unpack(_cg_load(p_ref, r))
        y = (jnp.dot(c_ref[r], pr, preferred_element_type=F32)
             - jnp.dot(s_ref[r], pi, preferred_element_type=F32))
        v = _cg_load(v_ref, r)
        _cg_store(o_ref, r, _cg_load(g_ref, r) * (y + v * fb_ref[...]))


def _fft_d(p, tabs, gsrc, gblock, vsrc, vblock, fbias):
    cd, sd = tabs
    N2, H1, N1 = cd.shape
    G = p.shape[0]
    L = H1 * N2
    tspec = pl.BlockSpec((SUB, H1, N1), lambda j: (j, 0, 0))
    sig = lambda gb: pl.BlockSpec((G, H1, SUB, LANES), lambda j: (gb, 0, j, 0))
    out = pl.pallas_call(
        _fft_d_kernel,
        grid=(N2 // SUB,),
        in_specs=[pl.BlockSpec((G, N1, SUB, LANES), lambda j: (0, 0, j, 0)), tspec, tspec, sig(gblock), sig(vblock),
                  _full((1, G * LANES))],
        out_specs=sig(0),
        out_shape=jax.ShapeDtypeStruct((G, H1, N2, LANES), F32),
        compiler_params=_cparams("parallel"),
        name="fft_stage_d",
    )(p, cd, sd, gsrc.reshape(gsrc.shape[0], H1, N2, LANES), vsrc.reshape(vsrc.shape[0], H1, N2, LANES), fbias)
    return out.reshape(G, L, LANES)


def _hyena_long(z, filt, ssq, fbias, tables):
    L = z.shape[1]
    gc = z.shape[0] // 3
    C = gc * LANES
    N = 2 * L
    taba, tabd, tab2 = tables
    scale = lax.rsqrt(ssq + H_DECAY_EPS) * (1.0 / N)
    h = _fft_h(_fft_a(filt, gc, 0, 2, taba), gc, tab2, scale)
    fb = fbias.reshape(1, 2 * C)
    p = _fft_b(_fft_a(z, gc, 2, 1, taba), h, 0, tab2)
    y1 = _fft_d(p, tabd, z, 0, z, 2, fb[:, :C])
    p = _fft_b(_fft_a(y1, gc, 0, 1, taba), h, 1, tab2)
    return _fft_d(p, tabd, z, 1, y1, 0, fb[:, C:])


def _short_conv_kernel(z_ref, f_ref, ss_ref, fb_ref, cf_ref, sf_ref, cd_ref, sd_ref, o_ref):
    L = z_ref.shape[1]
    G = o_ref.shape[0]
    C = G * LANES
    N = 2 * L

    def dft(xb):
        return (jnp.dot(cf_ref[...], xb, preferred_element_type=F32),
                -jnp.dot(sf_ref[...], xb, preferred_element_type=F32))

    def conv(u, blk, gate):
        sl = slice(blk * C, (blk + 1) * C)
        ur, ui = dft(u.astype(BF16))
        fr, fi = dft(_cg_cat(f_ref.at[blk * G:(blk + 1) * G]).astype(BF16))
        q = lax.broadcasted_iota(jnp.int32, fr.shape, 0) & 3
        fr, fi = (jnp.where(q == 0, fr, jnp.where(q == 1, -fi, jnp.where(q == 2, -fr, fi))),
                  jnp.where(q == 0, fi, jnp.where(q == 1, fr, jnp.where(q == 2, -fi, -fr))))
        yr = (ur * fr - ui * fi).astype(BF16)
        yi = (ur * fi + ui * fr).astype(BF16)
        y = (jnp.dot(cd_ref[...], yr, preferred_element_type=F32)
             - jnp.dot(sd_ref[...], yi, preferred_element_type=F32))
        y = y * (lax.rsqrt(ss_ref[:, sl] + H_DECAY_EPS) * (1.0 / N))
        return gate * (y + u * fb_ref[:, sl])

    x1 = _cg_cat(z_ref.at[0:G])
    x2 = _cg_cat(z_ref.at[G:2 * G])
    v = _cg_cat(z_ref.at[2 * G:3 * G])
    _cg_put(o_ref, conv(conv(v, 0, x1), 1, x2))


def _hyena_short(z, filt, ssq, fbias):
    L = z.shape[1]
    G = z.shape[0] // 3
    C = G * LANES
    N = 2 * L
    k = jnp.arange(N, dtype=jnp.int32)
    n = jnp.arange(L, dtype=jnp.int32)
    ang = ((k[:, None] * n[None, :]) % N).astype(F32) * (2.0 * math.pi / N)
    cf, sf = jnp.cos(ang).astype(BF16), jnp.sin(ang).astype(BF16)
    args = (z, filt, ssq, fbias.reshape(1, 2 * C), cf, sf, cf.T, sf.T)
    return pl.pallas_call(
        _short_conv_kernel,
        grid=(1,),
        in_specs=[_full(a.shape) for a in args],
        out_specs=_full((G, L, LANES)),
        out_shape=jax.ShapeDtypeStruct((G, L, LANES), F32),
        compiler_params=_cparams("arbitrary"),
        name="hyena_short_conv",
    )(*args)


def _router_kernel(x_ref, g_ref, mod_ref, rw_ref, rb_ref, c0_ref, xt_ref, rt_ref, cnt_ref, carry):
    i = pl.program_id(0)

    @pl.when(i == 0)
    def _():
        carry[...] = c0_ref[...]

    u = _norm_mod(x_ref[...], g_ref[...], mod_ref[...], 3)
    _tt_store(xt_ref, u)
    logits = jnp.dot(u, rw_ref[...], precision=HI, preferred_element_type=F32) + rb_ref[...]
    tm = u.shape[0]
    lane = lax.broadcasted_iota(jnp.int32, logits.shape, 1).astype(F32)
    ninf = -jnp.inf

    def top(vals):
        mx = jnp.max(vals, axis=-1, keepdims=True)
        ix = jnp.min(jnp.where(vals == mx, lane, float(LANES)), axis=-1, keepdims=True)
        return mx, ix

    lg = jnp.where(lane < N_GROUPS, logits, ninf)
    gmax, gidx = top(lg)
    g_w = 1.0 / jnp.sum(jnp.exp(lg - gmax), axis=-1, keepdims=True)
    lo = N_GROUPS + EXP_PER_GROUP * gidx
    le = jnp.where((lane >= lo) & (lane < lo + EXP_PER_GROUP), logits, ninf)
    m1, i1 = top(le)
    m2, i2 = top(jnp.where(lane == i1, ninf, le))
    r = jnp.exp(m2 - m1)
    w1 = g_w / (1.0 + r)
    w2 = w1 * r
    e1 = i1 - N_GROUPS
    e2 = i2 - N_GROUPS
    oh1 = lane == e1
    oh2 = lane == e2
    both = jnp.where(oh1 | oh2, 1.0, 0.0)
    rr = lax.broadcasted_iota(jnp.int32, (tm, tm), 0)
    cc = lax.broadcasted_iota(jnp.int32, (tm, tm), 1)
    earlier = jnp.where(cc < rr, 1.0, 0.0).astype(BF16)
    before = jnp.dot(earlier, both.astype(BF16), preferred_element_type=F32) + carry[...]
    rank1 = jnp.sum(jnp.where(oh1, before, 0.0), axis=-1, keepdims=True)
    rank2 = jnp.sum(jnp.where(oh2, before, 0.0), axis=-1, keepdims=True)
    carry[...] += jnp.sum(both, axis=0, keepdims=True)
    cnt_ref[...] = carry[...]
    out = jnp.zeros(logits.shape, F32)
    for j, col in enumerate((e1, e2, w1, w2, rank1, rank2)):
        out = jnp.where(lane == j, col, out)
    rt_ref[...] = out


def _router(h, g, mod, rw, rb, counts0):
    L, D = h.shape
    tm = min(256, L)
    return pl.pallas_call(
        _router_kernel,
        grid=(L // tm,),
        in_specs=[pl.BlockSpec((tm, D), lambda i: (i, 0)), _full((1, D)), _full(mod.shape),
                  _full(rw.shape), _full(rb.shape), _full((1, LANES))],
        out_specs=[pl.BlockSpec((tm, D // LANES, LANES), lambda i: (i, 0, 0)),
                   pl.BlockSpec((tm, LANES), lambda i: (i, 0)), _full((1, LANES))],
        out_shape=[jax.ShapeDtypeStruct((L, D // LANES, LANES), F32), jax.ShapeDtypeStruct((L, LANES), F32),
                   jax.ShapeDtypeStruct((1, LANES), F32)],
        scratch_shapes=[pltpu.VMEM((1, LANES), F32)],
        compiler_params=_cparams("arbitrary"),
        name="moe_router",
    )(h, g.reshape(1, D), mod, rw, rb, counts0)


def _dispatch_kernel(dest_ref, x_ref, xb_in_ref, xb_ref, sem):
    del xb_in_ref
    tm = x_ref.shape[0]
    a0 = 2 * tm * pl.program_id(0)

    def issue(r, carry):
        for k in range(2):
            pltpu.make_async_copy(x_ref.at[r], xb_ref.at[dest_ref[a0 + 2 * r + k]], sem).start()
        return carry

    def wait(r, carry):
        pltpu.make_async_copy(x_ref.at[0], xb_ref.at[0], sem).wait()
        return carry

    lax.fori_loop(0, tm, issue, 0, unroll=DMA_UNROLL)
    lax.fori_loop(0, 2 * tm, wait, 0, unroll=DMA_UNROLL)


def _dispatch(xt, dest, R):
    T, G, _ = xt.shape
    tm = min(256, T)
    return pl.pallas_call(
        _dispatch_kernel,
        grid_spec=pltpu.PrefetchScalarGridSpec(
            num_scalar_prefetch=1,
            grid=(T // tm,),
            in_specs=[pl.BlockSpec((tm, G, LANES), lambda i, d: (i, 0, 0)), pl.BlockSpec(memory_space=pl.ANY)],
            out_specs=pl.BlockSpec(memory_space=pl.ANY),
            scratch_shapes=[pltpu.SemaphoreType.DMA(())]),
        out_shape=jax.ShapeDtypeStruct((R, G, LANES), F32),
        input_output_aliases={2: 0},
        compiler_params=_cparams("arbitrary"),
        name="moe_dispatch",
    )(dest, xt, jnp.zeros((R, G, LANES), F32))


def _expert_kernel(be_ref, nb_ref, x_ref, w13_ref, w2_ref, o_ref, w13b, w2b):
    b = pl.program_id(0)
    live = b < nb_ref[0]
    new_expert = (b == 0) | (be_ref[b] != be_ref[jnp.maximum(b - 1, 0)])

    @pl.when(jnp.logical_not(live))
    def _():
        o_ref[...] = jnp.zeros_like(o_ref)

    @pl.when(live & new_expert)
    def _():
        w13b[...] = w13_ref[0].astype(BF16)
        w2b[...] = w2_ref[0].astype(BF16)

    @pl.when(live)
    def _():
        de = w2b.shape[0]
        x = _tt_load(x_ref).astype(BF16)
        hcat = jnp.dot(x, w13b[...], preferred_element_type=F32)
        a = (_silu(hcat[:, :de]) * hcat[:, de:]).astype(BF16)
        _tt_store(o_ref, jnp.dot(a, w2b[...], preferred_element_type=F32))


def _experts(xb, blk_exp, n_used, w13, w2):
    R, G, _ = xb.shape
    D = G * LANES
    de = w2.shape[1]
    nb = R // MOE_BLOCK
    blk = pl.BlockSpec((MOE_BLOCK, G, LANES), lambda b, be, nu: (b, 0, 0))
    return pl.pallas_call(
        _expert_kernel,
        grid_spec=pltpu.PrefetchScalarGridSpec(
            num_scalar_prefetch=2,
            grid=(nb,),
            in_specs=[blk,
                      pl.BlockSpec((1, D, 2 * de), lambda b, be, nu: (be[b], 0, 0)),
                      pl.BlockSpec((1, de, D), lambda b, be, nu: (be[b], 0, 0))],
            out_specs=blk,
            scratch_shapes=[pltpu.VMEM((D, 2 * de), BF16), pltpu.VMEM((de, D), BF16)]),
        out_shape=jax.ShapeDtypeStruct(xb.shape, F32),
        compiler_params=_cparams("arbitrary"),
        name="moe_experts",
    )(blk_exp, n_used, xb, w13, w2)


def _combine_kernel(dest_ref, h_ref, rt_ref, mod_ref, yb_ref, o_ref, buf, sem, *, a_base):
    tm = h_ref.shape[0]
    a0 = a_base + 2 * tm * pl.program_id(0)

    def issue(r, carry):
        for k in range(2):
            pltpu.make_async_copy(yb_ref.at[dest_ref[a0 + 2 * r + k]], buf.at[k, r], sem).start()
        return carry

    def wait(r, carry):
        pltpu.make_async_copy(yb_ref.at[0], buf.at[0, 0], sem).wait()
        return carry

    lax.fori_loop(0, tm, issue, 0, unroll=DMA_UNROLL)
    lax.fori_loop(0, 2 * tm, wait, 0, unroll=DMA_UNROLL)
    rt = rt_ref[...]
    y = rt[:, 2:3] * _tt_load(buf.at[0]) + rt[:, 3:4] * _tt_load(buf.at[1])
    o_ref[...] = h_ref[...] + mod_ref[5:6, :] * y


def _combine(h, yb, dest, row0, route, mod):
    L, D = h.shape
    tm = min(256, L)
    return pl.pallas_call(
        functools.partial(_combine_kernel, a_base=2 * row0),
        grid_spec=pltpu.PrefetchScalarGridSpec(
            num_scalar_prefetch=1,
            grid=(L // tm,),
            in_specs=[pl.BlockSpec((tm, D), lambda i, d: (i, 0)), pl.BlockSpec((tm, LANES), lambda i, d: (i, 0)),
                      pl.BlockSpec(mod.shape, lambda i, d: (0, 0)), pl.BlockSpec(memory_space=pl.ANY)],
            out_specs=pl.BlockSpec((tm, D), lambda i, d: (i, 0)),
            scratch_shapes=[pltpu.VMEM((2, tm, D // LANES, LANES), F32), pltpu.SemaphoreType.DMA(())]),
        out_shape=jax.ShapeDtypeStruct((L, D), F32),
        compiler_params=_cparams("arbitrary"),
        name="moe_combine",
    )(dest, h, route, mod, yb)


def _moe(streams, g, rw, rb, w13, w2, layer):
    counts = jnp.zeros((1, LANES), F32)
    xts, routes = [], []
    for h, mod in streams:
        xt, rt, counts = _router(h, g, mod, rw, rb, counts)
        xts.append(xt)
        routes.append(rt)
    xt = jnp.concatenate(xts, axis=0) if len(xts) > 1 else xts[0]
    route = jnp.concatenate(routes, axis=0) if len(routes) > 1 else routes[0]
    T = xt.shape[0]
    A = 2 * T
    cnt = counts[0, :N_EXPERTS].astype(jnp.int32)
    padded = (cnt + MOE_BLOCK - 1) // MOE_BLOCK * MOE_BLOCK
    p_ends = jnp.cumsum(padded)
    p_starts = p_ends - padded
    experts = route[:, 0:2].astype(jnp.int32)
    onehot = experts[:, :, None] == jnp.arange(N_EXPERTS, dtype=jnp.int32)
    dest = (jnp.sum(jnp.where(onehot, p_starts, 0), axis=-1) + route[:, 4:6].astype(jnp.int32)).reshape(A)
    n_blocks = -(-A // MOE_BLOCK) + N_EXPERTS
    blk_start = jnp.arange(n_blocks, dtype=jnp.int32) * MOE_BLOCK
    blk_exp = jnp.minimum(jnp.sum(p_ends[None, :] <= blk_start[:, None], axis=1), N_EXPERTS - 1).astype(jnp.int32)
    blk_exp = blk_exp + layer * N_EXPERTS
    n_used = (p_ends[-1:] // MOE_BLOCK).astype(jnp.int32)
    yb = _experts(_dispatch(xt, dest, n_blocks * MOE_BLOCK), blk_exp, n_used, w13, w2)
    outs, row0 = [], 0
    for (h, mod), rt in zip(streams, routes):
        outs.append(_combine(h, yb, dest, row0, rt, mod))
        row0 += h.shape[0]
    return outs


def _final_kernel(x_ref, g_ref, o_ref):
    x = x_ref[...]
    o_ref[...] = x * lax.rsqrt(jnp.mean(x * x, axis=-1, keepdims=True) + EPS) * g_ref[...]


def _final_norm(h, g):
    L, D = h.shape
    tm = min(512, L)
    return pl.pallas_call(
        _final_kernel,
        grid=(L // tm,),
        in_specs=[pl.BlockSpec((tm, D), lambda i: (i, 0)), _full((1, D))],
        out_specs=pl.BlockSpec((tm, D), lambda i: (i, 0)),
        out_shape=jax.ShapeDtypeStruct((L, D), F32),
        compiler_params=_cparams("parallel"),
        name="final_norm",
    )(h, g.reshape(1, D))


def _zero_state(heads, dqk, dv):
    return (jnp.zeros((heads, dqk, dv), F32), jnp.zeros((heads, 1, dqk), F32), jnp.zeros((heads, 1, LANES), F32))


def kernel(x, c, ctx, c_ctx, ada_w, ada_b, norm1_g, norm2_g, rg_w, rg_b, re_w, re_b, moe_w13, moe_w2, m_in_w, m_conv_w, m_conv_b, m_q_w, m_k_w, m_v_w, m_gate_w, m_gate_b, m_norm_g, m_skip, m_out_w, h_in_w, h_in_b, h_sc_w, h_sc_b, h_f_w1, h_f_b1, h_f_freq, h_f_w2, h_f_b2, h_f_w3, h_decay, h_fbias, h_out_w, h_out_b, final_g):
    B, S, D = x.shape
    assert B == 1, "kernel is written for the single-sequence problem shape"
    depth = ada_w.shape[0]
    hl, hc = x[0], ctx[0]
    cc = jnp.zeros((8, D), F32).at[0].set(c[0]).at[1].set(c_ctx)
    mods = _ada(cc, ada_w, ada_b).reshape(depth, 8, 6, D)
    pad8 = lambda m: jnp.concatenate([m, jnp.zeros((2, D), F32)], axis=0)
    nrt = N_GROUPS + N_EXPERTS
    zeros_d = jnp.zeros((D,), F32)
    tables = _dft_tables(S) if depth > 1 else None
    w13_all = moe_w13.reshape((-1,) + moe_w13.shape[2:])
    w2_all = moe_w2.reshape((-1,) + moe_w2.shape[2:])

    for i in range(depth):
        kind, slot = i % N_MIXERS, i // N_MIXERS
        col_major = (slot % 2) == 1
        ctx_live = any(j % N_MIXERS == 0 for j in range(i + 1, depth))
        ml, mc = pad8(mods[i, 0]), pad8(mods[i, 1])

        if kind == 0:
            w_in = m_in_w[slot].astype(BF16)
            w_out = m_out_w[slot].astype(BF16)
            inner = w_out.shape[0]
            mp = (m_conv_w[slot], m_conv_b[slot], m_q_w[slot], m_k_w[slot], m_v_w[slot], m_gate_w[slot],
                  m_gate_b[slot], m_norm_g[slot], m_skip[slot])
            heads, _, dqk = m_q_w[slot].shape
            dv = m_v_w[slot].shape[2]
            zb = jnp.zeros((w_in.shape[1],), F32)
            xo_c = _norm_proj(hc, norm1_g[i], mc, w_in, zb, False)
            yc, st_f, st_b = _mlstm_stream(xo_c, mp, _zero_state(heads, dqk, dv), _zero_state(heads, dqk, dv))
            xo_l = _norm_proj(hl, norm1_g[i], ml, w_in, zb, col_major)
            yl, _, _ = _mlstm_stream(xo_l, mp, st_f, st_b)
            hl = _out_proj(yl, w_out, zeros_d, hl, ml, 2, col_major)
            if ctx_live:
                hc = _out_proj(yc, w_out, zeros_d, hc, mc, 2, False)
        else:
            w_in = h_in_w[slot].astype(BF16)
            w_out = h_out_w[slot].astype(BF16)
            fp = (h_f_w1[slot], h_f_b1[slot], h_f_freq[slot], h_f_w2[slot], h_f_b2[slot], h_f_w3[slot], h_decay[slot])
            zl = _sconv(_norm_proj(hl, norm1_g[i], ml, w_in, h_in_b[slot], col_major), h_sc_w[slot], h_sc_b[slot])
            filt, ssq = _hyena_filters(S, *fp)
            yl = _hyena_long(zl, filt, ssq, h_fbias[slot], tables)
            hl = _out_proj(yl, w_out, h_out_b[slot], hl, ml, 2, col_major)
            if ctx_live:
                zc = _sconv(_norm_proj(hc, norm1_g[i], mc, w_in, h_in_b[slot], False), h_sc_w[slot], h_sc_b[slot])
                filt_c, ssq_c = _hyena_filters(hc.shape[0], *fp)
                yc = _hyena_short(zc, filt_c, ssq_c, h_fbias[slot])
                hc = _out_proj(yc, w_out, h_out_b[slot], hc, mc, 2, False)

        rw = jnp.zeros((D, LANES), F32).at[:, :N_GROUPS].set(rg_w[i]).at[:, N_GROUPS:nrt].set(re_w[i])
        rb = jnp.zeros((1, LANES), F32).at[0, :N_GROUPS].set(rg_b[i]).at[0, N_GROUPS:nrt].set(re_b[i])
        if ctx_live:
            hc, hl = _moe([(hc, mc), (hl, ml)], norm2_g[i], rw, rb, w13_all, w2_all, i)
        else:
            (hl,) = _moe([(hl, ml)], norm2_g[i], rw, rb, w13_all, w2_all, i)

    return _final_norm(hl, final_g)[None]
```

```python
import functools
import math

import jax
import jax.numpy as jnp
from jax import lax
from jax.experimental import pallas as pl
from jax.experimental.pallas import tpu as pltpu

F32 = jnp.float32
BF16 = jnp.bfloat16
U32 = jnp.uint32
HI = lax.Precision.HIGHEST

EPS = 1e-6
GRID_W = 64
N_MIXERS = 2
M_HEADS = 4
H_BANDS = 16
N_GROUPS = 4
EXP_PER_GROUP = 8
N_EXPERTS = N_GROUPS * EXP_PER_GROUP
MOE_BLOCK = 256
DMA_UNROLL = 8
LANES = 128
HALO = 16
SUB = 8
FFT_N2 = 128
FFT_KB = 2
H_DECAY_EPS = EPS
VMEM_LIMIT = 56 * 1024 * 1024


def _cparams(*sem):
    return pltpu.CompilerParams(dimension_semantics=sem, vmem_limit_bytes=VMEM_LIMIT)


def _full(shape):
    n = len(shape)
    return pl.BlockSpec(shape, lambda *_: (0,) * n)


def _raster(L, tr_pref=128):
    rows = L // GRID_W
    return rows, min(tr_pref, rows)


def _cg_cat(ref):
    return jnp.concatenate([ref[s] for s in range(ref.shape[0])], axis=1)


def _cg_put(ref, val):
    for s in range(ref.shape[0]):
        ref[s] = val[:, s * LANES:(s + 1) * LANES]


def _cg_load(ref4, r):
    G, n = ref4.shape[0], ref4.shape[1]
    flat = ref4.reshape(G, n * SUB, LANES)
    return jnp.concatenate([flat[s, pl.ds(r, n, stride=SUB), :] for s in range(G)], axis=1)


def _cg_store(ref4, r, val):
    G, n = ref4.shape[0], ref4.shape[1]
    flat = ref4.reshape(G, n * SUB, LANES)
    for s in range(G):
        flat[s, pl.ds(r, n, stride=SUB), :] = val[:, s * LANES:(s + 1) * LANES]


def _tt_load(ref3):
    n, G = ref3.shape[0], ref3.shape[1]
    flat = ref3.reshape(n * G, LANES)
    return jnp.concatenate([flat[pl.ds(s, n, stride=G), :] for s in range(G)], axis=1)


def _tt_store(ref3, val):
    n, G = ref3.shape[0], ref3.shape[1]
    flat = ref3.reshape(n * G, LANES)
    for s in range(G):
        flat[pl.ds(s, n, stride=G), :] = val[:, s * LANES:(s + 1) * LANES]


def _pack(re, im):
    hi = pltpu.bitcast(re.astype(BF16).astype(F32), U32)
    lo = pltpu.bitcast(im.astype(BF16).astype(F32), U32)
    return (hi & jnp.uint32(0xFFFF0000)) | (lo >> 16)


def _unpack(w):
    re = pltpu.bitcast(w & jnp.uint32(0xFFFF0000), F32)
    im = pltpu.bitcast(w << 16, F32)
    return re.astype(BF16), im.astype(BF16)


def _sigmoid(x):
    return 1.0 / (1.0 + jnp.exp(-x))


def _silu(x):
    return x * _sigmoid(x)


def _norm_mod(x, g, mod, k):
    ms = jnp.mean(x * x, axis=-1, keepdims=True)
    y = x * lax.rsqrt(ms + EPS) * g
    return y * (1.0 + mod[k + 1:k + 2, :]) + mod[k:k + 1, :]


def _ada_kernel(c_ref, w_ref, b_ref, o_ref):
    s = _silu(c_ref[...])
    o_ref[0] = jnp.dot(s, w_ref[0], precision=HI, preferred_element_type=F32) + b_ref[0]


def _ada(cc, ada_w, ada_b):
    depth, D, N = ada_w.shape
    tn = N // 6
    return pl.pallas_call(
        _ada_kernel,
        grid=(depth, N // tn),
        in_specs=[pl.BlockSpec((8, D), lambda l, j: (0, 0)),
                  pl.BlockSpec((1, D, tn), lambda l, j: (l, 0, j)),
                  pl.BlockSpec((1, 1, tn), lambda l, j: (l, 0, j))],
        out_specs=pl.BlockSpec((1, 8, tn), lambda l, j: (l, 0, j)),
        out_shape=jax.ShapeDtypeStruct((depth, 8, N), F32),
        compiler_params=_cparams("parallel", "parallel"),
        name="ada",
    )(cc, ada_w, ada_b.reshape(depth, 1, N))


def _raster_col(refs, c):
    tr = refs[0].shape[0]
    return jnp.concatenate([r.reshape(tr * SUB, LANES)[pl.ds(c, tr, stride=SUB), :] for r in refs], axis=1)


def _lane_blocks(tr, D, index):
    return [pl.BlockSpec((tr, SUB, LANES), functools.partial(index, s)) for s in range(D // LANES)]


def _norm_proj_kernel(*refs, nchunk, col_major):
    *x_refs, g_ref, mod_ref, w_ref, b_ref, o_ref = refs
    N = o_ref.shape[-1]
    step = N // nchunk
    for c in range(SUB if col_major else 1):
        x = _raster_col(x_refs, c) if col_major else x_refs[0][...]
        u = _norm_mod(x, g_ref[...], mod_ref[...], 0).astype(BF16)
        for j in range(nchunk):
            sl = slice(j * step, (j + 1) * step)
            acc = (jnp.dot(u, w_ref[:, sl], preferred_element_type=F32) + b_ref[:, sl]).astype(o_ref.dtype)
            if col_major:
                o_ref[c, :, sl] = acc
            else:
                o_ref[:, sl] = acc


def _norm_proj(x, g, mod, w, b, col_major):
    L, D = x.shape
    N = w.shape[1]
    consts = [_full((1, D)), _full(mod.shape), _full((D, N)), _full((1, N))]
    kern = functools.partial(_norm_proj_kernel, nchunk=max(1, N // 1024), col_major=col_major)
    args = (g.reshape(1, D), mod, w, b.reshape(1, N))
    if col_major:
        rows, tr = _raster(L)
        out = pl.pallas_call(
            kern,
            grid=(GRID_W // SUB, rows // tr),
            in_specs=_lane_blocks(tr, D, lambda s, iw, ir: (ir, iw, s)) + consts,
            out_specs=pl.BlockSpec((SUB, tr, N), lambda iw, ir: (iw, ir, 0)),
            out_shape=jax.ShapeDtypeStruct((GRID_W, rows, N), BF16),
            compiler_params=_cparams("parallel", "parallel"),
            name="norm_proj_colmajor",
        )(*([x.reshape(rows, GRID_W, D)] * (D // LANES)), *args)
        return out.reshape(L, N)
    tm = min(256, L)
    return pl.pallas_call(
        kern,
        grid=(L // tm,),
        in_specs=[pl.BlockSpec((tm, D), lambda i: (i, 0))] + consts,
        out_specs=pl.BlockSpec((tm, N), lambda i: (i, 0)),
        out_shape=jax.ShapeDtypeStruct((L, N), BF16),
        compiler_params=_cparams("parallel"),
        name="norm_proj",
    )(x, *args)


def _out_proj_kernel(y_ref, w_ref, b_ref, *refs, gate_row, col_major, y_cg):
    *h_refs, mod_ref, o_ref = refs
    gate = mod_ref[gate_row:gate_row + 1, :]
    for c in range(SUB if col_major else 1):
        if y_cg:
            y = jnp.concatenate([y_ref[s, c] if col_major else y_ref[s] for s in range(y_ref.shape[0])],
                                axis=1).astype(BF16)
        else:
            y = y_ref[c] if col_major else y_ref[...]
        acc = jnp.dot(y, w_ref[...], preferred_element_type=F32) + b_ref[...]
        if col_major:
            o_ref[:, c, :] = _raster_col(h_refs, c) + gate * acc
        else:
            o_ref[...] = h_refs[0][...] + gate * acc


def _out_proj(y, w, b, h, mod, gate_row, col_major):
    L, D = h.shape
    K = w.shape[0]
    y_cg = y.ndim == 3
    G = K // LANES
    kern = functools.partial(_out_proj_kernel, gate_row=gate_row, col_major=col_major, y_cg=y_cg)
    if col_major:
        rows, tr = _raster(L)
        hspec = pl.BlockSpec((tr, SUB, D), lambda iw, ir: (ir, iw, 0))
        if y_cg:
            yspec = pl.BlockSpec((G, SUB, tr, LANES), lambda iw, ir: (0, iw, ir, 0))
            yv = y.reshape(G, GRID_W, rows, LANES)
        else:
            yspec = pl.BlockSpec((SUB, tr, K), lambda iw, ir: (iw, ir, 0))
            yv = y.reshape(GRID_W, rows, K)
        out = pl.pallas_call(
            kern,
            grid=(GRID_W // SUB, rows // tr),
            in_specs=[yspec, _full((K, D)), _full((1, D))] + _lane_blocks(tr, D, lambda s, iw, ir: (ir, iw, s))
            + [_full(mod.shape)],
            out_specs=hspec,
            out_shape=jax.ShapeDtypeStruct((rows, GRID_W, D), F32),
            compiler_params=_cparams("parallel", "parallel"),
            name="out_proj_colmajor",
        )(yv, w, b.reshape(1, D), *([h.reshape(rows, GRID_W, D)] * (D // LANES)), mod)
        return out.reshape(L, D)
    tm = min(256, L)
    rows_spec = lambda c: pl.BlockSpec((tm, c), lambda i: (i, 0))
    yspec = pl.BlockSpec((G, tm, LANES), lambda i: (0, i, 0)) if y_cg else rows_spec(K)
    return pl.pallas_call(
        kern,
        grid=(L // tm,),
        in_specs=[yspec, _full((K, D)), _full((1, D)), rows_spec(D), _full(mod.shape)],
        out_specs=rows_spec(D),
        out_shape=jax.ShapeDtypeStruct((L, D), F32),
        compiler_params=_cparams("parallel"),
        name="out_proj",
    )(y, w, b.reshape(1, D), h, mod)


def _halo_specs(tm, L, ncol, cblock):
    hb = tm // HALO
    nb = L // HALO
    prev = pl.BlockSpec((HALO, ncol), lambda i: (jnp.maximum(i * hb - 1, 0), cblock))
    nxt = pl.BlockSpec((HALO, ncol), lambda i: (jnp.minimum((i + 1) * hb, nb - 1), cblock))
    return prev, nxt


def _conv3(x, prev_ref, next_ref, w, b):
    tm = x.shape[0]
    i = pl.program_id(0)
    first = i == 0
    last = i == pl.num_programs(0) - 1
    p = jnp.where(first, 0.0, prev_ref[HALO - 1:HALO, :].astype(F32))
    n = jnp.where(last, 0.0, next_ref[0:1, :].astype(F32))
    row = lax.broadcasted_iota(jnp.int32, x.shape, 0)
    xm1 = jnp.where(row == 0, p, pltpu.roll(x, 1, 0))
    xp1 = jnp.where(row == tm - 1, n, pltpu.roll(x, tm - 1, 0))
    return b + xm1 * w[0:1, :] + x * w[1:2, :] + xp1 * w[2:3, :]


def _mlstm_pre_kernel(x_ref, p_ref, n_ref, cw_ref, cb_ref, qw_ref, kw_ref, vw_ref, gw_ref, gb_ref,
                      xc_ref, q_ref, k_ref, v_ref, g_ref, gt_ref, *, heads, kscale):
    xm = x_ref[...]
    xc = _silu(_conv3(xm.astype(F32), p_ref, n_ref, cw_ref[...], cb_ref[...]))
    xcb = xc.astype(BF16)
    xc_ref[...] = xcb
    inner = xm.shape[1]
    dh = inner // heads
    dqk = qw_ref.shape[2]
    dv = vw_ref.shape[2]
    for h in range(heads):
        xs = xcb[:, h * dh:(h + 1) * dh]
        q_ref[:, h * dqk:(h + 1) * dqk] = jnp.dot(xs, qw_ref[h], preferred_element_type=F32).astype(BF16)
        kh = jnp.dot(xs, kw_ref[h], preferred_element_type=F32) * kscale
        k_ref[:, h * dqk:(h + 1) * dqk] = kh.astype(BF16)
        v_ref[:, h * dv:(h + 1) * dv] = jnp.dot(xm[:, h * dh:(h + 1) * dh], vw_ref[h],
                                                 preferred_element_type=F32).astype(BF16)
    g = (jnp.dot(xcb, gw_ref[:inner, :], preferred_element_type=F32)
         + jnp.dot(xm, gw_ref[inner:, :], preferred_element_type=F32) + gb_ref[...])
    lane = lax.broadcasted_iota(jnp.int32, g.shape, 1)
    is_forget = ((lane >= heads) & (lane < 2 * heads)) | ((lane >= 3 * heads) & (lane < 4 * heads))
    logsig = jnp.minimum(g, 0.0) - jnp.log(1.0 + jnp.exp(-jnp.abs(g)))
    g = jnp.where(is_forget, logsig, g)
    g_ref[...] = g
    gt_ref[...] = g.T[:gt_ref.shape[0], :]


def _mlstm_pre(xo, conv_w, conv_b, q_w, k_w, v_w, gate_w, gate_b):
    L = xo.shape[0]
    heads, dh, dqk = q_w.shape
    dv = v_w.shape[2]
    inner = heads * dh
    tm = min(256, L)
    ng = 4 * heads
    gw = jnp.zeros((2 * inner, LANES), BF16).at[:, :ng].set(gate_w.astype(BF16))
    gb = jnp.zeros((1, LANES), F32).at[0, :ng].set(gate_b)
    prev, nxt = _halo_specs(tm, L, inner, 0)
    rows = lambda c: pl.BlockSpec((tm, c), lambda i: (i, 0))
    return pl.pallas_call(
        functools.partial(_mlstm_pre_kernel, heads=heads, kscale=dqk ** -0.5),
        grid=(L // tm,),
        in_specs=[rows(inner), prev, nxt, _full((3, inner)), _full((1, inner)),
                  _full(q_w.shape), _full(k_w.shape), _full(v_w.shape), _full(gw.shape), _full(gb.shape)],
        out_specs=[rows(inner), rows(heads * dqk), rows(heads * dqk), rows(heads * dv), rows(LANES),
                   pl.BlockSpec((ng, tm), lambda i: (0, i))],
        out_shape=[jax.ShapeDtypeStruct((L, inner), BF16), jax.ShapeDtypeStruct((L, heads * dqk), BF16),
                   jax.ShapeDtypeStruct((L, heads * dqk), BF16), jax.ShapeDtypeStruct((L, heads * dv), BF16),
                   jax.ShapeDtypeStruct((L, LANES), F32), jax.ShapeDtypeStruct((ng, L), F32)],
        compiler_params=_cparams("parallel"),
        name="mlstm_pre",
    )(xo, xo, xo, conv_w, conv_b.reshape(1, inner), q_w.astype(BF16), k_w.astype(BF16), v_w.astype(BF16), gw, gb)


def _scan_kernel(*refs, heads, rev, final):
    if final:
        (q_ref, k_ref, v_ref, g_ref, gt_ref, c0_ref, n0_ref, m0_ref, hf_ref, xc_ref, o_ref, ng_ref, sk_ref,
         out_ref, cT_ref, nT_ref, mT_ref, C_sc, n_sc, m_sc) = refs
    else:
        (q_ref, k_ref, v_ref, g_ref, gt_ref, c0_ref, n0_ref, m0_ref,
         out_ref, cT_ref, nT_ref, mT_ref, C_sc, n_sc, m_sc) = refs
    ci = pl.program_id(0)

    @pl.when(ci == 0)
    def _():
        C_sc[...] = c0_ref[...]
        n_sc[...] = n0_ref[...]
        m_sc[...] = m0_ref[...]

    Lc = q_ref.shape[0]
    dqk = q_ref.shape[1] // heads
    dv = v_ref.shape[1] // heads
    G = g_ref[...]
    GT = gt_ref[...]
    r = lax.broadcasted_iota(jnp.int32, (Lc, Lc), 0)
    c = lax.broadcasted_iota(jnp.int32, (Lc, Lc), 1)
    mask = (c >= r) if rev else (c <= r)
    tri = jnp.where(mask, 1.0, 0.0)
    Bc = jnp.dot(tri, G, precision=HI, preferred_element_type=F32)
    BT = lax.dot_general(GT, tri, (((1,), (1,)), ((), ())), precision=HI, preferred_element_type=F32)
    tot = Bc[0:1, :] if rev else Bc[Lc - 1:Lc, :]
    base = 2 * heads if rev else 0
    for h in range(heads):
        ci_, cf_ = base + h, base + heads + h
        b_col = Bc[:, cf_:cf_ + 1]
        i_col = G[:, ci_:ci_ + 1]
        bT = BT[cf_:cf_ + 1, :]
        iT = GT[ci_:ci_ + 1, :]
        b_last = tot[:, cf_:cf_ + 1]
        m = m_sc[h][:, 0:1]
        logd = jnp.where(mask, b_col - bT + iT, -jnp.inf)
        inter = b_col + m
        mj = jnp.maximum(inter, jnp.max(logd, axis=-1, keepdims=True))
        dmat = jnp.exp(logd - mj)
        qh = q_ref[:, h * dqk:(h + 1) * dqk]
        kh = k_ref[:, h * dqk:(h + 1) * dqk]
        vh = v_ref[:, h * dv:(h + 1) * dv]
        s = lax.dot_general(qh, kh, (((1,), (1,)), ((), ())), preferred_element_type=F32) * dmat
        w_inter = jnp.exp(inter - mj)
        Ct = C_sc[h]
        nrow = n_sc[h]
        num = (jnp.dot(s.astype(BF16), vh, preferred_element_type=F32)
               + w_inter * jnp.dot(qh, Ct.astype(BF16), preferred_element_type=F32))
        den = (jnp.sum(s, axis=-1, keepdims=True)
               + w_inter * jnp.sum(qh.astype(F32) * nrow, axis=-1, keepdims=True))
        hout = num / jnp.maximum(jnp.abs(den), jnp.exp(-mj))
        g_col = b_last - b_col + i_col
        g_row = b_last - bT + iT
        m_new = jnp.maximum(b_last + m, jnp.max(g_row, axis=-1, keepdims=True))
        decay = jnp.exp(b_last + m - m_new)
        wg = jnp.exp(g_col - m_new)
        vw = (vh.astype(F32) * wg).astype(BF16)
        C_sc[h] = decay * Ct + lax.dot_general(kh, vw, (((0,), (0,)), ((), ())), preferred_element_type=F32)
        n_sc[h] = decay * nrow + jnp.sum(kh.astype(F32) * wg, axis=0, keepdims=True)
        m_sc[h] = jnp.broadcast_to(m_new, m_sc.shape[1:])
        cs = slice(h * dv, (h + 1) * dv)
        if final:
            hs = hf_ref[:, cs] + hout
            mu = jnp.mean(hs, axis=-1, keepdims=True)
            d = hs - mu
            var = jnp.mean(d * d, axis=-1, keepdims=True)
            hn = d * lax.rsqrt(var + EPS)
            y = (hn * ng_ref[:, cs] + sk_ref[:, cs] * xc_ref[:, cs].astype(F32)) * _sigmoid(o_ref[:, cs].astype(F32))
            out_ref[:, cs] = y.astype(out_ref.dtype)
        else:
            out_ref[:, cs] = hout

    @pl.when(ci == pl.num_programs(0) - 1)
    def _():
        cT_ref[...] = C_sc[...]
        nT_ref[...] = n_sc[...]
        mT_ref[...] = m_sc[...]


def _mlstm_scan(q, k, v, g, gt, state, rev, final_args=None):
    L = q.shape[0]
    heads = M_HEADS
    dqk = q.shape[1] // heads
    dv = v.shape[1] // heads
    Lc = min(256, L)
    nc = L // Lc
    idx = (lambda i: (nc - 1 - i, 0)) if rev else (lambda i: (i, 0))
    idx_t = (lambda i: (0, nc - 1 - i)) if rev else (lambda i: (0, i))
    rows = lambda c, cb=0: pl.BlockSpec((Lc, c), lambda i: (idx(i)[0], cb))
    c0, n0, m0 = state
    final = final_args is not None
    in_specs = [rows(heads * dqk), rows(heads * dqk), rows(heads * dv), rows(LANES),
                pl.BlockSpec((gt.shape[0], Lc), idx_t), _full(c0.shape), _full(n0.shape), _full(m0.shape)]
    args = [q, k, v, g, gt, c0, n0, m0]
    if final:
        hf, xc, xo, norm_g, skip = final_args
        inner = heads * dv
        in_specs += [rows(inner), rows(inner), rows(inner, 1), _full((1, inner)), _full((1, inner))]
        args += [hf, xc, xo, norm_g.reshape(1, inner), skip.reshape(1, inner)]
    out_dtype = BF16 if final else F32
    return pl.pallas_call(
        functools.partial(_scan_kernel, heads=heads, rev=rev, final=final),
        grid=(nc,),
        in_specs=in_specs,
        out_specs=[rows(heads * dv), _full(c0.shape), _full(n0.shape), _full(m0.shape)],
        out_shape=[jax.ShapeDtypeStruct((L, heads * dv), out_dtype), jax.ShapeDtypeStruct(c0.shape, F32),
                   jax.ShapeDtypeStruct(n0.shape, F32), jax.ShapeDtypeStruct(m0.shape, F32)],
        scratch_shapes=[pltpu.VMEM(c0.shape, F32), pltpu.VMEM(n0.shape, F32), pltpu.VMEM(m0.shape, F32)],
        compiler_params=_cparams("arbitrary"),
        name="mlstm_scan_bwd" if rev else "mlstm_scan_fwd",
    )(*args)


def _mlstm_stream(xo, mp, st_f, st_b):
    (conv_w, conv_b, q_w, k_w, v_w, gate_w, gate_b, norm_g, skip) = mp
    xc, q, k, v, g, gt = _mlstm_pre(xo, conv_w, conv_b, q_w, k_w, v_w, gate_w, gate_b)
    hf, *sf = _mlstm_scan(q, k, v, g, gt, st_f, rev=False)
    y, *sb = _mlstm_scan(q, k, v, g, gt, st_b, rev=True, final_args=(hf, xc, xo, norm_g, skip))
    return y, tuple(sf), tuple(sb)


def _sconv_kernel(x_ref, p_ref, n_ref, w_ref, b_ref, o_ref):
    _cg_put(o_ref, _conv3(x_ref[...].astype(F32), p_ref, n_ref, w_ref[...], b_ref[...]))


def _sconv(zp, w, b):
    L, C = zp.shape
    tm = min(1024, L)
    tc = C // 3
    hb, nb = tm // HALO, L // HALO
    return pl.pallas_call(
        _sconv_kernel,
        grid=(L // tm, C // tc),
        in_specs=[pl.BlockSpec((tm, tc), lambda i, j: (i, j)),
                  pl.BlockSpec((HALO, tc), lambda i, j: (jnp.maximum(i * hb - 1, 0), j)),
                  pl.BlockSpec((HALO, tc), lambda i, j: (jnp.minimum((i + 1) * hb, nb - 1), j)),
                  pl.BlockSpec((3, tc), lambda i, j: (0, j)), pl.BlockSpec((1, tc), lambda i, j: (0, j))],
        out_specs=pl.BlockSpec((tc // LANES, tm, LANES), lambda i, j: (j, i, 0)),
        out_shape=jax.ShapeDtypeStruct((C // LANES, L, LANES), F32),
        compiler_params=_cparams("parallel", "parallel"),
        name="hyena_sconv",
    )(zp, zp, zp, w, b.reshape(1, C))


def _filt_kernel(w1_ref, b1_ref, fr_ref, w2_ref, b2_ref, w3_ref, dec_ref, f_ref, ss_ref, *, L):
    i = pl.program_id(0)
    tm = f_ref.shape[1]
    emb = w1_ref.shape[0]
    pos = (lax.broadcasted_iota(jnp.int32, (tm, LANES), 0) + i * tm).astype(F32)
    lane = lax.broadcasted_iota(jnp.int32, (tm, LANES), 1)
    band = jnp.where(lane <= H_BANDS, lane, lane - H_BANDS).astype(F32)
    ang = (2.0 * math.pi / L) * pos * band
    quarter = jnp.where(lane <= H_BANDS, 0.5 * math.pi, 0.0)
    feat = jnp.where(lane == 0, pos / max(L - 1, 1), jnp.where(lane < emb, jnp.sin(ang + quarter), 0.0))
    fr = fr_ref[...]
    z = jnp.sin(fr * (jnp.dot(feat, w1_ref[...], precision=HI, preferred_element_type=F32) + b1_ref[...]))
    z = jnp.sin(fr * (jnp.dot(z, w2_ref[...], precision=HI, preferred_element_type=F32) + b2_ref[...]))
    z = jnp.dot(z.astype(BF16), w3_ref[...].astype(BF16), preferred_element_type=F32)
    tau = jnp.abs(pos[:, 0:1] - (L // 2)) / (L / 2)
    filt = z * jnp.exp(-tau * dec_ref[...])
    _cg_put(f_ref, filt)

    @pl.when(i == 0)
    def _():
        ss_ref[...] = jnp.zeros_like(ss_ref)

    ss_ref[...] += jnp.sum(filt * filt, axis=0, keepdims=True)


def _hyena_filters(L, w1, b1, freq, w2, b2, w3, decay):
    emb, fh = w1.shape
    C2 = w3.shape[1]
    tm = min(256, L)
    w1p = jnp.zeros((LANES, fh), F32).at[:emb].set(w1)
    return pl.pallas_call(
        functools.partial(_filt_kernel, L=L),
        grid=(L // tm,),
        in_specs=[_full((LANES, fh)), _full((1, fh)), _full((1, fh)), _full((fh, fh)), _full((1, fh)),
                  _full((fh, C2)), _full((1, C2))],
        out_specs=[pl.BlockSpec((C2 // LANES, tm, LANES), lambda i: (0, i, 0)), _full((1, C2))],
        out_shape=[jax.ShapeDtypeStruct((C2 // LANES, L, LANES), F32), jax.ShapeDtypeStruct((1, C2), F32)],
        compiler_params=_cparams("arbitrary"),
        name="hyena_filters",
    )(w1p, b1.reshape(1, fh), freq.reshape(1, fh), w2, b2.reshape(1, fh), w3, decay.reshape(1, C2))


def _dft_tables(L):
    N = 2 * L
    N2 = FFT_N2
    N1 = N // N2
    k1 = jnp.arange(N1, dtype=jnp.int32)
    n1 = jnp.arange(N1 // 2, dtype=jnp.int32)
    n2 = jnp.arange(N2, dtype=jnp.int32)
    a = ((k1[:, None] * n1[None, :]) % N1).astype(F32) * (2.0 * math.pi / N1)
    b = ((n2[:, None] * k1[None, :]) % N).astype(F32) * (2.0 * math.pi / N)
    cos_a, sin_a, cos_b, sin_b = jnp.cos(a)[None], jnp.sin(a)[None], jnp.cos(b)[:, :, None], jnp.sin(b)[:, :, None]
    ca = (cos_b * cos_a - sin_b * sin_a).astype(BF16)
    sa = (sin_b * cos_a + cos_b * sin_a).astype(BF16)
    cd, sd = jnp.swapaxes(ca, 1, 2), jnp.swapaxes(sa, 1, 2)
    ph2 = (n2[:, None] * n2[None, :]) % N2
    ang2 = ph2.astype(F32) * (2.0 * math.pi / N2)
    c2, s2 = jnp.cos(ang2), jnp.sin(ang2)
    fwd = jnp.block([[c2, s2], [-s2, c2]]).astype(BF16)
    inv = jnp.block([[c2, -s2], [s2, c2]]).astype(BF16)
    return (ca, sa), (cd, sd), (fwd, inv)


def _fft_a_kernel(x_ref, c_ref, s_ref, a_ref):
    for r in range(SUB):
        x = _cg_load(x_ref, r).astype(BF16)
        re = jnp.dot(c_ref[r], x, preferred_element_type=F32)
        im = -jnp.dot(s_ref[r], x, preferred_element_type=F32)
        _cg_store(a_ref, r, _pack(re, im))


def _fft_a(x, gc, gblock, nblocks, tabs):
    ca, sa = tabs
    N2, N1, H1 = ca.shape
    x4 = x.reshape(x.shape[0], H1, N2, LANES)
    tspec = pl.BlockSpec((SUB, N1, H1), lambda j, cb: (j, 0, 0))
    return pl.pallas_call(
        _fft_a_kernel,
        grid=(N2 // SUB, nblocks),
        in_specs=[pl.BlockSpec((gc, H1, SUB, LANES), lambda j, cb: (gblock + cb, 0, j, 0)), tspec, tspec],
        out_specs=pl.BlockSpec((gc, N1, SUB, LANES), lambda j, cb: (cb, 0, j, 0)),
        out_shape=jax.ShapeDtypeStruct((gc * nblocks, N1, N2, LANES), U32),
        compiler_params=_cparams("parallel", "parallel"),
        name="fft_stage_a",
    )(x4, ca, sa)


def _cdft(t_ref, xr, xi):
    n = xr.shape[0]
    out = jnp.dot(t_ref[...], jnp.concatenate([xr, xi], axis=0), preferred_element_type=F32)
    return out[:n], out[n:]


def _fft_h_kernel(a_ref, f_ref, sc_ref, h_ref):
    for t in range(FFT_KB):
        xr, xi = _cdft(f_ref, *_unpack(_cg_cat(a_ref.at[:, t])))
        k1 = pl.program_id(0) * FFT_KB + t
        sc = sc_ref[...] * jnp.where((k1 & 2) == 0, 1.0, -1.0)
        if t % 2 == 0:
            _cg_put(h_ref.at[:, t], _pack(xr * sc, xi * sc))
        else:
            _cg_put(h_ref.at[:, t], _pack(xi * (-sc), xr * sc))


def _fft_h(a, gc, tab2, scale):
    G, N1, N2, _ = a.shape
    blk = pl.BlockSpec((gc, FFT_KB, N2, LANES), lambda j, cb: (cb, j, 0, 0))
    return pl.pallas_call(
        _fft_h_kernel,
        grid=(N1 // FFT_KB, G // gc),
        in_specs=[blk, _full(tab2[0].shape), pl.BlockSpec((1, gc * LANES), lambda j, cb: (0, cb))],
        out_specs=blk,
        out_shape=jax.ShapeDtypeStruct(a.shape, U32),
        compiler_params=_cparams("parallel", "parallel"),
        name="fft_filter_spectrum",
    )(a, tab2[0], scale)


def _fft_b_kernel(a_ref, h_ref, f_ref, i_ref, p_ref):
    for t in range(FFT_KB):
        xr, xi = _cdft(f_ref, *_unpack(_cg_cat(a_ref.at[:, t])))
        hr, hi = _unpack(_cg_cat(h_ref.at[:, t]))
        hr, hi = hr.astype(F32), hi.astype(F32)
        yr = (xr * hr - xi * hi).astype(BF16)
        yi = (xr * hi + xi * hr).astype(BF16)
        _cg_put(p_ref.at[:, t], _pack(*_cdft(i_ref, yr, yi)))


def _fft_b(a, h, hblock, tab2):
    G, N1, N2, _ = a.shape
    blk = pl.BlockSpec((G, FFT_KB, N2, LANES), lambda j: (0, j, 0, 0))
    hblk = pl.BlockSpec((G, FFT_KB, N2, LANES), lambda j: (hblock, j, 0, 0))
    return pl.pallas_call(
        _fft_b_kernel,
        grid=(N1 // FFT_KB,),
        in_specs=[blk, hblk, _full(tab2[0].shape), _full(tab2[1].shape)],
        out_specs=blk,
        out_shape=jax.ShapeDtypeStruct(a.shape, U32),
        compiler_params=_cparams("parallel"),
        name="fft_stage_b",
    )(a, h, tab2[0], tab2[1])


def _fft_d_kernel(p_ref, c_ref, s_ref, g_ref, v_ref, fb_ref, o_ref):
    for r in range(SUB):
        pr, pi = _unpack(_cg_load(p_ref, r))
        y = (jnp.dot(c_ref[r], pr, preferred_element_type=F32)
             - jnp.dot(s_ref[r], pi, preferred_element_type=F32))
        v = _cg_load(v_ref, r)
        _cg_store(o_ref, r, _cg_load(g_ref, r) * (y + v * fb_ref[...]))


def _fft_d(p, tabs, gsrc, gblock, vsrc, vblock, fbias):
    cd, sd = tabs
    N2, H1, N1 = cd.shape
    G = p.shape[0]
    L = H1 * N2
    tspec = pl.BlockSpec((SUB, H1, N1), lambda j: (j, 0, 0))
    sig = lambda gb: pl.BlockSpec((G, H1, SUB, LANES), lambda j: (gb, 0, j, 0))
    out = pl.pallas_call(
        _fft_d_kernel,
        grid=(N2 // SUB,),
        in_specs=[pl.BlockSpec((G, N1, SUB, LANES), lambda j: (0, 0, j, 0)), tspec, tspec, sig(gblock), sig(vblock),
                  _full((1, G * LANES))],
        out_specs=sig(0),
        out_shape=jax.ShapeDtypeStruct((G, H1, N2, LANES), F32),
        compiler_params=_cparams("parallel"),
        name="fft_stage_d",
    )(p, cd, sd, gsrc.reshape(gsrc.shape[0], H1, N2, LANES), vsrc.reshape(vsrc.shape[0], H1, N2, LANES), fbias)
    return out.reshape(G, L, LANES)


def _hyena_long(z, filt, ssq, fbias, tables):
    L = z.shape[1]
    gc = z.shape[0] // 3
    C = gc * LANES
    N = 2 * L
    taba, tabd, tab2 = tables
    scale = lax.rsqrt(ssq + H_DECAY_EPS) * (1.0 / N)
    h = _fft_h(_fft_a(filt, gc, 0, 2, taba), gc, tab2, scale)
    fb = fbias.reshape(1, 2 * C)
    p = _fft_b(_fft_a(z, gc, 2, 1, taba), h, 0, tab2)
    y1 = _fft_d(p, tabd, z, 0, z, 2, fb[:, :C])
    p = _fft_b(_fft_a(y1, gc, 0, 1, taba), h, 1, tab2)
    return _fft_d(p, tabd, z, 1, y1, 0, fb[:, C:])


def _short_conv_kernel(z_ref, f_ref, ss_ref, fb_ref, cf_ref, sf_ref, cd_ref, sd_ref, o_ref):
    L = z_ref.shape[1]
    G = o_ref.shape[0]
    C = G * LANES
    N = 2 * L

    def dft(xb):
        return (jnp.dot(cf_ref[...], xb, preferred_element_type=F32),
                -jnp.dot(sf_ref[...], xb, preferred_element_type=F32))

    def conv(u, blk, gate):
        sl = slice(blk * C, (blk + 1) * C)
        ur, ui = dft(u.astype(BF16))
        fr, fi = dft(_cg_cat(f_ref.at[blk * G:(blk + 1) * G]).astype(BF16))
        q = lax.broadcasted_iota(jnp.int32, fr.shape, 0) & 3
        fr, fi = (jnp.where(q == 0, fr, jnp.where(q == 1, -fi, jnp.where(q == 2, -fr, fi))),
                  jnp.where(q == 0, fi, jnp.where(q == 1, fr, jnp.where(q == 2, -fi, -fr))))
        yr = (ur * fr - ui * fi).astype(BF16)
        yi = (ur * fi + ui * fr).astype(BF16)
        y = (jnp.dot(cd_ref[...], yr, preferred_element_type=F32)
             - jnp.dot(sd_ref[...], yi, preferred_element_type=F32))
        y = y * (lax.rsqrt(ss_ref[:, sl] + H_DECAY_EPS) * (1.0 / N))
        return gate * (y + u * fb_ref[:, sl])

    x1 = _cg_cat(z_ref.at[0:G])
    x2 = _cg_cat(z_ref.at[G:2 * G])
    v = _cg_cat(z_ref.at[2 * G:3 * G])
    _cg_put(o_ref, conv(conv(v, 0, x1), 1, x2))


def _hyena_short(z, filt, ssq, fbias):
    L = z.shape[1]
    G = z.shape[0] // 3
    C = G * LANES
    N = 2 * L
    k = jnp.arange(N, dtype=jnp.int32)
    n = jnp.arange(L, dtype=jnp.int32)
    ang = ((k[:, None] * n[None, :]) % N).astype(F32) * (2.0 * math.pi / N)
    cf, sf = jnp.cos(ang).astype(BF16), jnp.sin(ang).astype(BF16)
    args = (z, filt, ssq, fbias.reshape(1, 2 * C), cf, sf, cf.T, sf.T)
    return pl.pallas_call(
        _short_conv_kernel,
        grid=(1,),
        in_specs=[_full(a.shape) for a in args],
        out_specs=_full((G, L, LANES)),
        out_shape=jax.ShapeDtypeStruct((G, L, LANES), F32),
        compiler_params=_cparams("arbitrary"),
        name="hyena_short_conv",
    )(*args)


def _router_kernel(x_ref, g_ref, mod_ref, rw_ref, rb_ref, c0_ref, xt_ref, rt_ref, cnt_ref, carry):
    i = pl.program_id(0)

    @pl.when(i == 0)
    def _():
        carry[...] = c0_ref[...]

    u = _norm_mod(x_ref[...], g_ref[...], mod_ref[...], 3)
    _tt_store(xt_ref, u)
    logits = jnp.dot(u, rw_ref[...], precision=HI, preferred_element_type=F32) + rb_ref[...]
    tm = u.shape[0]
    lane = lax.broadcasted_iota(jnp.int32, logits.shape, 1).astype(F32)
    ninf = -jnp.inf

    def top(vals):
        mx = jnp.max(vals, axis=-1, keepdims=True)
        ix = jnp.min(jnp.where(vals == mx, lane, float(LANES)), axis=-1, keepdims=True)
        return mx, ix

    lg = jnp.where(lane < N_GROUPS, logits, ninf)
    gmax, gidx = top(lg)
    g_w = 1.0 / jnp.sum(jnp.exp(lg - gmax), axis=-1, keepdims=True)
    lo = N_GROUPS + EXP_PER_GROUP * gidx
    le = jnp.where((lane >= lo) & (lane < lo + EXP_PER_GROUP), logits, ninf)
    m1, i1 = top(le)
    m2, i2 = top(jnp.where(lane == i1, ninf, le))
    r = jnp.exp(m2 - m1)
    w1 = g_w / (1.0 + r)
    w2 = w1 * r
    e1 = i1 - N_GROUPS
    e2 = i2 - N_GROUPS
    oh1 = lane == e1
    oh2 = lane == e2
    both = jnp.where(oh1 | oh2, 1.0, 0.0)
    rr = lax.broadcasted_iota(jnp.int32, (tm, tm), 0)
    cc = lax.broadcasted_iota(jnp.int32, (tm, tm), 1)
    earlier = jnp.where(cc < rr, 1.0, 0.0).astype(BF16)
    before = jnp.dot(earlier, both.astype(BF16), preferred_element_type=F32) + carry[...]
    rank1 = jnp.sum(jnp.where(oh1, before, 0.0), axis=-1, keepdims=True)
    rank2 = jnp.sum(jnp.where(oh2, before, 0.0), axis=-1, keepdims=True)
    carry[...] += jnp.sum(both, axis=0, keepdims=True)
    cnt_ref[...] = carry[...]
    out = jnp.zeros(logits.shape, F32)
    for j, col in enumerate((e1, e2, w1, w2, rank1, rank2)):
        out = jnp.where(lane == j, col, out)
    rt_ref[...] = out


def _router(h, g, mod, rw, rb, counts0):
    L, D = h.shape
    tm = min(256, L)
    return pl.pallas_call(
        _router_kernel,
        grid=(L // tm,),
        in_specs=[pl.BlockSpec((tm, D), lambda i: (i, 0)), _full((1, D)), _full(mod.shape),
                  _full(rw.shape), _full(rb.shape), _full((1, LANES))],
        out_specs=[pl.BlockSpec((tm, D // LANES, LANES), lambda i: (i, 0, 0)),
                   pl.BlockSpec((tm, LANES), lambda i: (i, 0)), _full((1, LANES))],
        out_shape=[jax.ShapeDtypeStruct((L, D // LANES, LANES), F32), jax.ShapeDtypeStruct((L, LANES), F32),
                   jax.ShapeDtypeStruct((1, LANES), F32)],
        scratch_shapes=[pltpu.VMEM((1, LANES), F32)],
        compiler_params=_cparams("arbitrary"),
        name="moe_router",
    )(h, g.reshape(1, D), mod, rw, rb, counts0)


def _dispatch_kernel(dest_ref, x_ref, xb_in_ref, xb_ref, sem):
    del xb_in_ref
    tm = x_ref.shape[0]
    a0 = 2 * tm * pl.program_id(0)

    def issue(r, carry):
        for k in range(2):
            pltpu.make_async_copy(x_ref.at[r], xb_ref.at[dest_ref[a0 + 2 * r + k]], sem).start()
        return carry

    def wait(r, carry):
        pltpu.make_async_copy(x_ref.at[0], xb_ref.at[0], sem).wait()
        return carry

    lax.fori_loop(0, tm, issue, 0, unroll=DMA_UNROLL)
    lax.fori_loop(0, 2 * tm, wait, 0, unroll=DMA_UNROLL)


def _dispatch(xt, dest, R):
    T, G, _ = xt.shape
    tm = min(256, T)
    return pl.pallas_call(
        _dispatch_kernel,
        grid_spec=pltpu.PrefetchScalarGridSpec(
            num_scalar_prefetch=1,
            grid=(T // tm,),
            in_specs=[pl.BlockSpec((tm, G, LANES), lambda i, d: (i, 0, 0)), pl.BlockSpec(memory_space=pl.ANY)],
            out_specs=pl.BlockSpec(memory_space=pl.ANY),
            scratch_shapes=[pltpu.SemaphoreType.DMA(())]),
        out_shape=jax.ShapeDtypeStruct((R, G, LANES), F32),
        input_output_aliases={2: 0},
        compiler_params=_cparams("arbitrary"),
        name="moe_dispatch",
    )(dest, xt, jnp.zeros((R, G, LANES), F32))


def _expert_kernel(be_ref, nb_ref, x_ref, w13_ref, w2_ref, o_ref, w13b, w2b):
    b = pl.program_id(0)
    live = b < nb_ref[0]
    new_expert = (b == 0) | (be_ref[b] != be_ref[jnp.maximum(b - 1, 0)])

    @pl.when(jnp.logical_not(live))
    def _():
        o_ref[...] = jnp.zeros_like(o_ref)

    @pl.when(live & new_expert)
    def _():
        w13b[...] = w13_ref[0].astype(BF16)
        w2b[...] = w2_ref[0].astype(BF16)

    @pl.when(live)
    def _():
        de = w2b.shape[0]
        x = _tt_load(x_ref).astype(BF16)
        hcat = jnp.dot(x, w13b[...], preferred_element_type=F32)
        a = (_silu(hcat[:, :de]) * hcat[:, de:]).astype(BF16)
        _tt_store(o_ref, jnp.dot(a, w2b[...], preferred_element_type=F32))


def _experts(xb, blk_exp, n_used, w13, w2):
    R, G, _ = xb.shape
    D = G * LANES
    de = w2.shape[1]
    nb = R // MOE_BLOCK
    blk = pl.BlockSpec((MOE_BLOCK, G, LANES), lambda b, be, nu: (b, 0, 0))
    return pl.pallas_call(
        _expert_kernel,
        grid_spec=pltpu.PrefetchScalarGridSpec(
            num_scalar_prefetch=2,
            grid=(nb,),
            in_specs=[blk,
                      pl.BlockSpec((1, D, 2 * de), lambda b, be, nu: (be[b], 0, 0)),
                      pl.BlockSpec((1, de, D), lambda b, be, nu: (be[b], 0, 0))],
            out_specs=blk,
            scratch_shapes=[pltpu.VMEM((D, 2 * de), BF16), pltpu.VMEM((de, D), BF16)]),
        out_shape=jax.ShapeDtypeStruct(xb.shape, F32),
        compiler_params=_cparams("arbitrary"),
        name="moe_experts",
    )(blk_exp, n_used, xb, w13, w2)


def _combine_kernel(dest_ref, h_ref, rt_ref, mod_ref, yb_ref, o_ref, buf, sem, *, a_base):
    tm = h_ref.shape[0]
    a0 = a_base + 2 * tm * pl.program_id(0)

    def issue(r, carry):
        for k in range(2):
            pltpu.make_async_copy(yb_ref.at[dest_ref[a0 + 2 * r + k]], buf.at[k, r], sem).start()
        return carry

    def wait(r, carry):
        pltpu.make_async_copy(yb_ref.at[0], buf.at[0, 0], sem).wait()
        return carry

    lax.fori_loop(0, tm, issue, 0, unroll=DMA_UNROLL)
    lax.fori_loop(0, 2 * tm, wait, 0, unroll=DMA_UNROLL)
    rt = rt_ref[...]
    y = rt[:, 2:3] * _tt_load(buf.at[0]) + rt[:, 3:4] * _tt_load(buf.at[1])
    o_ref[...] = h_ref[...] + mod_ref[5:6, :] * y


def _combine(h, yb, dest, row0, route, mod):
    L, D = h.shape
    tm = min(256, L)
    return pl.pallas_call(
        functools.partial(_combine_kernel, a_base=2 * row0),
        grid_spec=pltpu.PrefetchScalarGridSpec(
            num_scalar_prefetch=1,
            grid=(L // tm,),
            in_specs=[pl.BlockSpec((tm, D), lambda i, d: (i, 0)), pl.BlockSpec((tm, LANES), lambda i, d: (i, 0)),
                      pl.BlockSpec(mod.shape, lambda i, d: (0, 0)), pl.BlockSpec(memory_space=pl.ANY)],
            out_specs=pl.BlockSpec((tm, D), lambda i, d: (i, 0)),
            scratch_shapes=[pltpu.VMEM((2, tm, D // LANES, LANES), F32), pltpu.SemaphoreType.DMA(())]),
        out_shape=jax.ShapeDtypeStruct((L, D), F32),
        compiler_params=_cparams("arbitrary"),
        name="moe_combine",
    )(dest, h, route, mod, yb)


def _moe(streams, g, rw, rb, w13, w2, layer):
    counts = jnp.zeros((1, LANES), F32)
    xts, routes = [], []
    for h, mod in streams:
        xt, rt, counts = _router(h, g, mod, rw, rb, counts)
        xts.append(xt)
        routes.append(rt)
    xt = jnp.concatenate(xts, axis=0) if len(xts) > 1 else xts[0]
    route = jnp.concatenate(routes, axis=0) if len(routes) > 1 else routes[0]
    T = xt.shape[0]
    A = 2 * T
    cnt = counts[0, :N_EXPERTS].astype(jnp.int32)
    padded = (cnt + MOE_BLOCK - 1) // MOE_BLOCK * MOE_BLOCK
    p_ends = jnp.cumsum(padded)
    p_starts = p_ends - padded
    experts = route[:, 0:2].astype(jnp.int32)
    onehot = experts[:, :, None] == jnp.arange(N_EXPERTS, dtype=jnp.int32)
    dest = (jnp.sum(jnp.where(onehot, p_starts, 0), axis=-1) + route[:, 4:6].astype(jnp.int32)).reshape(A)
    n_blocks = -(-A // MOE_BLOCK) + N_EXPERTS
    blk_start = jnp.arange(n_blocks, dtype=jnp.int32) * MOE_BLOCK
    blk_exp = jnp.minimum(jnp.sum(p_ends[None, :] <= blk_start[:, None], axis=1), N_EXPERTS - 1).astype(jnp.int32)
    blk_exp = blk_exp + layer * N_EXPERTS
    n_used = (p_ends[-1:] // MOE_BLOCK).astype(jnp.int32)
    yb = _experts(_dispatch(xt, dest, n_blocks * MOE_BLOCK), blk_exp, n_used, w13, w2)
    outs, row0 = [], 0
    for (h, mod), rt in zip(streams, routes):
        outs.append(_combine(h, yb, dest, row0, rt, mod))
        row0 += h.shape[0]
    return outs


def _final_kernel(x_ref, g_ref, o_ref):
    x = x_ref[...]
    o_ref[...] = x * lax.rsqrt(jnp.mean(x * x, axis=-1, keepdims=True) + EPS) * g_ref[...]


def _final_norm(h, g):
    L, D = h.shape
    tm = min(512, L)
    return pl.pallas_call(
        _final_kernel,
        grid=(L // tm,),
        in_specs=[pl.BlockSpec((tm, D), lambda i: (i, 0)), _full((1, D))],
        out_specs=pl.BlockSpec((tm, D), lambda i: (i, 0)),
        out_shape=jax.ShapeDtypeStruct((L, D), F32),
        compiler_params=_cparams("parallel"),
        name="final_norm",
    )(h, g.reshape(1, D))


def _zero_state(heads, dqk, dv):
    return (jnp.zeros((heads, dqk, dv), F32), jnp.zeros((heads, 1, dqk), F32), jnp.zeros((heads, 1, LANES), F32))


def kernel(x, c, ctx, c_ctx, ada_w, ada_b, norm1_g, norm2_g, rg_w, rg_b, re_w, re_b, moe_w13, moe_w2, m_in_w, m_conv_w, m_conv_b, m_q_w, m_k_w, m_v_w, m_gate_w, m_gate_b, m_norm_g, m_skip, m_out_w, h_in_w, h_in_b, h_sc_w, h_sc_b, h_f_w1, h_f_b1, h_f_freq, h_f_w2, h_f_b2, h_f_w3, h_decay, h_fbias, h_out_w, h_out_b, final_g):
    B, S, D = x.shape
    assert B == 1, "kernel is written for the single-sequence problem shape"
    depth = ada_w.shape[0]
    hl, hc = x[0], ctx[0]
    cc = jnp.zeros((8, D), F32).at[0].set(c[0]).at[1].set(c_ctx)
    mods = _ada(cc, ada_w, ada_b).reshape(depth, 8, 6, D)
    pad8 = lambda m: jnp.concatenate([m, jnp.zeros((2, D), F32)], axis=0)
    nrt = N_GROUPS + N_EXPERTS
    zeros_d = jnp.zeros((D,), F32)
    tables = _dft_tables(S) if depth > 1 else None
    w13_all = moe_w13.reshape((-1,) + moe_w13.shape[2:])
    w2_all = moe_w2.reshape((-1,) + moe_w2.shape[2:])

    for i in range(depth):
        kind, slot = i % N_MIXERS, i // N_MIXERS
        col_major = (slot % 2) == 1
        ctx_live = any(j % N_MIXERS == 0 for j in range(i + 1, depth))
        ml, mc = pad8(mods[i, 0]), pad8(mods[i, 1])

        if kind == 0:
            w_in = m_in_w[slot].astype(BF16)
            w_out = m_out_w[slot].astype(BF16)
            inner = w_out.shape[0]
            mp = (m_conv_w[slot], m_conv_b[slot], m_q_w[slot], m_k_w[slot], m_v_w[slot], m_gate_w[slot],
                  m_gate_b[slot], m_norm_g[slot], m_skip[slot])
            heads, _, dqk = m_q_w[slot].shape
            dv = m_v_w[slot].shape[2]
            zb = jnp.zeros((w_in.shape[1],), F32)
            xo_c = _norm_proj(hc, norm1_g[i], mc, w_in, zb, False)
            yc, st_f, st_b = _mlstm_stream(xo_c, mp, _zero_state(heads, dqk, dv), _zero_state(heads, dqk, dv))
            xo_l = _norm_proj(hl, norm1_g[i], ml, w_in, zb, col_major)
            yl, _, _ = _mlstm_stream(xo_l, mp, st_f, st_b)
            hl = _out_proj(yl, w_out, zeros_d, hl, ml, 2, col_major)
            if ctx_live:
                hc = _out_proj(yc, w_out, zeros_d, hc, mc, 2, False)
        else:
            w_in = h_in_w[slot].astype(BF16)
            w_out = h_out_w[slot].astype(BF16)
            fp = (h_f_w1[slot], h_f_b1[slot], h_f_freq[slot], h_f_w2[slot], h_f_b2[slot], h_f_w3[slot], h_decay[slot])
            zl = _sconv(_norm_proj(hl, norm1_g[i], ml, w_in, h_in_b[slot], col_major), h_sc_w[slot], h_sc_b[slot])
            filt, ssq = _hyena_filters(S, *fp)
            yl = _hyena_long(zl, filt, ssq, h_fbias[slot], tables)
            hl = _out_proj(yl, w_out, h_out_b[slot], hl, ml, 2, col_major)
            if ctx_live:
                zc = _sconv(_norm_proj(hc, norm1_g[i], mc, w_in, h_in_b[slot], False), h_sc_w[slot], h_sc_b[slot])
                filt_c, ssq_c = _hyena_filters(hc.shape[0], *fp)
                yc = _hyena_short(zc, filt_c, ssq_c, h_fbias[slot])
                hc = _out_proj(yc, w_out, h_out_b[slot], hc, mc, 2, False)

        rw = jnp.zeros((D, LANES), F32).at[:, :N_GROUPS].set(rg_w[i]).at[:, N_GROUPS:nrt].set(re_w[i])
        rb = jnp.zeros((1, LANES), F32).at[0, :N_GROUPS].set(rg_b[i]).at[0, N_GROUPS:nrt].set(re_b[i])
        if ctx_live:
            hc, hl = _moe([(hc, mc), (hl, ml)], norm2_g[i], rw, rb, w13_all, w2_all, i)
        else:
            (hl,) = _moe([(hl, ml)], norm2_g[i], rw, rb, w13_all, w2_all, i)

    return _final_norm(hl, final_g)[None]
```

```python
import functools
import math

import jax
import jax.numpy as jnp
from jax import lax
from jax.experimental import pallas as pl
from jax.experimental.pallas import tpu as pltpu

F32 = jnp.float32
BF16 = jnp.bfloat16
U32 = jnp.uint32
HI = lax.Precision.HIGHEST

EPS = 1e-6
GRID_W = 64
N_MIXERS = 2
M_HEADS = 4
H_BANDS = 16
N_GROUPS = 4
EXP_PER_GROUP = 8
N_EXPERTS = N_GROUPS * EXP_PER_GROUP
MOE_BLOCK = 256
DMA_UNROLL = 8
LANES = 128
HALO = 16
SUB = 8
FFT_N2 = 128
FFT_KB = 4
H_DECAY_EPS = EPS
VMEM_LIMIT = 56 * 1024 * 1024


def _cparams(*sem):
    return pltpu.CompilerParams(dimension_semantics=sem, vmem_limit_bytes=VMEM_LIMIT)


def _full(shape):
    n = len(shape)
    return pl.BlockSpec(shape, lambda *_: (0,) * n)


def _raster(L, tr_pref=128):
    rows = L // GRID_W
    return rows, min(tr_pref, rows)


def _cg_cat(ref):
    return jnp.concatenate([ref[s] for s in range(ref.shape[0])], axis=1)


def _cg_put(ref, val):
    for s in range(ref.shape[0]):
        ref[s] = val[:, s * LANES:(s + 1) * LANES]


def _cg_load(ref4, r):
    G, n = ref4.shape[0], ref4.shape[1]
    flat = ref4.reshape(G, n * SUB, LANES)
    return jnp.concatenate([flat[s, pl.ds(r, n, stride=SUB), :] for s in range(G)], axis=1)


def _cg_store(ref4, r, val):
    G, n = ref4.shape[0], ref4.shape[1]
    flat = ref4.reshape(G, n * SUB, LANES)
    for s in range(G):
        flat[s, pl.ds(r, n, stride=SUB), :] = val[:, s * LANES:(s + 1) * LANES]


def _tt_load(ref3):
    n, G = ref3.shape[0], ref3.shape[1]
    flat = ref3.reshape(n * G, LANES)
    return jnp.concatenate([flat[pl.ds(s, n, stride=G), :] for s in range(G)], axis=1)


def _tt_store(ref3, val):
    n, G = ref3.shape[0], ref3.shape[1]
    flat = ref3.reshape(n * G, LANES)
    for s in range(G):
        flat[pl.ds(s, n, stride=G), :] = val[:, s * LANES:(s + 1) * LANES]


def _pack(re, im):
    hi = pltpu.bitcast(re.astype(BF16).astype(F32), U32)
    lo = pltpu.bitcast(im.astype(BF16).astype(F32), U32)
    return (hi & jnp.uint32(0xFFFF0000)) | (lo >> 16)


def _unpack(w):
    re = pltpu.bitcast(w & jnp.uint32(0xFFFF0000), F32)
    im = pltpu.bitcast(w << 16, F32)
    return re.astype(BF16), im.astype(BF16)


def _sigmoid(x):
    return 1.0 / (1.0 + jnp.exp(-x))


def _silu(x):
    return x * _sigmoid(x)


def _norm_mod(x, g, mod, k):
    ms = jnp.mean(x * x, axis=-1, keepdims=True)
    y = x * lax.rsqrt(ms + EPS) * g
    return y * (1.0 + mod[k + 1:k + 2, :]) + mod[k:k + 1, :]


def _ada_kernel(c_ref, w_ref, b_ref, o_ref):
    s = _silu(c_ref[...])
    o_ref[0] = jnp.dot(s, w_ref[0], precision=HI, preferred_element_type=F32) + b_ref[0]


def _ada(cc, ada_w, ada_b):
    depth, D, N = ada_w.shape
    tn = N // 6
    return pl.pallas_call(
        _ada_kernel,
        grid=(depth, N // tn),
        in_specs=[pl.BlockSpec((8, D), lambda l, j: (0, 0)),
                  pl.BlockSpec((1, D, tn), lambda l, j: (l, 0, j)),
                  pl.BlockSpec((1, 1, tn), lambda l, j: (l, 0, j))],
        out_specs=pl.BlockSpec((1, 8, tn), lambda l, j: (l, 0, j)),
        out_shape=jax.ShapeDtypeStruct((depth, 8, N), F32),
        compiler_params=_cparams("parallel", "parallel"),
        name="ada",
    )(cc, ada_w, ada_b.reshape(depth, 1, N))


def _raster_col(refs, c):
    tr = refs[0].shape[0]
    return jnp.concatenate([r.reshape(tr * SUB, LANES)[pl.ds(c, tr, stride=SUB), :] for r in refs], axis=1)


def _lane_blocks(tr, D, index):
    return [pl.BlockSpec((tr, SUB, LANES), functools.partial(index, s)) for s in range(D // LANES)]


def _norm_proj_kernel(*refs, nchunk, col_major):
    *x_refs, g_ref, mod_ref, w_ref, b_ref, o_ref = refs
    N = o_ref.shape[-1]
    step = N // nchunk
    for c in range(SUB if col_major else 1):
        x = _raster_col(x_refs, c) if col_major else x_refs[0][...]
        u = _norm_mod(x, g_ref[...], mod_ref[...], 0).astype(BF16)
        for j in range(nchunk):
            sl = slice(j * step, (j + 1) * step)
            acc = (jnp.dot(u, w_ref[:, sl], preferred_element_type=F32) + b_ref[:, sl]).astype(o_ref.dtype)
            if col_major:
                o_ref[c, :, sl] = acc
            else:
                o_ref[:, sl] = acc


def _norm_proj(x, g, mod, w, b, col_major):
    L, D = x.shape
    N = w.shape[1]
    consts = [_full((1, D)), _full(mod.shape), _full((D, N)), _full((1, N))]
    kern = functools.partial(_norm_proj_kernel, nchunk=max(1, N // 1024), col_major=col_major)
    args = (g.reshape(1, D), mod, w, b.reshape(1, N))
    if col_major:
        rows, tr = _raster(L)
        out = pl.pallas_call(
            kern,
            grid=(GRID_W // SUB, rows // tr),
            in_specs=_lane_blocks(tr, D, lambda s, iw, ir: (ir, iw, s)) + consts,
            out_specs=pl.BlockSpec((SUB, tr, N), lambda iw, ir: (iw, ir, 0)),
            out_shape=jax.ShapeDtypeStruct((GRID_W, rows, N), BF16),
            compiler_params=_cparams("parallel", "parallel"),
            name="norm_proj_colmajor",
        )(*([x.reshape(rows, GRID_W, D)] * (D // LANES)), *args)
        return out.reshape(L, N)
    tm = min(256, L)
    return pl.pallas_call(
        kern,
        grid=(L // tm,),
        in_specs=[pl.BlockSpec((tm, D), lambda i: (i, 0))] + consts,
        out_specs=pl.BlockSpec((tm, N), lambda i: (i, 0)),
        out_shape=jax.ShapeDtypeStruct((L, N), BF16),
        compiler_params=_cparams("parallel"),
        name="norm_proj",
    )(x, *args)


def _out_proj_kernel(y_ref, w_ref, b_ref, *refs, gate_row, col_major, y_cg):
    *h_refs, mod_ref, o_ref = refs
    gate = mod_ref[gate_row:gate_row + 1, :]
    for c in range(SUB if col_major else 1):
        if y_cg:
            y = jnp.concatenate([y_ref[s, c] if col_major else y_ref[s] for s in range(y_ref.shape[0])],
                                axis=1).astype(BF16)
        else:
            y = y_ref[c] if col_major else y_ref[...]
        acc = jnp.dot(y, w_ref[...], preferred_element_type=F32) + b_ref[...]
        if col_major:
            o_ref[:, c, :] = _raster_col(h_refs, c) + gate * acc
        else:
            o_ref[...] = h_refs[0][...] + gate * acc


def _out_proj(y, w, b, h, mod, gate_row, col_major):
    L, D = h.shape
    K = w.shape[0]
    y_cg = y.ndim == 3
    G = K // LANES
    kern = functools.partial(_out_proj_kernel, gate_row=gate_row, col_major=col_major, y_cg=y_cg)
    if col_major:
        rows, tr = _raster(L)
        hspec = pl.BlockSpec((tr, SUB, D), lambda iw, ir: (ir, iw, 0))
        if y_cg:
            yspec = pl.BlockSpec((G, SUB, tr, LANES), lambda iw, ir: (0, iw, ir, 0))
            yv = y.reshape(G, GRID_W, rows, LANES)
        else:
            yspec = pl.BlockSpec((SUB, tr, K), lambda iw, ir: (iw, ir, 0))
            yv = y.reshape(GRID_W, rows, K)
        out = pl.pallas_call(
            kern,
            grid=(GRID_W // SUB, rows // tr),
            in_specs=[yspec, _full((K, D)), _full((1, D))] + _lane_blocks(tr, D, lambda s, iw, ir: (ir, iw, s))
            + [_full(mod.shape)],
            out_specs=hspec,
            out_shape=jax.ShapeDtypeStruct((rows, GRID_W, D), F32),
            compiler_params=_cparams("parallel", "parallel"),
            name="out_proj_colmajor",
        )(yv, w, b.reshape(1, D), *([h.reshape(rows, GRID_W, D)] * (D // LANES)), mod)
        return out.reshape(L, D)
    tm = min(256, L)
    rows_spec = lambda c: pl.BlockSpec((tm, c), lambda i: (i, 0))
    yspec = pl.BlockSpec((G, tm, LANES), lambda i: (0, i, 0)) if y_cg else rows_spec(K)
    return pl.pallas_call(
        kern,
        grid=(L // tm,),
        in_specs=[yspec, _full((K, D)), _full((1, D)), rows_spec(D), _full(mod.shape)],
        out_specs=rows_spec(D),
        out_shape=jax.ShapeDtypeStruct((L, D), F32),
        compiler_params=_cparams("parallel"),
        name="out_proj",
    )(y, w, b.reshape(1, D), h, mod)


def _halo_specs(tm, L, ncol, cblock):
    hb = tm // HALO
    nb = L // HALO
    prev = pl.BlockSpec((HALO, ncol), lambda i: (jnp.maximum(i * hb - 1, 0), cblock))
    nxt = pl.BlockSpec((HALO, ncol), lambda i: (jnp.minimum((i + 1) * hb, nb - 1), cblock))
    return prev, nxt


def _conv3(x, prev_ref, next_ref, w, b):
    tm = x.shape[0]
    i = pl.program_id(0)
    first = i == 0
    last = i == pl.num_programs(0) - 1
    p = jnp.where(first, 0.0, prev_ref[HALO - 1:HALO, :].astype(F32))
    n = jnp.where(last, 0.0, next_ref[0:1, :].astype(F32))
    row = lax.broadcasted_iota(jnp.int32, x.shape, 0)
    xm1 = jnp.where(row == 0, p, pltpu.roll(x, 1, 0))
    xp1 = jnp.where(row == tm - 1, n, pltpu.roll(x, tm - 1, 0))
    return b + xm1 * w[0:1, :] + x * w[1:2, :] + xp1 * w[2:3, :]


def _mlstm_pre_kernel(x_ref, p_ref, n_ref, cw_ref, cb_ref, qw_ref, kw_ref, vw_ref, gw_ref, gb_ref,
                      xc_ref, q_ref, k_ref, v_ref, g_ref, gt_ref, *, heads, kscale):
    xm = x_ref[...]
    xc = _silu(_conv3(xm.astype(F32), p_ref, n_ref, cw_ref[...], cb_ref[...]))
    xcb = xc.astype(BF16)
    xc_ref[...] = xcb
    inner = xm.shape[1]
    dh = inner // heads
    dqk = qw_ref.shape[2]
    dv = vw_ref.shape[2]
    for h in range(heads):
        xs = xcb[:, h * dh:(h + 1) * dh]
        q_ref[:, h * dqk:(h + 1) * dqk] = jnp.dot(xs, qw_ref[h], preferred_element_type=F32).astype(BF16)
        kh = jnp.dot(xs, kw_ref[h], preferred_element_type=F32) * kscale
        k_ref[:, h * dqk:(h + 1) * dqk] = kh.astype(BF16)
        v_ref[:, h * dv:(h + 1) * dv] = jnp.dot(xm[:, h * dh:(h + 1) * dh], vw_ref[h],
                                                 preferred_element_type=F32).astype(BF16)
    g = (jnp.dot(xcb, gw_ref[:inner, :], preferred_element_type=F32)
         + jnp.dot(xm, gw_ref[inner:, :], preferred_element_type=F32) + gb_ref[...])
    lane = lax.broadcasted_iota(jnp.int32, g.shape, 1)
    is_forget = ((lane >= heads) & (lane < 2 * heads)) | ((lane >= 3 * heads) & (lane < 4 * heads))
    logsig = jnp.minimum(g, 0.0) - jnp.log(1.0 + jnp.exp(-jnp.abs(g)))
    g = jnp.where(is_forget, logsig, g)
    g_ref[...] = g
    gt_ref[...] = g.T[:gt_ref.shape[0], :]


def _mlstm_pre(xo, conv_w, conv_b, q_w, k_w, v_w, gate_w, gate_b):
    L = xo.shape[0]
    heads, dh, dqk = q_w.shape
    dv = v_w.shape[2]
    inner = heads * dh
    tm = min(256, L)
    ng = 4 * heads
    gw = jnp.zeros((2 * inner, LANES), BF16).at[:, :ng].set(gate_w.astype(BF16))
    gb = jnp.zeros((1, LANES), F32).at[0, :ng].set(gate_b)
    prev, nxt = _halo_specs(tm, L, inner, 0)
    rows = lambda c: pl.BlockSpec((tm, c), lambda i: (i, 0))
    return pl.pallas_call(
        functools.partial(_mlstm_pre_kernel, heads=heads, kscale=dqk ** -0.5),
        grid=(L // tm,),
        in_specs=[rows(inner), prev, nxt, _full((3, inner)), _full((1, inner)),
                  _full(q_w.shape), _full(k_w.shape), _full(v_w.shape), _full(gw.shape), _full(gb.shape)],
        out_specs=[rows(inner), rows(heads * dqk), rows(heads * dqk), rows(heads * dv), rows(LANES),
                   pl.BlockSpec((ng, tm), lambda i: (0, i))],
        out_shape=[jax.ShapeDtypeStruct((L, inner), BF16), jax.ShapeDtypeStruct((L, heads * dqk), BF16),
                   jax.ShapeDtypeStruct((L, heads * dqk), BF16), jax.ShapeDtypeStruct((L, heads * dv), BF16),
                   jax.ShapeDtypeStruct((L, LANES), F32), jax.ShapeDtypeStruct((ng, L), F32)],
        compiler_params=_cparams("parallel"),
        name="mlstm_pre",
    )(xo, xo, xo, conv_w, conv_b.reshape(1, inner), q_w.astype(BF16), k_w.astype(BF16), v_w.astype(BF16), gw, gb)


def _scan_kernel(*refs, heads, rev, final):
    if final:
        (q_ref, k_ref, v_ref, g_ref, gt_ref, c0_ref, n0_ref, m0_ref, hf_ref, xc_ref, o_ref, ng_ref, sk_ref,
         out_ref, cT_ref, nT_ref, mT_ref, C_sc, n_sc, m_sc) = refs
    else:
        (q_ref, k_ref, v_ref, g_ref, gt_ref, c0_ref, n0_ref, m0_ref,
         out_ref, cT_ref, nT_ref, mT_ref, C_sc, n_sc, m_sc) = refs
    ci = pl.program_id(0)

    @pl.when(ci == 0)
    def _():
        C_sc[...] = c0_ref[...]
        n_sc[...] = n0_ref[...]
        m_sc[...] = m0_ref[...]

    Lc = q_ref.shape[0]
    dqk = q_ref.shape[1] // heads
    dv = v_ref.shape[1] // heads
    G = g_ref[...]
    GT = gt_ref[...]
    r = lax.broadcasted_iota(jnp.int32, (Lc, Lc), 0)
    c = lax.broadcasted_iota(jnp.int32, (Lc, Lc), 1)
    mask = (c >= r) if rev else (c <= r)
    tri = jnp.where(mask, 1.0, 0.0)
    Bc = jnp.dot(tri, G, precision=HI, preferred_element_type=F32)
    BT = lax.dot_general(GT, tri, (((1,), (1,)), ((), ())), precision=HI, preferred_element_type=F32)
    tot = Bc[0:1, :] if rev else Bc[Lc - 1:Lc, :]
    base = 2 * heads if rev else 0
    for h in range(heads):
        ci_, cf_ = base + h, base + heads + h
        b_col = Bc[:, cf_:cf_ + 1]
        i_col = G[:, ci_:ci_ + 1]
        bT = BT[cf_:cf_ + 1, :]
        iT = GT[ci_:ci_ + 1, :]
        b_last = tot[:, cf_:cf_ + 1]
        m = m_sc[h][:, 0:1]
        logd = jnp.where(mask, b_col - bT + iT, -jnp.inf)
        inter = b_col + m
        mj = jnp.maximum(inter, jnp.max(logd, axis=-1, keepdims=True))
        dmat = jnp.exp(logd - mj)
        qh = q_ref[:, h * dqk:(h + 1) * dqk]
        kh = k_ref[:, h * dqk:(h + 1) * dqk]
        vh = v_ref[:, h * dv:(h + 1) * dv]
        s = lax.dot_general(qh, kh, (((1,), (1,)), ((), ())), preferred_element_type=F32) * dmat
        w_inter = jnp.exp(inter - mj)
        Ct = C_sc[h]
        nrow = n_sc[h]
        num = (jnp.dot(s.astype(BF16), vh, preferred_element_type=F32)
               + w_inter * jnp.dot(qh, Ct.astype(BF16), preferred_element_type=F32))
        den = (jnp.sum(s, axis=-1, keepdims=True)
               + w_inter * jnp.sum(qh.astype(F32) * nrow, axis=-1, keepdims=True))
        hout = num / jnp.maximum(jnp.abs(den), jnp.exp(-mj))
        g_col = b_last - b_col + i_col
        g_row = b_last - bT + iT
        m_new = jnp.maximum(b_last + m, jnp.max(g_row, axis=-1, keepdims=True))
        decay = jnp.exp(b_last + m - m_new)
        wg = jnp.exp(g_col - m_new)
        vw = (vh.astype(F32) * wg).astype(BF16)
        C_sc[h] = decay * Ct + lax.dot_general(kh, vw, (((0,), (0,)), ((), ())), preferred_element_type=F32)
        n_sc[h] = decay * nrow + jnp.sum(kh.astype(F32) * wg, axis=0, keepdims=True)
        m_sc[h] = jnp.broadcast_to(m_new, m_sc.shape[1:])
        cs = slice(h * dv, (h + 1) * dv)
        if final:
            hs = hf_ref[:, cs] + hout
            mu = jnp.mean(hs, axis=-1, keepdims=True)
            d = hs - mu
            var = jnp.mean(d * d, axis=-1, keepdims=True)
            hn = d * lax.rsqrt(var + EPS)
            y = (hn * ng_ref[:, cs] + sk_ref[:, cs] * xc_ref[:, cs].astype(F32)) * _sigmoid(o_ref[:, cs].astype(F32))
            out_ref[:, cs] = y.astype(out_ref.dtype)
        else:
            out_ref[:, cs] = hout

    @pl.when(ci == pl.num_programs(0) - 1)
    def _():
        cT_ref[...] = C_sc[...]
        nT_ref[...] = n_sc[...]
        mT_ref[...] = m_sc[...]


def _mlstm_scan(q, k, v, g, gt, state, rev, final_args=None):
    L = q.shape[0]
    heads = M_HEADS
    dqk = q.shape[1] // heads
    dv = v.shape[1] // heads
    Lc = min(256, L)
    nc = L // Lc
    idx = (lambda i: (nc - 1 - i, 0)) if rev else (lambda i: (i, 0))
    idx_t = (lambda i: (0, nc - 1 - i)) if rev else (lambda i: (0, i))
    rows = lambda c, cb=0: pl.BlockSpec((Lc, c), lambda i: (idx(i)[0], cb))
    c0, n0, m0 = state
    final = final_args is not None
    in_specs = [rows(heads * dqk), rows(heads * dqk), rows(heads * dv), rows(LANES),
                pl.BlockSpec((gt.shape[0], Lc), idx_t), _full(c0.shape), _full(n0.shape), _full(m0.shape)]
    args = [q, k, v, g, gt, c0, n0, m0]
    if final:
        hf, xc, xo, norm_g, skip = final_args
        inner = heads * dv
        in_specs += [rows(inner), rows(inner), rows(inner, 1), _full((1, inner)), _full((1, inner))]
        args += [hf, xc, xo, norm_g.reshape(1, inner), skip.reshape(1, inner)]
    out_dtype = BF16 if final else F32
    return pl.pallas_call(
        functools.partial(_scan_kernel, heads=heads, rev=rev, final=final),
        grid=(nc,),
        in_specs=in_specs,
        out_specs=[rows(heads * dv), _full(c0.shape), _full(n0.shape), _full(m0.shape)],
        out_shape=[jax.ShapeDtypeStruct((L, heads * dv), out_dtype), jax.ShapeDtypeStruct(c0.shape, F32),
                   jax.ShapeDtypeStruct(n0.shape, F32), jax.ShapeDtypeStruct(m0.shape, F32)],
        scratch_shapes=[pltpu.VMEM(c0.shape, F32), pltpu.VMEM(n0.shape, F32), pltpu.VMEM(m0.shape, F32)],
        compiler_params=_cparams("arbitrary"),
        name="mlstm_scan_bwd" if rev else "mlstm_scan_fwd",
    )(*args)


def _mlstm_stream(xo, mp, st_f, st_b):
    (conv_w, conv_b, q_w, k_w, v_w, gate_w, gate_b, norm_g, skip) = mp
    xc, q, k, v, g, gt = _mlstm_pre(xo, conv_w, conv_b, q_w, k_w, v_w, gate_w, gate_b)
    hf, *sf = _mlstm_scan(q, k, v, g, gt, st_f, rev=False)
    y, *sb = _mlstm_scan(q, k, v, g, gt, st_b, rev=True, final_args=(hf, xc, xo, norm_g, skip))
    return y, tuple(sf), tuple(sb)


def _sconv_kernel(x_ref, p_ref, n_ref, w_ref, b_ref, o_ref):
    _cg_put(o_ref, _conv3(x_ref[...].astype(F32), p_ref, n_ref, w_ref[...], b_ref[...]))


def _sconv(zp, w, b):
    L, C = zp.shape
    tm = min(1024, L)
    tc = C // 3
    hb, nb = tm // HALO, L // HALO
    return pl.pallas_call(
        _sconv_kernel,
        grid=(L // tm, C // tc),
        in_specs=[pl.BlockSpec((tm, tc), lambda i, j: (i, j)),
                  pl.BlockSpec((HALO, tc), lambda i, j: (jnp.maximum(i * hb - 1, 0), j)),
                  pl.BlockSpec((HALO, tc), lambda i, j: (jnp.minimum((i + 1) * hb, nb - 1), j)),
                  pl.BlockSpec((3, tc), lambda i, j: (0, j)), pl.BlockSpec((1, tc), lambda i, j: (0, j))],
        out_specs=pl.BlockSpec((tc // LANES, tm, LANES), lambda i, j: (j, i, 0)),
        out_shape=jax.ShapeDtypeStruct((C // LANES, L, LANES), F32),
        compiler_params=_cparams("parallel", "parallel"),
        name="hyena_sconv",
    )(zp, zp, zp, w, b.reshape(1, C))


def _filt_kernel(w1_ref, b1_ref, fr_ref, w2_ref, b2_ref, w3_ref, dec_ref, f_ref, ss_ref, *, L):
    i = pl.program_id(0)
    tm = f_ref.shape[1]
    emb = w1_ref.shape[0]
    pos = (lax.broadcasted_iota(jnp.int32, (tm, LANES), 0) + i * tm).astype(F32)
    lane = lax.broadcasted_iota(jnp.int32, (tm, LANES), 1)
    band = jnp.where(lane <= H_BANDS, lane, lane - H_BANDS).astype(F32)
    ang = (2.0 * math.pi / L) * pos * band
    quarter = jnp.where(lane <= H_BANDS, 0.5 * math.pi, 0.0)
    feat = jnp.where(lane == 0, pos / max(L - 1, 1), jnp.where(lane < emb, jnp.sin(ang + quarter), 0.0))
    fr = fr_ref[...]
    z = jnp.sin(fr * (jnp.dot(feat, w1_ref[...], precision=HI, preferred_element_type=F32) + b1_ref[...]))
    z = jnp.sin(fr * (jnp.dot(z, w2_ref[...], precision=HI, preferred_element_type=F32) + b2_ref[...]))
    z = jnp.dot(z.astype(BF16), w3_ref[...].astype(BF16), preferred_element_type=F32)
    tau = jnp.abs(pos[:, 0:1] - (L // 2)) / (L / 2)
    filt = z * jnp.exp(-tau * dec_ref[...])
    _cg_put(f_ref, filt)

    @pl.when(i == 0)
    def _():
        ss_ref[...] = jnp.zeros_like(ss_ref)

    ss_ref[...] += jnp.sum(filt * filt, axis=0, keepdims=True)


def _hyena_filters(L, w1, b1, freq, w2, b2, w3, decay):
    emb, fh = w1.shape
    C2 = w3.shape[1]
    tm = min(256, L)
    w1p = jnp.zeros((LANES, fh), F32).at[:emb].set(w1)
    return pl.pallas_call(
        functools.partial(_filt_kernel, L=L),
        grid=(L // tm,),
        in_specs=[_full((LANES, fh)), _full((1, fh)), _full((1, fh)), _full((fh, fh)), _full((1, fh)),
                  _full((fh, C2)), _full((1, C2))],
        out_specs=[pl.BlockSpec((C2 // LANES, tm, LANES), lambda i: (0, i, 0)), _full((1, C2))],
        out_shape=[jax.ShapeDtypeStruct((C2 // LANES, L, LANES), F32), jax.ShapeDtypeStruct((1, C2), F32)],
        compiler_params=_cparams("arbitrary"),
        name="hyena_filters",
    )(w1p, b1.reshape(1, fh), freq.reshape(1, fh), w2, b2.reshape(1, fh), w3, decay.reshape(1, C2))


def _dft_tables(L):
    N = 2 * L
    N2 = FFT_N2
    N1 = N // N2
    k1 = jnp.arange(N1, dtype=jnp.int32)
    n1 = jnp.arange(N1 // 2, dtype=jnp.int32)
    n2 = jnp.arange(N2, dtype=jnp.int32)
    a = ((k1[:, None] * n1[None, :]) % N1).astype(F32) * (2.0 * math.pi / N1)
    b = ((n2[:, None] * k1[None, :]) % N).astype(F32) * (2.0 * math.pi / N)
    cos_a, sin_a, cos_b, sin_b = jnp.cos(a)[None], jnp.sin(a)[None], jnp.cos(b)[:, :, None], jnp.sin(b)[:, :, None]
    ca = (cos_b * cos_a - sin_b * sin_a).astype(BF16)
    sa = (sin_b * cos_a + cos_b * sin_a).astype(BF16)
    cd, sd = jnp.swapaxes(ca, 1, 2), jnp.swapaxes(sa, 1, 2)
    ph2 = (n2[:, None] * n2[None, :]) % N2
    ang2 = ph2.astype(F32) * (2.0 * math.pi / N2)
    c2, s2 = jnp.cos(ang2), jnp.sin(ang2)
    fwd = jnp.block([[c2, s2], [-s2, c2]]).astype(BF16)
    inv = jnp.block([[c2, -s2], [s2, c2]]).astype(BF16)
    return (ca, sa), (cd, sd), (fwd, inv)


def _fft_a_kernel(x_ref, c_ref, s_ref, a_ref):
    for r in range(SUB):
        x = _cg_load(x_ref, r).astype(BF16)
        re = jnp.dot(c_ref[r], x, preferred_element_type=F32)
        im = -jnp.dot(s_ref[r], x, preferred_element_type=F32)
        _cg_store(a_ref, r, _pack(re, im))


def _fft_a(x, gc, gblock, nblocks, tabs):
    ca, sa = tabs
    N2, N1, H1 = ca.shape
    x4 = x.reshape(x.shape[0], H1, N2, LANES)
    tspec = pl.BlockSpec((SUB, N1, H1), lambda j, cb: (j, 0, 0))
    return pl.pallas_call(
        _fft_a_kernel,
        grid=(N2 // SUB, nblocks),
        in_specs=[pl.BlockSpec((gc, H1, SUB, LANES), lambda j, cb: (gblock + cb, 0, j, 0)), tspec, tspec],
        out_specs=pl.BlockSpec((gc, N1, SUB, LANES), lambda j, cb: (cb, 0, j, 0)),
        out_shape=jax.ShapeDtypeStruct((gc * nblocks, N1, N2, LANES), U32),
        compiler_params=_cparams("parallel", "parallel"),
        name="fft_stage_a",
    )(x4, ca, sa)


def _cdft(t_ref, xr, xi):
    n = xr.shape[0]
    out = jnp.dot(t_ref[...], jnp.concatenate([xr, xi], axis=0), preferred_element_type=F32)
    return out[:n], out[n:]


def _fft_h_kernel(a_ref, f_ref, sc_ref, h_ref):
    for t in range(FFT_KB):
        xr, xi = _cdft(f_ref, *_unpack(_cg_cat(a_ref.at[:, t])))
        k1 = pl.program_id(0) * FFT_KB + t
        sc = sc_ref[...] * jnp.where((k1 & 2) == 0, 1.0, -1.0)
        if t % 2 == 0:
            _cg_put(h_ref.at[:, t], _pack(xr * sc, xi * sc))
        else:
            _cg_put(h_ref.at[:, t], _pack(xi * (-sc), xr * sc))


def _fft_h(a, gc, tab2, scale):
    G, N1, N2, _ = a.shape
    blk = pl.BlockSpec((gc, FFT_KB, N2, LANES), lambda j, cb: (cb, j, 0, 0))
    return pl.pallas_call(
        _fft_h_kernel,
        grid=(N1 // FFT_KB, G // gc),
        in_specs=[blk, _full(tab2[0].shape), pl.BlockSpec((1, gc * LANES), lambda j, cb: (0, cb))],
        out_specs=blk,
        out_shape=jax.ShapeDtypeStruct(a.shape, U32),
        compiler_params=_cparams("parallel", "parallel"),
        name="fft_filter_spectrum",
    )(a, tab2[0], scale)


def _fft_b_kernel(a_ref, h_ref, f_ref, i_ref, p_ref):
    for t in range(FFT_KB):
        xr, xi = _cdft(f_ref, *_unpack(_cg_cat(a_ref.at[:, t])))
        hr, hi = _unpack(_cg_cat(h_ref.at[:, t]))
        hr, hi = hr.astype(F32), hi.astype(F32)
        yr = (xr * hr - xi * hi).astype(BF16)
        yi = (xr * hi + xi * hr).astype(BF16)
        _cg_put(p_ref.at[:, t], _pack(*_cdft(i_ref, yr, yi)))


def _fft_b(a, h, hblock, tab2):
    G, N1, N2, _ = a.shape
    blk = pl.BlockSpec((G, FFT_KB, N2, LANES), lambda j: (0, j, 0, 0))
    hblk = pl.BlockSpec((G, FFT_KB, N2, LANES), lambda j: (hblock, j, 0, 0))
    return pl.pallas_call(
        _fft_b_kernel,
        grid=(N1 // FFT_KB,),
        in_specs=[blk, hblk, _full(tab2[0].shape), _full(tab2[1].shape)],
        out_specs=blk,
        out_shape=jax.ShapeDtypeStruct(a.shape, U32),
        compiler_params=_cparams("parallel"),
        name="fft_stage_b",
    )(a, h, tab2[0], tab2[1])


def _fft_d_kernel(p_ref, c_ref, s_ref, g_ref, v_ref, fb_ref, o_ref):
    for r in range(SUB):
        pr, pi = _unpack(_cg_load(p_ref, r))
        y = (jnp.dot(c_ref[r], pr, preferred_element_type=F32)
             - jnp.dot(s_ref[r], pi, preferred_element_type=F32))
        v = _cg_load(v_ref, r)
        _cg_store(o_ref, r, _cg_load(g_ref, r) * (y + v * fb_ref[...]))


def _fft_d(p, tabs, gsrc, gblock, vsrc, vblock, fbias):
    cd, sd = tabs
    N2, H1, N1 = cd.shape
    G = p.shape[0]
    L = H1 * N2
    tspec = pl.BlockSpec((SUB, H1, N1), lambda j: (j, 0, 0))
    sig = lambda gb: pl.BlockSpec((G, H1, SUB, LANES), lambda j: (gb, 0, j, 0))
    out = pl.pallas_call(
        _fft_d_kernel,
        grid=(N2 // SUB,),
        in_specs=[pl.BlockSpec((G, N1, SUB, LANES), lambda j: (0, 0, j, 0)), tspec, tspec, sig(gblock), sig(vblock),
                  _full((1, G * LANES))],
        out_specs=sig(0),
        out_shape=jax.ShapeDtypeStruct((G, H1, N2, LANES), F32),
        compiler_params=_cparams("parallel"),
        name="fft_stage_d",
    )(p, cd, sd, gsrc.reshape(gsrc.shape[0], H1, N2, LANES), vsrc.reshape(vsrc.shape[0], H1, N2, LANES), fbias)
    return out.reshape(G, L, LANES)


def _hyena_long(z, filt, ssq, fbias, tables):
    L = z.shape[1]
    gc = z.shape[0] // 3
    C = gc * LANES
    N = 2 * L
    taba, tabd, tab2 = tables
    scale = lax.rsqrt(ssq + H_DECAY_EPS) * (1.0 / N)
    h = _fft_h(_fft_a(filt, gc, 0, 2, taba), gc, tab2, scale)
    fb = fbias.reshape(1, 2 * C)
    p = _fft_b(_fft_a(z, gc, 2, 1, taba), h, 0, tab2)
    y1 = _fft_d(p, tabd, z, 0, z, 2, fb[:, :C])
    p = _fft_b(_fft_a(y1, gc, 0, 1, taba), h, 1, tab2)
    return _fft_d(p, tabd, z, 1, y1, 0, fb[:, C:])


def _short_conv_kernel(z_ref, f_ref, ss_ref, fb_ref, cf_ref, sf_ref, cd_ref, sd_ref, o_ref):
    L = z_ref.shape[1]
    G = o_ref.shape[0]
    C = G * LANES
    N = 2 * L

    def dft(xb):
        return (jnp.dot(cf_ref[...], xb, preferred_element_type=F32),
                -jnp.dot(sf_ref[...], xb, preferred_element_type=F32))

    def conv(u, blk, gate):
        sl = slice(blk * C, (blk + 1) * C)
        ur, ui = dft(u.astype(BF16))
        fr, fi = dft(_cg_cat(f_ref.at[blk * G:(blk + 1) * G]).astype(BF16))
        q = lax.broadcasted_iota(jnp.int32, fr.shape, 0) & 3
        fr, fi = (jnp.where(q == 0, fr, jnp.where(q == 1, -fi, jnp.where(q == 2, -fr, fi))),
                  jnp.where(q == 0, fi, jnp.where(q == 1, fr, jnp.where(q == 2, -fi, -fr))))
        yr = (ur * fr - ui * fi).astype(BF16)
        yi = (ur * fi + ui * fr).astype(BF16)
        y = (jnp.dot(cd_ref[...], yr, preferred_element_type=F32)
             - jnp.dot(sd_ref[...], yi, preferred_element_type=F32))
        y = y * (lax.rsqrt(ss_ref[:, sl] + H_DECAY_EPS) * (1.0 / N))
        return gate * (y + u * fb_ref[:, sl])

    x1 = _cg_cat(z_ref.at[0:G])
    x2 = _cg_cat(z_ref.at[G:2 * G])
    v = _cg_cat(z_ref.at[2 * G:3 * G])
    _cg_put(o_ref, conv(conv(v, 0, x1), 1, x2))


def _hyena_short(z, filt, ssq, fbias):
    L = z.shape[1]
    G = z.shape[0] // 3
    C = G * LANES
    N = 2 * L
    k = jnp.arange(N, dtype=jnp.int32)
    n = jnp.arange(L, dtype=jnp.int32)
    ang = ((k[:, None] * n[None, :]) % N).astype(F32) * (2.0 * math.pi / N)
    cf, sf = jnp.cos(ang).astype(BF16), jnp.sin(ang).astype(BF16)
    args = (z, filt, ssq, fbias.reshape(1, 2 * C), cf, sf, cf.T, sf.T)
    return pl.pallas_call(
        _short_conv_kernel,
        grid=(1,),
        in_specs=[_full(a.shape) for a in args],
        out_specs=_full((G, L, LANES)),
        out_shape=jax.ShapeDtypeStruct((G, L, LANES), F32),
        compiler_params=_cparams("arbitrary"),
        name="hyena_short_conv",
    )(*args)


def _router_kernel(x_ref, g_ref, mod_ref, rw_ref, rb_ref, c0_ref, xt_ref, rt_ref, cnt_ref, carry):
    i = pl.program_id(0)

    @pl.when(i == 0)
    def _():
        carry[...] = c0_ref[...]

    u = _norm_mod(x_ref[...], g_ref[...], mod_ref[...], 3)
    _tt_store(xt_ref, u)
    u_hi = u.astype(BF16)
    u_lo = (u - u_hi.astype(F32)).astype(BF16)
    logits = jnp.dot(jnp.concatenate([u_hi, u_hi, u_lo], axis=1), rw_ref[...], preferred_element_type=F32) + rb_ref[...]
    tm = u.shape[0]
    lane = lax.broadcasted_iota(jnp.int32, logits.shape, 1).astype(F32)
    ninf = -jnp.inf

    def top(vals):
        mx = jnp.max(vals, axis=-1, keepdims=True)
        ix = jnp.min(jnp.where(vals == mx, lane, float(LANES)), axis=-1, keepdims=True)
        return mx, ix

    lg = jnp.where(lane < N_GROUPS, logits, ninf)
    gmax, gidx = top(lg)
    g_w = 1.0 / jnp.sum(jnp.exp(lg - gmax), axis=-1, keepdims=True)
    lo = N_GROUPS + EXP_PER_GROUP * gidx
    le = jnp.where((lane >= lo) & (lane < lo + EXP_PER_GROUP), logits, ninf)
    m1, i1 = top(le)
    m2, i2 = top(jnp.where(lane == i1, ninf, le))
    r = jnp.exp(m2 - m1)
    w1 = g_w / (1.0 + r)
    w2 = w1 * r
    e1 = i1 - N_GROUPS
    e2 = i2 - N_GROUPS
    oh1 = lane == e1
    oh2 = lane == e2
    both = jnp.where(oh1 | oh2, 1.0, 0.0)
    rr = lax.broadcasted_iota(jnp.int32, (tm, tm), 0)
    cc = lax.broadcasted_iota(jnp.int32, (tm, tm), 1)
    earlier = jnp.where(cc < rr, 1.0, 0.0).astype(BF16)
    before = jnp.dot(earlier, both.astype(BF16), preferred_element_type=F32) + carry[...]
    rank1 = jnp.sum(jnp.where(oh1, before, 0.0), axis=-1, keepdims=True)
    rank2 = jnp.sum(jnp.where(oh2, before, 0.0), axis=-1, keepdims=True)
    carry[...] += jnp.sum(both, axis=0, keepdims=True)
    cnt_ref[...] = carry[...]
    out = jnp.zeros(logits.shape, F32)
    for j, col in enumerate((e1, e2, w1, w2, rank1, rank2)):
        out = jnp.where(lane == j, col, out)
    rt_ref[...] = out


def _router(h, g, mod, rw, rb, counts0):
    L, D = h.shape
    tm = min(256, L)
    return pl.pallas_call(
        _router_kernel,
        grid=(L // tm,),
        in_specs=[pl.BlockSpec((tm, D), lambda i: (i, 0)), _full((1, D)), _full(mod.shape),
                  _full(rw.shape), _full(rb.shape), _full((1, LANES))],
        out_specs=[pl.BlockSpec((tm, D // LANES, LANES), lambda i: (i, 0, 0)),
                   pl.BlockSpec((tm, LANES), lambda i: (i, 0)), _full((1, LANES))],
        out_shape=[jax.ShapeDtypeStruct((L, D // LANES, LANES), F32), jax.ShapeDtypeStruct((L, LANES), F32),
                   jax.ShapeDtypeStruct((1, LANES), F32)],
        scratch_shapes=[pltpu.VMEM((1, LANES), F32)],
        compiler_params=_cparams("arbitrary"),
        name="moe_router",
    )(h, g.reshape(1, D), mod, rw, rb, counts0)


def _dispatch_kernel(dest_ref, x_ref, xb_in_ref, xb_ref, sem):
    del xb_in_ref
    tm = x_ref.shape[0]
    a0 = 2 * tm * pl.program_id(0)

    def issue(r, carry):
        for k in range(2):
            pltpu.make_async_copy(x_ref.at[r], xb_ref.at[dest_ref[a0 + 2 * r + k]], sem).start()
        return carry

    def wait(r, carry):
        pltpu.make_async_copy(x_ref.at[0], xb_ref.at[0], sem).wait()
        return carry

    lax.fori_loop(0, tm, issue, 0, unroll=DMA_UNROLL)
    lax.fori_loop(0, 2 * tm, wait, 0, unroll=DMA_UNROLL)


def _dispatch(xt, dest, R):
    T, G, _ = xt.shape
    tm = min(256, T)
    return pl.pallas_call(
        _dispatch_kernel,
        grid_spec=pltpu.PrefetchScalarGridSpec(
            num_scalar_prefetch=1,
            grid=(T // tm,),
            in_specs=[pl.BlockSpec((tm, G, LANES), lambda i, d: (i, 0, 0)), pl.BlockSpec(memory_space=pl.ANY)],
            out_specs=pl.BlockSpec(memory_space=pl.ANY),
            scratch_shapes=[pltpu.SemaphoreType.DMA(())]),
        out_shape=jax.ShapeDtypeStruct((R, G, LANES), F32),
        input_output_aliases={2: 0},
        compiler_params=_cparams("arbitrary"),
        name="moe_dispatch",
    )(dest, xt, jnp.zeros((R, G, LANES), F32))


def _expert_kernel(be_ref, nb_ref, x_ref, w13_ref, w2_ref, o_ref, w13b, w2b):
    b = pl.program_id(0)
    live = b < nb_ref[0]
    new_expert = (b == 0) | (be_ref[b] != be_ref[jnp.maximum(b - 1, 0)])

    @pl.when(jnp.logical_not(live))
    def _():
        o_ref[...] = jnp.zeros_like(o_ref)

    @pl.when(live & new_expert)
    def _():
        w13b[...] = w13_ref[0].astype(BF16)
        w2b[...] = w2_ref[0].astype(BF16)

    @pl.when(live)
    def _():
        de = w2b.shape[0]
        x = _tt_load(x_ref).astype(BF16)
        hcat = jnp.dot(x, w13b[...], preferred_element_type=F32)
        a = (_silu(hcat[:, :de]) * hcat[:, de:]).astype(BF16)
        _tt_store(o_ref, jnp.dot(a, w2b[...], preferred_element_type=F32))


def _experts(xb, blk_exp, n_used, w13, w2):
    R, G, _ = xb.shape
    D = G * LANES
    de = w2.shape[1]
    nb = R // MOE_BLOCK
    blk = pl.BlockSpec((MOE_BLOCK, G, LANES), lambda b, be, nu: (b, 0, 0))
    return pl.pallas_call(
        _expert_kernel,
        grid_spec=pltpu.PrefetchScalarGridSpec(
            num_scalar_prefetch=2,
            grid=(nb,),
            in_specs=[blk,
                      pl.BlockSpec((1, D, 2 * de), lambda b, be, nu: (be[b], 0, 0)),
                      pl.BlockSpec((1, de, D), lambda b, be, nu: (be[b], 0, 0))],
            out_specs=blk,
            scratch_shapes=[pltpu.VMEM((D, 2 * de), BF16), pltpu.VMEM((de, D), BF16)]),
        out_shape=jax.ShapeDtypeStruct(xb.shape, F32),
        compiler_params=_cparams("arbitrary"),
        name="moe_experts",
    )(blk_exp, n_used, xb, w13, w2)


def _combine_kernel(dest_ref, h_ref, rt_ref, mod_ref, yb_ref, o_ref, buf, sem, *, a_base):
    tm = h_ref.shape[0]
    i = pl.program_id(0)
    slot = i % 2

    def fetch(tile, sl):
        a0 = a_base + 2 * tm * tile

        def issue(r, carry):
            for k in range(2):
                pltpu.make_async_copy(yb_ref.at[dest_ref[a0 + 2 * r + k]], buf.at[sl, k, r], sem.at[sl]).start()
            return carry

        lax.fori_loop(0, tm, issue, 0, unroll=DMA_UNROLL)

    @pl.when(i == 0)
    def _():
        fetch(0, 0)

    @pl.when(i + 1 < pl.num_programs(0))
    def _():
        fetch(i + 1, 1 - slot)

    def wait(r, carry):
        pltpu.make_async_copy(yb_ref.at[0], buf.at[slot, 0, 0], sem.at[slot]).wait()
        return carry

    lax.fori_loop(0, 2 * tm, wait, 0, unroll=DMA_UNROLL)
    rt = rt_ref[...]
    y = rt[:, 2:3] * _tt_load(buf.at[slot, 0]) + rt[:, 3:4] * _tt_load(buf.at[slot, 1])
    o_ref[...] = h_ref[...] + mod_ref[5:6, :] * y


def _combine(h, yb, dest, row0, route, mod):
    L, D = h.shape
    tm = min(256, L)
    return pl.pallas_call(
        functools.partial(_combine_kernel, a_base=2 * row0),
        grid_spec=pltpu.PrefetchScalarGridSpec(
            num_scalar_prefetch=1,
            grid=(L // tm,),
            in_specs=[pl.BlockSpec((tm, D), lambda i, d: (i, 0)), pl.BlockSpec((tm, LANES), lambda i, d: (i, 0)),
                      pl.BlockSpec(mod.shape, lambda i, d: (0, 0)), pl.BlockSpec(memory_space=pl.ANY)],
            out_specs=pl.BlockSpec((tm, D), lambda i, d: (i, 0)),
            scratch_shapes=[pltpu.VMEM((2, 2, tm, D // LANES, LANES), F32), pltpu.SemaphoreType.DMA((2,))]),
        out_shape=jax.ShapeDtypeStruct((L, D), F32),
        compiler_params=_cparams("arbitrary"),
        name="moe_combine",
    )(dest, h, route, mod, yb)


def _moe(streams, g, rw, rb, w13, w2, layer):
    counts = jnp.zeros((1, LANES), F32)
    xts, routes = [], []
    for h, mod in streams:
        xt, rt, counts = _router(h, g, mod, rw, rb, counts)
        xts.append(xt)
        routes.append(rt)
    xt = jnp.concatenate(xts, axis=0) if len(xts) > 1 else xts[0]
    route = jnp.concatenate(routes, axis=0) if len(routes) > 1 else routes[0]
    T = xt.shape[0]
    A = 2 * T
    cnt = counts[0, :N_EXPERTS].astype(jnp.int32)
    padded = (cnt + MOE_BLOCK - 1) // MOE_BLOCK * MOE_BLOCK
    p_ends = jnp.cumsum(padded)
    p_starts = p_ends - padded
    experts = route[:, 0:2].astype(jnp.int32)
    onehot = experts[:, :, None] == jnp.arange(N_EXPERTS, dtype=jnp.int32)
    dest = (jnp.sum(jnp.where(onehot, p_starts, 0), axis=-1) + route[:, 4:6].astype(jnp.int32)).reshape(A)
    n_blocks = -(-A // MOE_BLOCK) + N_EXPERTS
    blk_start = jnp.arange(n_blocks, dtype=jnp.int32) * MOE_BLOCK
    blk_exp = jnp.minimum(jnp.sum(p_ends[None, :] <= blk_start[:, None], axis=1), N_EXPERTS - 1).astype(jnp.int32)
    blk_exp = blk_exp + layer * N_EXPERTS
    n_used = (p_ends[-1:] // MOE_BLOCK).astype(jnp.int32)
    yb = _experts(_dispatch(xt, dest, n_blocks * MOE_BLOCK), blk_exp, n_used, w13, w2)
    outs, row0 = [], 0
    for (h, mod), rt in zip(streams, routes):
        outs.append(_combine(h, yb, dest, row0, rt, mod))
        row0 += h.shape[0]
    return outs


def _final_kernel(x_ref, g_ref, o_ref):
    x = x_ref[...]
    o_ref[...] = x * lax.rsqrt(jnp.mean(x * x, axis=-1, keepdims=True) + EPS) * g_ref[...]


def _final_norm(h, g):
    L, D = h.shape
    tm = min(512, L)
    return pl.pallas_call(
        _final_kernel,
        grid=(L // tm,),
        in_specs=[pl.BlockSpec((tm, D), lambda i: (i, 0)), _full((1, D))],
        out_specs=pl.BlockSpec((tm, D), lambda i: (i, 0)),
        out_shape=jax.ShapeDtypeStruct((L, D), F32),
        compiler_params=_cparams("parallel"),
        name="final_norm",
    )(h, g.reshape(1, D))


def _zero_state(heads, dqk, dv):
    return (jnp.zeros((heads, dqk, dv), F32), jnp.zeros((heads, 1, dqk), F32), jnp.zeros((heads, 1, LANES), F32))


def kernel(x, c, ctx, c_ctx, ada_w, ada_b, norm1_g, norm2_g, rg_w, rg_b, re_w, re_b, moe_w13, moe_w2, m_in_w, m_conv_w, m_conv_b, m_q_w, m_k_w, m_v_w, m_gate_w, m_gate_b, m_norm_g, m_skip, m_out_w, h_in_w, h_in_b, h_sc_w, h_sc_b, h_f_w1, h_f_b1, h_f_freq, h_f_w2, h_f_b2, h_f_w3, h_decay, h_fbias, h_out_w, h_out_b, final_g):
    B, S, D = x.shape
    assert B == 1, "kernel is written for the single-sequence problem shape"
    depth = ada_w.shape[0]
    hl, hc = x[0], ctx[0]
    cc = jnp.zeros((8, D), F32).at[0].set(c[0]).at[1].set(c_ctx)
    mods = _ada(cc, ada_w, ada_b).reshape(depth, 8, 6, D)
    pad8 = lambda m: jnp.concatenate([m, jnp.zeros((2, D), F32)], axis=0)
    nrt = N_GROUPS + N_EXPERTS
    zeros_d = jnp.zeros((D,), F32)
    tables = _dft_tables(S) if depth > 1 else None
    w13_all = moe_w13.reshape((-1,) + moe_w13.shape[2:])
    w2_all = moe_w2.reshape((-1,) + moe_w2.shape[2:])

    for i in range(depth):
        kind, slot = i % N_MIXERS, i // N_MIXERS
        col_major = (slot % 2) == 1
        ctx_live = any(j % N_MIXERS == 0 for j in range(i + 1, depth))
        ml, mc = pad8(mods[i, 0]), pad8(mods[i, 1])

        if kind == 0:
            w_in = m_in_w[slot].astype(BF16)
            w_out = m_out_w[slot].astype(BF16)
            inner = w_out.shape[0]
            mp = (m_conv_w[slot], m_conv_b[slot], m_q_w[slot], m_k_w[slot], m_v_w[slot], m_gate_w[slot],
                  m_gate_b[slot], m_norm_g[slot], m_skip[slot])
            heads, _, dqk = m_q_w[slot].shape
            dv = m_v_w[slot].shape[2]
            zb = jnp.zeros((w_in.shape[1],), F32)
            xo_c = _norm_proj(hc, norm1_g[i], mc, w_in, zb, False)
            yc, st_f, st_b = _mlstm_stream(xo_c, mp, _zero_state(heads, dqk, dv), _zero_state(heads, dqk, dv))
            xo_l = _norm_proj(hl, norm1_g[i], ml, w_in, zb, col_major)
            yl, _, _ = _mlstm_stream(xo_l, mp, st_f, st_b)
            hl = _out_proj(yl, w_out, zeros_d, hl, ml, 2, col_major)
            if ctx_live:
                hc = _out_proj(yc, w_out, zeros_d, hc, mc, 2, False)
        else:
            w_in = h_in_w[slot].astype(BF16)
            w_out = h_out_w[slot].astype(BF16)
            fp = (h_f_w1[slot], h_f_b1[slot], h_f_freq[slot], h_f_w2[slot], h_f_b2[slot], h_f_w3[slot], h_decay[slot])
            zl = _sconv(_norm_proj(hl, norm1_g[i], ml, w_in, h_in_b[slot], col_major), h_sc_w[slot], h_sc_b[slot])
            filt, ssq = _hyena_filters(S, *fp)
            yl = _hyena_long(zl, filt, ssq, h_fbias[slot], tables)
            hl = _out_proj(yl, w_out, h_out_b[slot], hl, ml, 2, col_major)
            if ctx_live:
                zc = _sconv(_norm_proj(hc, norm1_g[i], mc, w_in, h_in_b[slot], False), h_sc_w[slot], h_sc_b[slot])
                filt_c, ssq_c = _hyena_filters(hc.shape[0], *fp)
                yc = _hyena_short(zc, filt_c, ssq_c, h_fbias[slot])
                hc = _out_proj(yc, w_out, h_out_b[slot], hc, mc, 2, False)

        rw = jnp.zeros((D, LANES), F32).at[:, :N_GROUPS].set(rg_w[i]).at[:, N_GROUPS:nrt].set(re_w[i])
        rw_hi = rw.astype(BF16)
        rw_lo = (rw - rw_hi.astype(F32)).astype(BF16)
        rw = jnp.concatenate([rw_hi, rw_lo, rw_hi], axis=0)
        rb = jnp.zeros((1, LANES), F32).at[0, :N_GROUPS].set(rg_b[i]).at[0, N_GROUPS:nrt].set(re_b[i])
        if ctx_live:
            hc, hl = _moe([(hc, mc), (hl, ml)], norm2_g[i], rw, rb, w13_all, w2_all, i)
        else:
            (hl,) = _moe([(hl, ml)], norm2_g[i], rw, rb, w13_all, w2_all, i)

    return _final_norm(hl, final_g)[None]
```

```python
import functools
import math

import jax
import jax.numpy as jnp
from jax import lax
from jax.experimental import pallas as pl
from jax.experimental.pallas import tpu as pltpu

F32 = jnp.float32
BF16 = jnp.bfloat16
U32 = jnp.uint32
HI = lax.Precision.HIGHEST

EPS = 1e-6
GRID_W = 64
N_MIXERS = 2
M_HEADS = 4
H_BANDS = 16
N_GROUPS = 4
EXP_PER_GROUP = 8
N_EXPERTS = N_GROUPS * EXP_PER_GROUP
MOE_BLOCK = 256
DMA_UNROLL = 8
LANES = 128
HALO = 16
SUB = 8
FFT_N2 = 128
FFT_KB = 4
H_DECAY_EPS = EPS
VMEM_LIMIT = 56 * 1024 * 1024


def _cparams(*sem):
    return pltpu.CompilerParams(dimension_semantics=sem, vmem_limit_bytes=VMEM_LIMIT)


def _full(shape):
    n = len(shape)
    return pl.BlockSpec(shape, lambda *_: (0,) * n)


def _raster(L, tr_pref=128):
    rows = L // GRID_W
    return rows, min(tr_pref, rows)


def _cg_cat(ref):
    return jnp.concatenate([ref[s] for s in range(ref.shape[0])], axis=1)


def _cg_put(ref, val):
    for s in range(ref.shape[0]):
        ref[s] = val[:, s * LANES:(s + 1) * LANES]


def _cg_load(ref4, r):
    G, n = ref4.shape[0], ref4.shape[1]
    flat = ref4.reshape(G, n * SUB, LANES)
    return jnp.concatenate([flat[s, pl.ds(r, n, stride=SUB), :] for s in range(G)], axis=1)


def _cg_store(ref4, r, val):
    G, n = ref4.shape[0], ref4.shape[1]
    flat = ref4.reshape(G, n * SUB, LANES)
    for s in range(G):
        flat[s, pl.ds(r, n, stride=SUB), :] = val[:, s * LANES:(s + 1) * LANES]


def _tt_load(ref3):
    n, G = ref3.shape[0], ref3.shape[1]
    flat = ref3.reshape(n * G, LANES)
    return jnp.concatenate([flat[pl.ds(s, n, stride=G), :] for s in range(G)], axis=1)


def _tt_store(ref3, val):
    n, G = ref3.shape[0], ref3.shape[1]
    flat = ref3.reshape(n * G, LANES)
    for s in range(G):
        flat[pl.ds(s, n, stride=G), :] = val[:, s * LANES:(s + 1) * LANES]


def _pack(re, im):
    hi = pltpu.bitcast(re.astype(BF16).astype(F32), U32)
    lo = pltpu.bitcast(im.astype(BF16).astype(F32), U32)
    return (hi & jnp.uint32(0xFFFF0000)) | (lo >> 16)


def _unpack(w):
    re = pltpu.bitcast(w & jnp.uint32(0xFFFF0000), F32)
    im = pltpu.bitcast(w << 16, F32)
    return re.astype(BF16), im.astype(BF16)


def _sigmoid(x):
    return 1.0 / (1.0 + jnp.exp(-x))


def _silu(x):
    return x * _sigmoid(x)


def _norm_mod(x, g, mod, k):
    ms = jnp.mean(x * x, axis=-1, keepdims=True)
    y = x * lax.rsqrt(ms + EPS) * g
    return y * (1.0 + mod[k + 1:k + 2, :]) + mod[k:k + 1, :]


def _ada_kernel(c_ref, w_ref, b_ref, o_ref):
    s = _silu(c_ref[...])
    o_ref[0] = jnp.dot(s, w_ref[0], precision=HI, preferred_element_type=F32) + b_ref[0]


def _ada(cc, ada_w, ada_b):
    depth, D, N = ada_w.shape
    tn = N // 6
    return pl.pallas_call(
        _ada_kernel,
        grid=(depth, N // tn),
        in_specs=[pl.BlockSpec((8, D), lambda l, j: (0, 0)),
                  pl.BlockSpec((1, D, tn), lambda l, j: (l, 0, j)),
                  pl.BlockSpec((1, 1, tn), lambda l, j: (l, 0, j))],
        out_specs=pl.BlockSpec((1, 8, tn), lambda l, j: (l, 0, j)),
        out_shape=jax.ShapeDtypeStruct((depth, 8, N), F32),
        compiler_params=_cparams("parallel", "parallel"),
        name="ada",
    )(cc, ada_w, ada_b.reshape(depth, 1, N))


def _raster_col(refs, c):
    tr = refs[0].shape[0]
    return jnp.concatenate([r.reshape(tr * SUB, LANES)[pl.ds(c, tr, stride=SUB), :] for r in refs], axis=1)


def _lane_blocks(tr, D, index):
    return [pl.BlockSpec((tr, SUB, LANES), functools.partial(index, s)) for s in range(D // LANES)]


def _norm_proj_kernel(*refs, nchunk, col_major):
    *x_refs, g_ref, mod_ref, w_ref, b_ref, o_ref = refs
    N = o_ref.shape[-1]
    step = N // nchunk
    for c in range(SUB if col_major else 1):
        x = _raster_col(x_refs, c) if col_major else x_refs[0][...]
        u = _norm_mod(x, g_ref[...], mod_ref[...], 0).astype(BF16)
        for j in range(nchunk):
            sl = slice(j * step, (j + 1) * step)
            acc = (jnp.dot(u, w_ref[:, sl], preferred_element_type=F32) + b_ref[:, sl]).astype(o_ref.dtype)
            if col_major:
                o_ref[c, :, sl] = acc
            else:
                o_ref[:, sl] = acc


def _norm_proj(x, g, mod, w, b, col_major):
    L, D = x.shape
    N = w.shape[1]
    consts = [_full((1, D)), _full(mod.shape), _full((D, N)), _full((1, N))]
    kern = functools.partial(_norm_proj_kernel, nchunk=max(1, N // 1024), col_major=col_major)
    args = (g.reshape(1, D), mod, w, b.reshape(1, N))
    if col_major:
        rows, tr = _raster(L)
        out = pl.pallas_call(
            kern,
            grid=(GRID_W // SUB, rows // tr),
            in_specs=_lane_blocks(tr, D, lambda s, iw, ir: (ir, iw, s)) + consts,
            out_specs=pl.BlockSpec((SUB, tr, N), lambda iw, ir: (iw, ir, 0)),
            out_shape=jax.ShapeDtypeStruct((GRID_W, rows, N), BF16),
            compiler_params=_cparams("parallel", "parallel"),
            name="norm_proj_colmajor",
        )(*([x.reshape(rows, GRID_W, D)] * (D // LANES)), *args)
        return out.reshape(L, N)
    tm = min(256, L)
    return pl.pallas_call(
        kern,
        grid=(L // tm,),
        in_specs=[pl.BlockSpec((tm, D), lambda i: (i, 0))] + consts,
        out_specs=pl.BlockSpec((tm, N), lambda i: (i, 0)),
        out_shape=jax.ShapeDtypeStruct((L, N), BF16),
        compiler_params=_cparams("parallel"),
        name="norm_proj",
    )(x, *args)


def _out_proj_kernel(y_ref, w_ref, b_ref, *refs, gate_row, col_major, y_cg):
    *h_refs, mod_ref, o_ref = refs
    gate = mod_ref[gate_row:gate_row + 1, :]
    for c in range(SUB if col_major else 1):
        if y_cg:
            y = jnp.concatenate([y_ref[s, c] if col_major else y_ref[s] for s in range(y_ref.shape[0])],
                                axis=1).astype(BF16)
        else:
            y = y_ref[c] if col_major else y_ref[...]
        acc = jnp.dot(y, w_ref[...], preferred_element_type=F32) + b_ref[...]
        if col_major:
            o_ref[:, c, :] = _raster_col(h_refs, c) + gate * acc
        else:
            o_ref[...] = h_refs[0][...] + gate * acc


def _out_proj(y, w, b, h, mod, gate_row, col_major):
    L, D = h.shape
    K = w.shape[0]
    y_cg = y.ndim == 3
    G = K // LANES
    kern = functools.partial(_out_proj_kernel, gate_row=gate_row, col_major=col_major, y_cg=y_cg)
    if col_major:
        rows, tr = _raster(L)
        hspec = pl.BlockSpec((tr, SUB, D), lambda iw, ir: (ir, iw, 0))
        if y_cg:
            yspec = pl.BlockSpec((G, SUB, tr, LANES), lambda iw, ir: (0, iw, ir, 0))
            yv = y.reshape(G, GRID_W, rows, LANES)
        else:
            yspec = pl.BlockSpec((SUB, tr, K), lambda iw, ir: (iw, ir, 0))
            yv = y.reshape(GRID_W, rows, K)
        out = pl.pallas_call(
            kern,
            grid=(GRID_W // SUB, rows // tr),
            in_specs=[yspec, _full((K, D)), _full((1, D))] + _lane_blocks(tr, D, lambda s, iw, ir: (ir, iw, s))
            + [_full(mod.shape)],
            out_specs=hspec,
            out_shape=jax.ShapeDtypeStruct((rows, GRID_W, D), F32),
            compiler_params=_cparams("parallel", "parallel"),
            name="out_proj_colmajor",
        )(yv, w, b.reshape(1, D), *([h.reshape(rows, GRID_W, D)] * (D // LANES)), mod)
        return out.reshape(L, D)
    tm = min(256, L)
    rows_spec = lambda c: pl.BlockSpec((tm, c), lambda i: (i, 0))
    yspec = pl.BlockSpec((G, tm, LANES), lambda i: (0, i, 0)) if y_cg else rows_spec(K)
    return pl.pallas_call(
        kern,
        grid=(L // tm,),
        in_specs=[yspec, _full((K, D)), _full((1, D)), rows_spec(D), _full(mod.shape)],
        out_specs=rows_spec(D),
        out_shape=jax.ShapeDtypeStruct((L, D), F32),
        compiler_params=_cparams("parallel"),
        name="out_proj",
    )(y, w, b.reshape(1, D), h, mod)


def _halo_specs(tm, L, ncol, cblock):
    hb = tm // HALO
    nb = L // HALO
    prev = pl.BlockSpec((HALO, ncol), lambda i: (jnp.maximum(i * hb - 1, 0), cblock))
    nxt = pl.BlockSpec((HALO, ncol), lambda i: (jnp.minimum((i + 1) * hb, nb - 1), cblock))
    return prev, nxt


def _conv3(x, prev_ref, next_ref, w, b):
    tm = x.shape[0]
    i = pl.program_id(0)
    first = i == 0
    last = i == pl.num_programs(0) - 1
    p = jnp.where(first, 0.0, prev_ref[HALO - 1:HALO, :].astype(F32))
    n = jnp.where(last, 0.0, next_ref[0:1, :].astype(F32))
    row = lax.broadcasted_iota(jnp.int32, x.shape, 0)
    xm1 = jnp.where(row == 0, p, pltpu.roll(x, 1, 0))
    xp1 = jnp.where(row == tm - 1, n, pltpu.roll(x, tm - 1, 0))
    return b + xm1 * w[0:1, :] + x * w[1:2, :] + xp1 * w[2:3, :]


def _mlstm_pre_kernel(x_ref, p_ref, n_ref, cw_ref, cb_ref, qw_ref, kw_ref, vw_ref, gw_ref, gb_ref,
                      xc_ref, q_ref, k_ref, v_ref, g_ref, gt_ref, *, heads, kscale):
    xm = x_ref[...]
    xc = _silu(_conv3(xm.astype(F32), p_ref, n_ref, cw_ref[...], cb_ref[...]))
    xcb = xc.astype(BF16)
    xc_ref[...] = xcb
    inner = xm.shape[1]
    dh = inner // heads
    dqk = qw_ref.shape[2]
    dv = vw_ref.shape[2]
    for h in range(heads):
        xs = xcb[:, h * dh:(h + 1) * dh]
        q_ref[:, h * dqk:(h + 1) * dqk] = jnp.dot(xs, qw_ref[h], preferred_element_type=F32).astype(BF16)
        kh = jnp.dot(xs, kw_ref[h], preferred_element_type=F32) * kscale
        k_ref[:, h * dqk:(h + 1) * dqk] = kh.astype(BF16)
        v_ref[:, h * dv:(h + 1) * dv] = jnp.dot(xm[:, h * dh:(h + 1) * dh], vw_ref[h],
                                                 preferred_element_type=F32).astype(BF16)
    g = (jnp.dot(xcb, gw_ref[:inner, :], preferred_element_type=F32)
         + jnp.dot(xm, gw_ref[inner:, :], preferred_element_type=F32) + gb_ref[...])
    lane = lax.broadcasted_iota(jnp.int32, g.shape, 1)
    is_forget = ((lane >= heads) & (lane < 2 * heads)) | ((lane >= 3 * heads) & (lane < 4 * heads))
    logsig = jnp.minimum(g, 0.0) - jnp.log(1.0 + jnp.exp(-jnp.abs(g)))
    g = jnp.where(is_forget, logsig, g)
    g_ref[...] = g
    gt_ref[...] = g.T[:gt_ref.shape[0], :]


def _mlstm_pre(xo, conv_w, conv_b, q_w, k_w, v_w, gate_w, gate_b):
    L = xo.shape[0]
    heads, dh, dqk = q_w.shape
    dv = v_w.shape[2]
    inner = heads * dh
    tm = min(256, L)
    ng = 4 * heads
    gw = jnp.zeros((2 * inner, LANES), BF16).at[:, :ng].set(gate_w.astype(BF16))
    gb = jnp.zeros((1, LANES), F32).at[0, :ng].set(gate_b)
    prev, nxt = _halo_specs(tm, L, inner, 0)
    rows = lambda c: pl.BlockSpec((tm, c), lambda i: (i, 0))
    return pl.pallas_call(
        functools.partial(_mlstm_pre_kernel, heads=heads, kscale=dqk ** -0.5),
        grid=(L // tm,),
        in_specs=[rows(inner), prev, nxt, _full((3, inner)), _full((1, inner)),
                  _full(q_w.shape), _full(k_w.shape), _full(v_w.shape), _full(gw.shape), _full(gb.shape)],
        out_specs=[rows(inner), rows(heads * dqk), rows(heads * dqk), rows(heads * dv), rows(LANES),
                   pl.BlockSpec((ng, tm), lambda i: (0, i))],
        out_shape=[jax.ShapeDtypeStruct((L, inner), BF16), jax.ShapeDtypeStruct((L, heads * dqk), BF16),
                   jax.ShapeDtypeStruct((L, heads * dqk), BF16), jax.ShapeDtypeStruct((L, heads * dv), BF16),
                   jax.ShapeDtypeStruct((L, LANES), F32), jax.ShapeDtypeStruct((ng, L), F32)],
        compiler_params=_cparams("parallel"),
        name="mlstm_pre",
    )(xo, xo, xo, conv_w, conv_b.reshape(1, inner), q_w.astype(BF16), k_w.astype(BF16), v_w.astype(BF16), gw, gb)


def _scan_kernel(*refs, heads, rev, final):
    if final:
        (q_ref, k_ref, v_ref, g_ref, gt_ref, c0_ref, n0_ref, m0_ref, hf_ref, xc_ref, o_ref, ng_ref, sk_ref,
         out_ref, cT_ref, nT_ref, mT_ref, C_sc, n_sc, m_sc) = refs
    else:
        (q_ref, k_ref, v_ref, g_ref, gt_ref, c0_ref, n0_ref, m0_ref,
         out_ref, cT_ref, nT_ref, mT_ref, C_sc, n_sc, m_sc) = refs
    ci = pl.program_id(0)

    @pl.when(ci == 0)
    def _():
        C_sc[...] = c0_ref[...]
        n_sc[...] = n0_ref[...]
        m_sc[...] = m0_ref[...]

    Lc = q_ref.shape[0]
    dqk = q_ref.shape[1] // heads
    dv = v_ref.shape[1] // heads
    G = g_ref[...]
    GT = gt_ref[...]
    r = lax.broadcasted_iota(jnp.int32, (Lc, Lc), 0)
    c = lax.broadcasted_iota(jnp.int32, (Lc, Lc), 1)
    mask = (c >= r) if rev else (c <= r)
    tri = jnp.where(mask, 1.0, 0.0)
    Bc = jnp.dot(tri, G, precision=HI, preferred_element_type=F32)
    BT = lax.dot_general(GT, tri, (((1,), (1,)), ((), ())), precision=HI, preferred_element_type=F32)
    tot = Bc[0:1, :] if rev else Bc[Lc - 1:Lc, :]
    base = 2 * heads if rev else 0
    for h in range(heads):
        ci_, cf_ = base + h, base + heads + h
        b_col = Bc[:, cf_:cf_ + 1]
        i_col = G[:, ci_:ci_ + 1]
        bT = BT[cf_:cf_ + 1, :]
        iT = GT[ci_:ci_ + 1, :]
        b_last = tot[:, cf_:cf_ + 1]
        m = m_sc[h][:, 0:1]
        logd = jnp.where(mask, b_col - bT + iT, -jnp.inf)
        inter = b_col + m
        mj = jnp.maximum(inter, jnp.max(logd, axis=-1, keepdims=True))
        dmat = jnp.exp(logd - mj)
        qh = q_ref[:, h * dqk:(h + 1) * dqk]
        kh = k_ref[:, h * dqk:(h + 1) * dqk]
        vh = v_ref[:, h * dv:(h + 1) * dv]
        s = lax.dot_general(qh, kh, (((1,), (1,)), ((), ())), preferred_element_type=F32) * dmat
        w_inter = jnp.exp(inter - mj)
        Ct = C_sc[h]
        nrow = n_sc[h]
        num = (jnp.dot(s.astype(BF16), vh, preferred_element_type=F32)
               + w_inter * jnp.dot(qh, Ct.astype(BF16), preferred_element_type=F32))
        den = (jnp.sum(s, axis=-1, keepdims=True)
               + w_inter * jnp.sum(qh.astype(F32) * nrow, axis=-1, keepdims=True))
        hout = num / jnp.maximum(jnp.abs(den), jnp.exp(-mj))
        g_col = b_last - b_col + i_col
        g_row = b_last - bT + iT
        m_new = jnp.maximum(b_last + m, jnp.max(g_row, axis=-1, keepdims=True))
        decay = jnp.exp(b_last + m - m_new)
        wg = jnp.exp(g_col - m_new)
        vw = (vh.astype(F32) * wg).astype(BF16)
        C_sc[h] = decay * Ct + lax.dot_general(kh, vw, (((0,), (0,)), ((), ())), preferred_element_type=F32)
        n_sc[h] = decay * nrow + jnp.sum(kh.astype(F32) * wg, axis=0, keepdims=True)
        m_sc[h] = jnp.broadcast_to(m_new, m_sc.shape[1:])
        cs = slice(h * dv, (h + 1) * dv)
        if final:
            hs = hf_ref[:, cs].astype(F32) + hout
            mu = jnp.mean(hs, axis=-1, keepdims=True)
            d = hs - mu
            var = jnp.mean(d * d, axis=-1, keepdims=True)
            hn = d * lax.rsqrt(var + EPS)
            y = (hn * ng_ref[:, cs] + sk_ref[:, cs] * xc_ref[:, cs].astype(F32)) * _sigmoid(o_ref[:, cs].astype(F32))
            out_ref[:, cs] = y.astype(out_ref.dtype)
        else:
            out_ref[:, cs] = hout.astype(out_ref.dtype)

    @pl.when(ci == pl.num_programs(0) - 1)
    def _():
        cT_ref[...] = C_sc[...]
        nT_ref[...] = n_sc[...]
        mT_ref[...] = m_sc[...]


def _mlstm_scan(q, k, v, g, gt, state, rev, final_args=None):
    L = q.shape[0]
    heads = M_HEADS
    dqk = q.shape[1] // heads
    dv = v.shape[1] // heads
    Lc = min(256, L)
    nc = L // Lc
    idx = (lambda i: (nc - 1 - i, 0)) if rev else (lambda i: (i, 0))
    idx_t = (lambda i: (0, nc - 1 - i)) if rev else (lambda i: (0, i))
    rows = lambda c, cb=0: pl.BlockSpec((Lc, c), lambda i: (idx(i)[0], cb))
    c0, n0, m0 = state
    final = final_args is not None
    in_specs = [rows(heads * dqk), rows(heads * dqk), rows(heads * dv), rows(LANES),
                pl.BlockSpec((gt.shape[0], Lc), idx_t), _full(c0.shape), _full(n0.shape), _full(m0.shape)]
    args = [q, k, v, g, gt, c0, n0, m0]
    if final:
        hf, xc, xo, norm_g, skip = final_args
        inner = heads * dv
        in_specs += [rows(inner), rows(inner), rows(inner, 1), _full((1, inner)), _full((1, inner))]
        args += [hf, xc, xo, norm_g.reshape(1, inner), skip.reshape(1, inner)]
    out_dtype = BF16
    return pl.pallas_call(
        functools.partial(_scan_kernel, heads=heads, rev=rev, final=final),
        grid=(nc,),
        in_specs=in_specs,
        out_specs=[rows(heads * dv), _full(c0.shape), _full(n0.shape), _full(m0.shape)],
        out_shape=[jax.ShapeDtypeStruct((L, heads * dv), out_dtype), jax.ShapeDtypeStruct(c0.shape, F32),
                   jax.ShapeDtypeStruct(n0.shape, F32), jax.ShapeDtypeStruct(m0.shape, F32)],
        scratch_shapes=[pltpu.VMEM(c0.shape, F32), pltpu.VMEM(n0.shape, F32), pltpu.VMEM(m0.shape, F32)],
        compiler_params=_cparams("arbitrary"),
        name="mlstm_scan_bwd" if rev else "mlstm_scan_fwd",
    )(*args)


def _mlstm_stream(xo, mp, st_f, st_b):
    (conv_w, conv_b, q_w, k_w, v_w, gate_w, gate_b, norm_g, skip) = mp
    xc, q, k, v, g, gt = _mlstm_pre(xo, conv_w, conv_b, q_w, k_w, v_w, gate_w, gate_b)
    hf, *sf = _mlstm_scan(q, k, v, g, gt, st_f, rev=False)
    y, *sb = _mlstm_scan(q, k, v, g, gt, st_b, rev=True, final_args=(hf, xc, xo, norm_g, skip))
    return y, tuple(sf), tuple(sb)


def _sconv_kernel(x_ref, p_ref, n_ref, w_ref, b_ref, o_ref):
    _cg_put(o_ref, _conv3(x_ref[...].astype(F32), p_ref, n_ref, w_ref[...], b_ref[...]))


def _sconv(zp, w, b):
    L, C = zp.shape
    tm = min(1024, L)
    tc = C // 3
    hb, nb = tm // HALO, L // HALO
    return pl.pallas_call(
        _sconv_kernel,
        grid=(L // tm, C // tc),
        in_specs=[pl.BlockSpec((tm, tc), lambda i, j: (i, j)),
                  pl.BlockSpec((HALO, tc), lambda i, j: (jnp.maximum(i * hb - 1, 0), j)),
                  pl.BlockSpec((HALO, tc), lambda i, j: (jnp.minimum((i + 1) * hb, nb - 1), j)),
                  pl.BlockSpec((3, tc), lambda i, j: (0, j)), pl.BlockSpec((1, tc), lambda i, j: (0, j))],
        out_specs=pl.BlockSpec((tc // LANES, tm, LANES), lambda i, j: (j, i, 0)),
        out_shape=jax.ShapeDtypeStruct((C // LANES, L, LANES), F32),
        compiler_params=_cparams("parallel", "parallel"),
        name="hyena_sconv",
    )(zp, zp, zp, w, b.reshape(1, C))


def _filt_kernel(w1_ref, b1_ref, fr_ref, w2_ref, b2_ref, w3_ref, dec_ref, f_ref, ss_ref, *, L):
    i = pl.program_id(0)
    tm = f_ref.shape[1]
    emb = w1_ref.shape[0]
    pos = (lax.broadcasted_iota(jnp.int32, (tm, LANES), 0) + i * tm).astype(F32)
    lane = lax.broadcasted_iota(jnp.int32, (tm, LANES), 1)
    band = jnp.where(lane <= H_BANDS, lane, lane - H_BANDS).astype(F32)
    ang = (2.0 * math.pi / L) * pos * band
    quarter = jnp.where(lane <= H_BANDS, 0.5 * math.pi, 0.0)
    feat = jnp.where(lane == 0, pos / max(L - 1, 1), jnp.where(lane < emb, jnp.sin(ang + quarter), 0.0))
    fr = fr_ref[...]
    z = jnp.sin(fr * (jnp.dot(feat, w1_ref[...], precision=HI, preferred_element_type=F32) + b1_ref[...]))
    z = jnp.sin(fr * (jnp.dot(z, w2_ref[...], precision=HI, preferred_element_type=F32) + b2_ref[...]))
    z = jnp.dot(z.astype(BF16), w3_ref[...].astype(BF16), preferred_element_type=F32)
    tau = jnp.abs(pos[:, 0:1] - (L // 2)) / (L / 2)
    filt = z * jnp.exp(-tau * dec_ref[...])
    _cg_put(f_ref, filt)

    @pl.when(i == 0)
    def _():
        ss_ref[...] = jnp.zeros_like(ss_ref)

    ss_ref[...] += jnp.sum(filt * filt, axis=0, keepdims=True)


def _hyena_filters(L, w1, b1, freq, w2, b2, w3, decay):
    emb, fh = w1.shape
    C2 = w3.shape[1]
    tm = min(256, L)
    w1p = jnp.zeros((LANES, fh), F32).at[:emb].set(w1)
    return pl.pallas_call(
        functools.partial(_filt_kernel, L=L),
        grid=(L // tm,),
        in_specs=[_full((LANES, fh)), _full((1, fh)), _full((1, fh)), _full((fh, fh)), _full((1, fh)),
                  _full((fh, C2)), _full((1, C2))],
        out_specs=[pl.BlockSpec((C2 // LANES, tm, LANES), lambda i: (0, i, 0)), _full((1, C2))],
        out_shape=[jax.ShapeDtypeStruct((C2 // LANES, L, LANES), F32), jax.ShapeDtypeStruct((1, C2), F32)],
        compiler_params=_cparams("arbitrary"),
        name="hyena_filters",
    )(w1p, b1.reshape(1, fh), freq.reshape(1, fh), w2, b2.reshape(1, fh), w3, decay.reshape(1, C2))


def _dft_tables(L):
    N = 2 * L
    N2 = FFT_N2
    N1 = N // N2
    k1 = jnp.arange(N1, dtype=jnp.int32)
    n1 = jnp.arange(N1 // 2, dtype=jnp.int32)
    n2 = jnp.arange(N2, dtype=jnp.int32)
    a = ((k1[:, None] * n1[None, :]) % N1).astype(F32) * (2.0 * math.pi / N1)
    b = ((n2[:, None] * k1[None, :]) % N).astype(F32) * (2.0 * math.pi / N)
    cos_a, sin_a, cos_b, sin_b = jnp.cos(a)[None], jnp.sin(a)[None], jnp.cos(b)[:, :, None], jnp.sin(b)[:, :, None]
    ca = (cos_b * cos_a - sin_b * sin_a).astype(BF16)
    sa = (sin_b * cos_a + cos_b * sin_a).astype(BF16)
    cd, sd = jnp.swapaxes(ca, 1, 2), jnp.swapaxes(sa, 1, 2)
    ph2 = (n2[:, None] * n2[None, :]) % N2
    ang2 = ph2.astype(F32) * (2.0 * math.pi / N2)
    c2, s2 = jnp.cos(ang2), jnp.sin(ang2)
    fwd = jnp.block([[c2, s2], [-s2, c2]]).astype(BF16)
    inv = jnp.block([[c2, -s2], [s2, c2]]).astype(BF16)
    return (ca, sa), (cd, sd), (fwd, inv)


def _fft_a_kernel(x_ref, c_ref, s_ref, a_ref):
    for r in range(SUB):
        x = _cg_load(x_ref, r).astype(BF16)
        re = jnp.dot(c_ref[r], x, preferred_element_type=F32)
        im = -jnp.dot(s_ref[r], x, preferred_element_type=F32)
        _cg_store(a_ref, r, _pack(re, im))


def _fft_a(x, gc, gblock, nblocks, tabs):
    ca, sa = tabs
    N2, N1, H1 = ca.shape
    x4 = x.reshape(x.shape[0], H1, N2, LANES)
    tspec = pl.BlockSpec((SUB, N1, H1), lambda j, cb: (j, 0, 0))
    return pl.pallas_call(
        _fft_a_kernel,
        grid=(N2 // SUB, nblocks),
        in_specs=[pl.BlockSpec((gc, H1, SUB, LANES), lambda j, cb: (gblock + cb, 0, j, 0)), tspec, tspec],
        out_specs=pl.BlockSpec((gc, N1, SUB, LANES), lambda j, cb: (cb, 0, j, 0)),
        out_shape=jax.ShapeDtypeStruct((gc * nblocks, N1, N2, LANES), U32),
        compiler_params=_cparams("parallel", "parallel"),
        name="fft_stage_a",
    )(x4, ca, sa)


def _cdft(t_ref, xr, xi):
    n = xr.shape[0]
    out = jnp.dot(t_ref[...], jnp.concatenate([xr, xi], axis=0), preferred_element_type=F32)
    return out[:n], out[n:]


def _fft_h_kernel(a_ref, f_ref, sc_ref, h_ref):
    for t in range(FFT_KB):
        xr, xi = _cdft(f_ref, *_unpack(_cg_cat(a_ref.at[:, t])))
        k1 = pl.program_id(0) * FFT_KB + t
        sc = sc_ref[...] * jnp.where((k1 & 2) == 0, 1.0, -1.0)
        if t % 2 == 0:
            _cg_put(h_ref.at[:, t], _pack(xr * sc, xi * sc))
        else:
            _cg_put(h_ref.at[:, t], _pack(xi * (-sc), xr * sc))


def _fft_h(a, gc, tab2, scale):
    G, N1, N2, _ = a.shape
    blk = pl.BlockSpec((gc, FFT_KB, N2, LANES), lambda j, cb: (cb, j, 0, 0))
    return pl.pallas_call(
        _fft_h_kernel,
        grid=(N1 // FFT_KB, G // gc),
        in_specs=[blk, _full(tab2[0].shape), pl.BlockSpec((1, gc * LANES), lambda j, cb: (0, cb))],
        out_specs=blk,
        out_shape=jax.ShapeDtypeStruct(a.shape, U32),
        compiler_params=_cparams("parallel", "parallel"),
        name="fft_filter_spectrum",
    )(a, tab2[0], scale)


def _fft_b_kernel(a_ref, h_ref, f_ref, i_ref, p_ref):
    for t in range(FFT_KB):
        xr, xi = _cdft(f_ref, *_unpack(_cg_cat(a_ref.at[:, t])))
        hr, hi = _unpack(_cg_cat(h_ref.at[:, t]))
        hr, hi = hr.astype(F32), hi.astype(F32)
        yr = (xr * hr - xi * hi).astype(BF16)
        yi = (xr * hi + xi * hr).astype(BF16)
        _cg_put(p_ref.at[:, t], _pack(*_cdft(i_ref, yr, yi)))


def _fft_b(a, h, hblock, tab2):
    G, N1, N2, _ = a.shape
    blk = pl.BlockSpec((G, FFT_KB, N2, LANES), lambda j: (0, j, 0, 0))
    hblk = pl.BlockSpec((G, FFT_KB, N2, LANES), lambda j: (hblock, j, 0, 0))
    return pl.pallas_call(
        _fft_b_kernel,
        grid=(N1 // FFT_KB,),
        in_specs=[blk, hblk, _full(tab2[0].shape), _full(tab2[1].shape)],
        out_specs=blk,
        out_shape=jax.ShapeDtypeStruct(a.shape, U32),
        compiler_params=_cparams("parallel"),
        name="fft_stage_b",
    )(a, h, tab2[0], tab2[1])


def _fft_d_kernel(p_ref, c_ref, s_ref, g_ref, v_ref, fb_ref, o_ref):
    for r in range(SUB):
        pr, pi = _unpack(_cg_load(p_ref, r))
        y = (jnp.dot(c_ref[r], pr, preferred_element_type=F32)
             - jnp.dot(s_ref[r], pi, preferred_element_type=F32))
        v = _cg_load(v_ref, r)
        _cg_store(o_ref, r, _cg_load(g_ref, r) * (y + v * fb_ref[...]))


def _fft_d(p, tabs, gsrc, gblock, vsrc, vblock, fbias):
    cd, sd = tabs
    N2, H1, N1 = cd.shape
    G = p.shape[0]
    L = H1 * N2
    tspec = pl.BlockSpec((SUB, H1, N1), lambda j: (j, 0, 0))
    sig = lambda gb: pl.BlockSpec((G, H1, SUB, LANES), lambda j: (gb, 0, j, 0))
    out = pl.pallas_call(
        _fft_d_kernel,
        grid=(N2 // SUB,),
        in_specs=[pl.BlockSpec((G, N1, SUB, LANES), lambda j: (0, 0, j, 0)), tspec, tspec, sig(gblock), sig(vblock),
                  _full((1, G * LANES))],
        out_specs=sig(0),
        out_shape=jax.ShapeDtypeStruct((G, H1, N2, LANES), F32),
        compiler_params=_cparams("parallel"),
        name="fft_stage_d",
    )(p, cd, sd, gsrc.reshape(gsrc.shape[0], H1, N2, LANES), vsrc.reshape(vsrc.shape[0], H1, N2, LANES), fbias)
    return out.reshape(G, L, LANES)


def _hyena_long(z, filt, ssq, fbias, tables):
    L = z.shape[1]
    gc = z.shape[0] // 3
    C = gc * LANES
    N = 2 * L
    taba, tabd, tab2 = tables
    scale = lax.rsqrt(ssq + H_DECAY_EPS) * (1.0 / N)
    h = _fft_h(_fft_a(filt, gc, 0, 2, taba), gc, tab2, scale)
    fb = fbias.reshape(1, 2 * C)
    p = _fft_b(_fft_a(z, gc, 2, 1, taba), h, 0, tab2)
    y1 = _fft_d(p, tabd, z, 0, z, 2, fb[:, :C])
    p = _fft_b(_fft_a(y1, gc, 0, 1, taba), h, 1, tab2)
    return _fft_d(p, tabd, z, 1, y1, 0, fb[:, C:])


def _short_conv_kernel(z_ref, f_ref, ss_ref, fb_ref, cf_ref, sf_ref, cd_ref, sd_ref, o_ref):
    L = z_ref.shape[1]
    G = o_ref.shape[0]
    C = G * LANES
    N = 2 * L

    def dft(xb):
        return (jnp.dot(cf_ref[...], xb, preferred_element_type=F32),
                -jnp.dot(sf_ref[...], xb, preferred_element_type=F32))

    def conv(u, blk, gate):
        sl = slice(blk * C, (blk + 1) * C)
        ur, ui = dft(u.astype(BF16))
        fr, fi = dft(_cg_cat(f_ref.at[blk * G:(blk + 1) * G]).astype(BF16))
        q = lax.broadcasted_iota(jnp.int32, fr.shape, 0) & 3
        fr, fi = (jnp.where(q == 0, fr, jnp.where(q == 1, -fi, jnp.where(q == 2, -fr, fi))),
                  jnp.where(q == 0, fi, jnp.where(q == 1, fr, jnp.where(q == 2, -fi, -fr))))
        yr = (ur * fr - ui * fi).astype(BF16)
        yi = (ur * fi + ui * fr).astype(BF16)
        y = (jnp.dot(cd_ref[...], yr, preferred_element_type=F32)
             - jnp.dot(sd_ref[...], yi, preferred_element_type=F32))
        y = y * (lax.rsqrt(ss_ref[:, sl] + H_DECAY_EPS) * (1.0 / N))
        return gate * (y + u * fb_ref[:, sl])

    x1 = _cg_cat(z_ref.at[0:G])
    x2 = _cg_cat(z_ref.at[G:2 * G])
    v = _cg_cat(z_ref.at[2 * G:3 * G])
    _cg_put(o_ref, conv(conv(v, 0, x1), 1, x2))


def _hyena_short(z, filt, ssq, fbias):
    L = z.shape[1]
    G = z.shape[0] // 3
    C = G * LANES
    N = 2 * L
    k = jnp.arange(N, dtype=jnp.int32)
    n = jnp.arange(L, dtype=jnp.int32)
    ang = ((k[:, None] * n[None, :]) % N).astype(F32) * (2.0 * math.pi / N)
    cf, sf = jnp.cos(ang).astype(BF16), jnp.sin(ang).astype(BF16)
    args = (z, filt, ssq, fbias.reshape(1, 2 * C), cf, sf, cf.T, sf.T)
    return pl.pallas_call(
        _short_conv_kernel,
        grid=(1,),
        in_specs=[_full(a.shape) for a in args],
        out_specs=_full((G, L, LANES)),
        out_shape=jax.ShapeDtypeStruct((G, L, LANES), F32),
        compiler_params=_cparams("arbitrary"),
        name="hyena_short_conv",
    )(*args)


def _router_kernel(x_ref, g_ref, mod_ref, rw_ref, rb_ref, c0_ref, xt_ref, rt_ref, cnt_ref, carry):
    i = pl.program_id(0)

    @pl.when(i == 0)
    def _():
        carry[...] = c0_ref[...]

    u = _norm_mod(x_ref[...], g_ref[...], mod_ref[...], 3)
    _tt_store(xt_ref, u)
    u_hi = u.astype(BF16)
    u_lo = (u - u_hi.astype(F32)).astype(BF16)
    logits = jnp.dot(jnp.concatenate([u_hi, u_hi, u_lo], axis=1), rw_ref[...], preferred_element_type=F32) + rb_ref[...]
    tm = u.shape[0]
    lane = lax.broadcasted_iota(jnp.int32, logits.shape, 1).astype(F32)
    ninf = -jnp.inf

    def top(vals):
        mx = jnp.max(vals, axis=-1, keepdims=True)
        ix = jnp.min(jnp.where(vals == mx, lane, float(LANES)), axis=-1, keepdims=True)
        return mx, ix

    lg = jnp.where(lane < N_GROUPS, logits, ninf)
    gmax, gidx = top(lg)
    g_w = 1.0 / jnp.sum(jnp.exp(lg - gmax), axis=-1, keepdims=True)
    lo = N_GROUPS + EXP_PER_GROUP * gidx
    le = jnp.where((lane >= lo) & (lane < lo + EXP_PER_GROUP), logits, ninf)
    m1, i1 = top(le)
    m2, i2 = top(jnp.where(lane == i1, ninf, le))
    r = jnp.exp(m2 - m1)
    w1 = g_w / (1.0 + r)
    w2 = w1 * r
    e1 = i1 - N_GROUPS
    e2 = i2 - N_GROUPS
    oh1 = lane == e1
    oh2 = lane == e2
    both = jnp.where(oh1 | oh2, 1.0, 0.0)
    rr = lax.broadcasted_iota(jnp.int32, (tm, tm), 0)
    cc = lax.broadcasted_iota(jnp.int32, (tm, tm), 1)
    earlier = jnp.where(cc < rr, 1.0, 0.0).astype(BF16)
    before = jnp.dot(earlier, both.astype(BF16), preferred_element_type=F32) + carry[...]
    rank1 = jnp.sum(jnp.where(oh1, before, 0.0), axis=-1, keepdims=True)
    rank2 = jnp.sum(jnp.where(oh2, before, 0.0), axis=-1, keepdims=True)
    carry[...] += jnp.sum(both, axis=0, keepdims=True)
    cnt_ref[...] = carry[...]
    out = jnp.zeros(logits.shape, F32)
    for j, col in enumerate((e1, e2, w1, w2, rank1, rank2)):
        out = jnp.where(lane == j, col, out)
    rt_ref[...] = out


def _router(h, g, mod, rw, rb, counts0):
    L, D = h.shape
    tm = min(256, L)
    return pl.pallas_call(
        _router_kernel,
        grid=(L // tm,),
        in_specs=[pl.BlockSpec((tm, D), lambda i: (i, 0)), _full((1, D)), _full(mod.shape),
                  _full(rw.shape), _full(rb.shape), _full((1, LANES))],
        out_specs=[pl.BlockSpec((tm, D // LANES, LANES), lambda i: (i, 0, 0)),
                   pl.BlockSpec((tm, LANES), lambda i: (i, 0)), _full((1, LANES))],
        out_shape=[jax.ShapeDtypeStruct((L, D // LANES, LANES), F32), jax.ShapeDtypeStruct((L, LANES), F32),
                   jax.ShapeDtypeStruct((1, LANES), F32)],
        scratch_shapes=[pltpu.VMEM((1, LANES), F32)],
        compiler_params=_cparams("arbitrary"),
        name="moe_router",
    )(h, g.reshape(1, D), mod, rw, rb, counts0)


def _dispatch_kernel(dest_ref, *refs):
    *x_refs, xb_in_ref, xb_ref, sem = refs
    del xb_in_ref
    tm = x_refs[0].shape[0]
    i = pl.program_id(0)
    a0 = 2 * tm * i

    def copy_rows(x_ref):
        def issue(r, carry):
            for k in range(2):
                pltpu.make_async_copy(x_ref.at[r], xb_ref.at[dest_ref[a0 + 2 * r + k]], sem).start()
            return carry

        def wait(r, carry):
            pltpu.make_async_copy(x_ref.at[0], xb_ref.at[0], sem).wait()
            return carry

        lax.fori_loop(0, tm, issue, 0, unroll=DMA_UNROLL)
        lax.fori_loop(0, 2 * tm, wait, 0, unroll=DMA_UNROLL)

    if len(x_refs) == 1:
        copy_rows(x_refs[0])
    else:
        pl.when(i == 0)(lambda: copy_rows(x_refs[0]))
        pl.when(i > 0)(lambda: copy_rows(x_refs[1]))


def _dispatch(xts, dest, xb):
    G = xts[0].shape[1]
    tm = min(256, xts[-1].shape[0])
    blk = (tm, G, LANES)
    if len(xts) == 2:
        assert xts[0].shape[0] == tm, "context stream must be exactly one tile"
        x_specs = [pl.BlockSpec(blk, lambda i, d: (0, 0, 0)),
                   pl.BlockSpec(blk, lambda i, d: (jnp.maximum(i - 1, 0), 0, 0))]
    else:
        x_specs = [pl.BlockSpec(blk, lambda i, d: (i, 0, 0))]
    T = sum(x.shape[0] for x in xts)
    return pl.pallas_call(
        _dispatch_kernel,
        grid_spec=pltpu.PrefetchScalarGridSpec(
            num_scalar_prefetch=1,
            grid=(T // tm,),
            in_specs=x_specs + [pl.BlockSpec(memory_space=pl.ANY)],
            out_specs=pl.BlockSpec(memory_space=pl.ANY),
            scratch_shapes=[pltpu.SemaphoreType.DMA(())]),
        out_shape=jax.ShapeDtypeStruct(xb.shape, F32),
        input_output_aliases={1 + len(xts): 0},
        compiler_params=_cparams("arbitrary"),
        name="moe_dispatch",
    )(dest, *xts, xb)


def _expert_kernel(be_ref, nb_ref, x_ref, w13_ref, w2_ref, o_ref, w13b, w2b):
    b = pl.program_id(0)
    live = b < nb_ref[0]
    new_expert = (b == 0) | (be_ref[b] != be_ref[jnp.maximum(b - 1, 0)])

    @pl.when(jnp.logical_not(live))
    def _():
        o_ref[...] = jnp.zeros_like(o_ref)

    @pl.when(live & new_expert)
    def _():
        w13b[...] = w13_ref[0].astype(BF16)
        w2b[...] = w2_ref[0].astype(BF16)

    @pl.when(live)
    def _():
        de = w2b.shape[0]
        x = _tt_load(x_ref).astype(BF16)
        hcat = jnp.dot(x, w13b[...], preferred_element_type=F32)
        a = (_silu(hcat[:, :de]) * hcat[:, de:]).astype(BF16)
        _tt_store(o_ref, jnp.dot(a, w2b[...], preferred_element_type=F32))


def _experts(xb, blk_exp, n_used, w13, w2):
    R, G, _ = xb.shape
    D = G * LANES
    de = w2.shape[1]
    nb = R // MOE_BLOCK
    blk = pl.BlockSpec((MOE_BLOCK, G, LANES), lambda b, be, nu: (b, 0, 0))
    return pl.pallas_call(
        _expert_kernel,
        grid_spec=pltpu.PrefetchScalarGridSpec(
            num_scalar_prefetch=2,
            grid=(nb,),
            in_specs=[blk,
                      pl.BlockSpec((1, D, 2 * de), lambda b, be, nu: (be[b], 0, 0)),
                      pl.BlockSpec((1, de, D), lambda b, be, nu: (be[b], 0, 0))],
            out_specs=blk,
            scratch_shapes=[pltpu.VMEM((D, 2 * de), BF16), pltpu.VMEM((de, D), BF16)]),
        out_shape=jax.ShapeDtypeStruct(xb.shape, F32),
        compiler_params=_cparams("arbitrary"),
        name="moe_experts",
    )(blk_exp, n_used, xb, w13, w2)


def _combine_kernel(dest_ref, h_ref, rt_ref, mod_ref, fg_ref, yb_ref, o_ref, buf, sem, *, a_base, final):
    tm = h_ref.shape[0]
    i = pl.program_id(0)
    slot = i % 2

    def fetch(tile, sl):
        a0 = a_base + 2 * tm * tile

        def issue(r, carry):
            for k in range(2):
                pltpu.make_async_copy(yb_ref.at[dest_ref[a0 + 2 * r + k]], buf.at[sl, k, r], sem.at[sl]).start()
            return carry

        lax.fori_loop(0, tm, issue, 0, unroll=DMA_UNROLL)

    @pl.when(i == 0)
    def _():
        fetch(0, 0)

    @pl.when(i + 1 < pl.num_programs(0))
    def _():
        fetch(i + 1, 1 - slot)

    def wait(r, carry):
        pltpu.make_async_copy(yb_ref.at[0], buf.at[slot, 0, 0], sem.at[slot]).wait()
        return carry

    lax.fori_loop(0, 2 * tm, wait, 0, unroll=DMA_UNROLL)
    rt = rt_ref[...]
    y = rt[:, 2:3] * _tt_load(buf.at[slot, 0]) + rt[:, 3:4] * _tt_load(buf.at[slot, 1])
    out = h_ref[...] + mod_ref[5:6, :] * y
    if final:
        out = out * lax.rsqrt(jnp.mean(out * out, axis=-1, keepdims=True) + EPS) * fg_ref[...]
    o_ref[...] = out


def _combine(h, yb, dest, row0, route, mod, final_g=None):
    L, D = h.shape
    final = final_g is not None
    fg = (final_g if final else jnp.ones((D,), F32)).reshape(1, D)
    tm = min(256, L)
    return pl.pallas_call(
        functools.partial(_combine_kernel, a_base=2 * row0, final=final),
        grid_spec=pltpu.PrefetchScalarGridSpec(
            num_scalar_prefetch=1,
            grid=(L // tm,),
            in_specs=[pl.BlockSpec((tm, D), lambda i, d: (i, 0)), pl.BlockSpec((tm, LANES), lambda i, d: (i, 0)),
                      pl.BlockSpec(mod.shape, lambda i, d: (0, 0)), pl.BlockSpec((1, D), lambda i, d: (0, 0)),
                      pl.BlockSpec(memory_space=pl.ANY)],
            out_specs=pl.BlockSpec((tm, D), lambda i, d: (i, 0)),
            scratch_shapes=[pltpu.VMEM((2, 2, tm, D // LANES, LANES), F32), pltpu.SemaphoreType.DMA((2,))]),
        out_shape=jax.ShapeDtypeStruct((L, D), F32),
        compiler_params=_cparams("arbitrary"),
        name="moe_combine",
    )(dest, h, route, mod, fg, yb)


def _moe(streams, g, rw, rb, w13, w2, layer, xb, final_g=None):
    counts = jnp.zeros((1, LANES), F32)
    xts, routes = [], []
    for h, mod in streams:
        xt, rt, counts = _router(h, g, mod, rw, rb, counts)
        xts.append(xt)
        routes.append(rt)
    route = jnp.concatenate(routes, axis=0) if len(routes) > 1 else routes[0]
    T = route.shape[0]
    A = 2 * T
    cnt = counts[0, :N_EXPERTS].astype(jnp.int32)
    padded = (cnt + MOE_BLOCK - 1) // MOE_BLOCK * MOE_BLOCK
    p_ends = jnp.cumsum(padded)
    p_starts = p_ends - padded
    experts = route[:, 0:2].astype(jnp.int32)
    onehot = experts[:, :, None] == jnp.arange(N_EXPERTS, dtype=jnp.int32)
    dest = (jnp.sum(jnp.where(onehot, p_starts, 0), axis=-1) + route[:, 4:6].astype(jnp.int32)).reshape(A)
    n_blocks = xb.shape[0] // MOE_BLOCK
    assert n_blocks >= -(-A // MOE_BLOCK) + N_EXPERTS
    blk_start = jnp.arange(n_blocks, dtype=jnp.int32) * MOE_BLOCK
    blk_exp = jnp.minimum(jnp.sum(p_ends[None, :] <= blk_start[:, None], axis=1), N_EXPERTS - 1).astype(jnp.int32)
    blk_exp = blk_exp + layer * N_EXPERTS
    n_used = (p_ends[-1:] // MOE_BLOCK).astype(jnp.int32)
    xb = _dispatch(xts, dest, xb)
    yb = _experts(xb, blk_exp, n_used, w13, w2)
    outs, row0 = [], 0
    for j, ((h, mod), rt) in enumerate(zip(streams, routes)):
        fg = final_g if j == len(streams) - 1 else None
        outs.append(_combine(h, yb, dest, row0, rt, mod, fg))
        row0 += h.shape[0]
    return outs, xb


def _zero_state(heads, dqk, dv):
    return (jnp.zeros((heads, dqk, dv), F32), jnp.zeros((heads, 1, dqk), F32), jnp.zeros((heads, 1, LANES), F32))


def kernel(x, c, ctx, c_ctx, ada_w, ada_b, norm1_g, norm2_g, rg_w, rg_b, re_w, re_b, moe_w13, moe_w2, m_in_w, m_conv_w, m_conv_b, m_q_w, m_k_w, m_v_w, m_gate_w, m_gate_b, m_norm_g, m_skip, m_out_w, h_in_w, h_in_b, h_sc_w, h_sc_b, h_f_w1, h_f_b1, h_f_freq, h_f_w2, h_f_b2, h_f_w3, h_decay, h_fbias, h_out_w, h_out_b, final_g):
    B, S, D = x.shape
    assert B == 1, "kernel is written for the single-sequence problem shape"
    depth = ada_w.shape[0]
    hl, hc = x[0], ctx[0]
    cc = jnp.zeros((8, D), F32).at[0].set(c[0]).at[1].set(c_ctx)
    mods = _ada(cc, ada_w, ada_b).reshape(depth, 8, 6, D)
    pad8 = lambda m: jnp.concatenate([m, jnp.zeros((2, D), F32)], axis=0)
    nrt = N_GROUPS + N_EXPERTS
    zeros_d = jnp.zeros((D,), F32)
    tables = _dft_tables(S) if depth > 1 else None
    w13_all = moe_w13.reshape((-1,) + moe_w13.shape[2:])
    w2_all = moe_w2.reshape((-1,) + moe_w2.shape[2:])
    n_rows = (-(-2 * (S + hc.shape[0]) // MOE_BLOCK) + N_EXPERTS) * MOE_BLOCK
    xb = jnp.zeros((n_rows, D // LANES, LANES), F32)

    for i in range(depth):
        kind, slot = i % N_MIXERS, i // N_MIXERS
        col_major = (slot % 2) == 1
        ctx_live = any(j % N_MIXERS == 0 for j in range(i + 1, depth))
        ml, mc = pad8(mods[i, 0]), pad8(mods[i, 1])

        if kind == 0:
            w_in = m_in_w[slot].astype(BF16)
            w_out = m_out_w[slot].astype(BF16)
            inner = w_out.shape[0]
            mp = (m_conv_w[slot], m_conv_b[slot], m_q_w[slot], m_k_w[slot], m_v_w[slot], m_gate_w[slot],
                  m_gate_b[slot], m_norm_g[slot], m_skip[slot])
            heads, _, dqk = m_q_w[slot].shape
            dv = m_v_w[slot].shape[2]
            zb = jnp.zeros((w_in.shape[1],), F32)
            xo_c = _norm_proj(hc, norm1_g[i], mc, w_in, zb, False)
            yc, st_f, st_b = _mlstm_stream(xo_c, mp, _zero_state(heads, dqk, dv), _zero_state(heads, dqk, dv))
            xo_l = _norm_proj(hl, norm1_g[i], ml, w_in, zb, col_major)
            yl, _, _ = _mlstm_stream(xo_l, mp, st_f, st_b)
            hl = _out_proj(yl, w_out, zeros_d, hl, ml, 2, col_major)
            if ctx_live:
                hc = _out_proj(yc, w_out, zeros_d, hc, mc, 2, False)
        else:
            w_in = h_in_w[slot].astype(BF16)
            w_out = h_out_w[slot].astype(BF16)
            fp = (h_f_w1[slot], h_f_b1[slot], h_f_freq[slot], h_f_w2[slot], h_f_b2[slot], h_f_w3[slot], h_decay[slot])
            zl = _sconv(_norm_proj(hl, norm1_g[i], ml, w_in, h_in_b[slot], col_major), h_sc_w[slot], h_sc_b[slot])
            filt, ssq = _hyena_filters(S, *fp)
            yl = _hyena_long(zl, filt, ssq, h_fbias[slot], tables)
            hl = _out_proj(yl, w_out, h_out_b[slot], hl, ml, 2, col_major)
            if ctx_live:
                zc = _sconv(_norm_proj(hc, norm1_g[i], mc, w_in, h_in_b[slot], False), h_sc_w[slot], h_sc_b[slot])
                filt_c, ssq_c = _hyena_filters(hc.shape[0], *fp)
                yc = _hyena_short(zc, filt_c, ssq_c, h_fbias[slot])
                hc = _out_proj(yc, w_out, h_out_b[slot], hc, mc, 2, False)

        rw = jnp.zeros((D, LANES), F32).at[:, :N_GROUPS].set(rg_w[i]).at[:, N_GROUPS:nrt].set(re_w[i])
        rw_hi = rw.astype(BF16)
        rw_lo = (rw - rw_hi.astype(F32)).astype(BF16)
        rw = jnp.concatenate([rw_hi, rw_lo, rw_hi], axis=0)
        rb = jnp.zeros((1, LANES), F32).at[0, :N_GROUPS].set(rg_b[i]).at[0, N_GROUPS:nrt].set(re_b[i])
        fg = final_g if i == depth - 1 else None
        if ctx_live:
            (hc, hl), xb = _moe([(hc, mc), (hl, ml)], norm2_g[i], rw, rb, w13_all, w2_all, i, xb, fg)
        else:
            (hl,), xb = _moe([(hl, ml)], norm2_g[i], rw, rb, w13_all, w2_all, i, xb, fg)

    return hl[None]
```

```python
import functools
import math

import jax
import jax.numpy as jnp
from jax import lax
from jax.experimental import pallas as pl
from jax.experimental.pallas import tpu as pltpu

F32 = jnp.float32
BF16 = jnp.bfloat16
U32 = jnp.uint32
HI = lax.Precision.HIGHEST

EPS = 1e-6
GRID_W = 64
N_MIXERS = 2
M_HEADS = 4
H_BANDS = 16
N_GROUPS = 4
EXP_PER_GROUP = 8
N_EXPERTS = N_GROUPS * EXP_PER_GROUP
MOE_BLOCK = 512
DMA_UNROLL = 8
LANES = 128
HALO = 16
SUB = 8
FFT_N2 = 128
FFT_KB = 4
H_DECAY_EPS = EPS
VMEM_LIMIT = 56 * 1024 * 1024


def _cparams(*sem):
    return pltpu.CompilerParams(dimension_semantics=sem, vmem_limit_bytes=VMEM_LIMIT)


def _full(shape):
    n = len(shape)
    return pl.BlockSpec(shape, lambda *_: (0,) * n)


def _raster(L, tr_pref=128):
    rows = L // GRID_W
    return rows, min(tr_pref, rows)


def _cg_cat(ref):
    return jnp.concatenate([ref[s] for s in range(ref.shape[0])], axis=1)


def _cg_put(ref, val):
    for s in range(ref.shape[0]):
        ref[s] = val[:, s * LANES:(s + 1) * LANES]


def _cg_load(ref4, r):
    G, n = ref4.shape[0], ref4.shape[1]
    flat = ref4.reshape(G, n * SUB, LANES)
    return jnp.concatenate([flat[s, pl.ds(r, n, stride=SUB), :] for s in range(G)], axis=1)


def _cg_store(ref4, r, val):
    G, n = ref4.shape[0], ref4.shape[1]
    flat = ref4.reshape(G, n * SUB, LANES)
    for s in range(G):
        flat[s, pl.ds(r, n, stride=SUB), :] = val[:, s * LANES:(s + 1) * LANES]


def _tt_load(ref3):
    n, G = ref3.shape[0], ref3.shape[1]
    flat = ref3.reshape(n * G, LANES)
    return jnp.concatenate([flat[pl.ds(s, n, stride=G), :] for s in range(G)], axis=1)


def _tt_store(ref3, val):
    n, G = ref3.shape[0], ref3.shape[1]
    flat = ref3.reshape(n * G, LANES)
    for s in range(G):
        flat[pl.ds(s, n, stride=G), :] = val[:, s * LANES:(s + 1) * LANES]


def _pack(re, im):
    hi = pltpu.bitcast(re.astype(BF16).astype(F32), U32)
    lo = pltpu.bitcast(im.astype(BF16).astype(F32), U32)
    return (hi & jnp.uint32(0xFFFF0000)) | (lo >> 16)


def _unpack(w):
    re = pltpu.bitcast(w & jnp.uint32(0xFFFF0000), F32)
    im = pltpu.bitcast(w << 16, F32)
    return re.astype(BF16), im.astype(BF16)


def _sigmoid(x):
    return 1.0 / (1.0 + jnp.exp(-x))


def _silu(x):
    return x * _sigmoid(x)


def _norm_mod(x, g, mod, k):
    ms = jnp.mean(x * x, axis=-1, keepdims=True)
    y = x * lax.rsqrt(ms + EPS) * g
    return y * (1.0 + mod[k + 1:k + 2, :]) + mod[k:k + 1, :]


def _ada_kernel(c_ref, w_ref, b_ref, o_ref):
    s = _silu(c_ref[...])
    o_ref[0] = jnp.dot(s, w_ref[0], precision=HI, preferred_element_type=F32) + b_ref[0]


def _ada(cc, ada_w, ada_b):
    depth, D, N = ada_w.shape
    tn = N // 6
    return pl.pallas_call(
        _ada_kernel,
        grid=(depth, N // tn),
        in_specs=[pl.BlockSpec((8, D), lambda l, j: (0, 0)),
                  pl.BlockSpec((1, D, tn), lambda l, j: (l, 0, j)),
                  pl.BlockSpec((1, 1, tn), lambda l, j: (l, 0, j))],
        out_specs=pl.BlockSpec((1, 8, tn), lambda l, j: (l, 0, j)),
        out_shape=jax.ShapeDtypeStruct((depth, 8, N), F32),
        compiler_params=_cparams("parallel", "parallel"),
        name="ada",
    )(cc, ada_w, ada_b.reshape(depth, 1, N))


def _raster_col(refs, c):
    tr = refs[0].shape[0]
    return jnp.concatenate([r.reshape(tr * SUB, LANES)[pl.ds(c, tr, stride=SUB), :] for r in refs], axis=1)


def _lane_blocks(tr, D, index):
    return [pl.BlockSpec((tr, SUB, LANES), functools.partial(index, s)) for s in range(D // LANES)]


def _norm_proj_kernel(*refs, nchunk, col_major):
    *x_refs, g_ref, mod_ref, w_ref, b_ref, o_ref = refs
    N = o_ref.shape[-1]
    step = N // nchunk
    for c in range(SUB if col_major else 1):
        x = _raster_col(x_refs, c) if col_major else x_refs[0][...]
        u = _norm_mod(x, g_ref[...], mod_ref[...], 0).astype(BF16)
        for j in range(nchunk):
            sl = slice(j * step, (j + 1) * step)
            acc = (jnp.dot(u, w_ref[:, sl], preferred_element_type=F32) + b_ref[:, sl]).astype(o_ref.dtype)
            if col_major:
                o_ref[c, :, sl] = acc
            else:
                o_ref[:, sl] = acc


def _norm_proj(x, g, mod, w, b, col_major):
    L, D = x.shape
    N = w.shape[1]
    consts = [_full((1, D)), _full(mod.shape), _full((D, N)), _full((1, N))]
    kern = functools.partial(_norm_proj_kernel, nchunk=max(1, N // 1024), col_major=col_major)
    args = (g.reshape(1, D), mod, w, b.reshape(1, N))
    if col_major:
        rows, tr = _raster(L)
        out = pl.pallas_call(
            kern,
            grid=(GRID_W // SUB, rows // tr),
            in_specs=_lane_blocks(tr, D, lambda s, iw, ir: (ir, iw, s)) + consts,
            out_specs=pl.BlockSpec((SUB, tr, N), lambda iw, ir: (iw, ir, 0)),
            out_shape=jax.ShapeDtypeStruct((GRID_W, rows, N), BF16),
            compiler_params=_cparams("parallel", "parallel"),
            name="norm_proj_colmajor",
        )(*([x.reshape(rows, GRID_W, D)] * (D // LANES)), *args)
        return out.reshape(L, N)
    tm = min(256, L)
    return pl.pallas_call(
        kern,
        grid=(L // tm,),
        in_specs=[pl.BlockSpec((tm, D), lambda i: (i, 0))] + consts,
        out_specs=pl.BlockSpec((tm, N), lambda i: (i, 0)),
        out_shape=jax.ShapeDtypeStruct((L, N), BF16),
        compiler_params=_cparams("parallel"),
        name="norm_proj",
    )(x, *args)


def _out_proj_kernel(y_ref, w_ref, b_ref, *refs, gate_row, col_major, y_cg):
    *h_refs, mod_ref, o_ref = refs
    gate = mod_ref[gate_row:gate_row + 1, :]
    for c in range(SUB if col_major else 1):
        if y_cg:
            y = jnp.concatenate([y_ref[s, c] if col_major else y_ref[s] for s in range(y_ref.shape[0])],
                                axis=1).astype(BF16)
        else:
            y = y_ref[c] if col_major else y_ref[...]
        acc = jnp.dot(y, w_ref[...], preferred_element_type=F32) + b_ref[...]
        if col_major:
            o_ref[:, c, :] = _raster_col(h_refs, c) + gate * acc
        else:
            o_ref[...] = h_refs[0][...] + gate * acc


def _out_proj(y, w, b, h, mod, gate_row, col_major):
    L, D = h.shape
    K = w.shape[0]
    y_cg = y.ndim == 3
    G = K // LANES
    kern = functools.partial(_out_proj_kernel, gate_row=gate_row, col_major=col_major, y_cg=y_cg)
    if col_major:
        rows, tr = _raster(L)
        hspec = pl.BlockSpec((tr, SUB, D), lambda iw, ir: (ir, iw, 0))
        if y_cg:
            yspec = pl.BlockSpec((G, SUB, tr, LANES), lambda iw, ir: (0, iw, ir, 0))
            yv = y.reshape(G, GRID_W, rows, LANES)
        else:
            yspec = pl.BlockSpec((SUB, tr, K), lambda iw, ir: (iw, ir, 0))
            yv = y.reshape(GRID_W, rows, K)
        out = pl.pallas_call(
            kern,
            grid=(GRID_W // SUB, rows // tr),
            in_specs=[yspec, _full((K, D)), _full((1, D))] + _lane_blocks(tr, D, lambda s, iw, ir: (ir, iw, s))
            + [_full(mod.shape)],
            out_specs=hspec,
            out_shape=jax.ShapeDtypeStruct((rows, GRID_W, D), F32),
            compiler_params=_cparams("parallel", "parallel"),
            name="out_proj_colmajor",
        )(yv, w, b.reshape(1, D), *([h.reshape(rows, GRID_W, D)] * (D // LANES)), mod)
        return out.reshape(L, D)
    tm = min(256, L)
    rows_spec = lambda c: pl.BlockSpec((tm, c), lambda i: (i, 0))
    yspec = pl.BlockSpec((G, tm, LANES), lambda i: (0, i, 0)) if y_cg else rows_spec(K)
    return pl.pallas_call(
        kern,
        grid=(L // tm,),
        in_specs=[yspec, _full((K, D)), _full((1, D)), rows_spec(D), _full(mod.shape)],
        out_specs=rows_spec(D),
        out_shape=jax.ShapeDtypeStruct((L, D), F32),
        compiler_params=_cparams("parallel"),
        name="out_proj",
    )(y, w, b.reshape(1, D), h, mod)


def _halo_specs(tm, L, ncol, cblock):
    hb = tm // HALO
    nb = L // HALO
    prev = pl.BlockSpec((HALO, ncol), lambda i: (jnp.maximum(i * hb - 1, 0), cblock))
    nxt = pl.BlockSpec((HALO, ncol), lambda i: (jnp.minimum((i + 1) * hb, nb - 1), cblock))
    return prev, nxt


def _conv3(x, prev_ref, next_ref, w, b):
    tm = x.shape[0]
    i = pl.program_id(0)
    first = i == 0
    last = i == pl.num_programs(0) - 1
    p = jnp.where(first, 0.0, prev_ref[HALO - 1:HALO, :].astype(F32))
    n = jnp.where(last, 0.0, next_ref[0:1, :].astype(F32))
    row = lax.broadcasted_iota(jnp.int32, x.shape, 0)
    xm1 = jnp.where(row == 0, p, pltpu.roll(x, 1, 0))
    xp1 = jnp.where(row == tm - 1, n, pltpu.roll(x, tm - 1, 0))
    return b + xm1 * w[0:1, :] + x * w[1:2, :] + xp1 * w[2:3, :]


def _mlstm_pre_kernel(x_ref, p_ref, n_ref, cw_ref, cb_ref, qw_ref, kw_ref, vw_ref, gw_ref, gb_ref,
                      xc_ref, q_ref, k_ref, v_ref, g_ref, gt_ref, *, heads, kscale):
    xm = x_ref[...]
    xc = _silu(_conv3(xm.astype(F32), p_ref, n_ref, cw_ref[...], cb_ref[...]))
    xcb = xc.astype(BF16)
    xc_ref[...] = xcb
    inner = xm.shape[1]
    dh = inner // heads
    dqk = qw_ref.shape[2]
    dv = vw_ref.shape[2]
    for h in range(heads):
        xs = xcb[:, h * dh:(h + 1) * dh]
        q_ref[:, h * dqk:(h + 1) * dqk] = jnp.dot(xs, qw_ref[h], preferred_element_type=F32).astype(BF16)
        kh = jnp.dot(xs, kw_ref[h], preferred_element_type=F32) * kscale
        k_ref[:, h * dqk:(h + 1) * dqk] = kh.astype(BF16)
        v_ref[:, h * dv:(h + 1) * dv] = jnp.dot(xm[:, h * dh:(h + 1) * dh], vw_ref[h],
                                                 preferred_element_type=F32).astype(BF16)
    g = (jnp.dot(xcb, gw_ref[:inner, :], preferred_element_type=F32)
         + jnp.dot(xm, gw_ref[inner:, :], preferred_element_type=F32) + gb_ref[...])
    lane = lax.broadcasted_iota(jnp.int32, g.shape, 1)
    is_forget = ((lane >= heads) & (lane < 2 * heads)) | ((lane >= 3 * heads) & (lane < 4 * heads))
    logsig = jnp.minimum(g, 0.0) - jnp.log(1.0 + jnp.exp(-jnp.abs(g)))
    g = jnp.where(is_forget, logsig, g)
    g_ref[...] = g
    gt_ref[...] = g.T[:gt_ref.shape[0], :]


def _mlstm_pre(xo, conv_w, conv_b, q_w, k_w, v_w, gate_w, gate_b):
    L = xo.shape[0]
    heads, dh, dqk = q_w.shape
    dv = v_w.shape[2]
    inner = heads * dh
    tm = min(256, L)
    ng = 4 * heads
    gw = jnp.zeros((2 * inner, LANES), BF16).at[:, :ng].set(gate_w.astype(BF16))
    gb = jnp.zeros((1, LANES), F32).at[0, :ng].set(gate_b)
    prev, nxt = _halo_specs(tm, L, inner, 0)
    rows = lambda c: pl.BlockSpec((tm, c), lambda i: (i, 0))
    return pl.pallas_call(
        functools.partial(_mlstm_pre_kernel, heads=heads, kscale=dqk ** -0.5),
        grid=(L // tm,),
        in_specs=[rows(inner), prev, nxt, _full((3, inner)), _full((1, inner)),
                  _full(q_w.shape), _full(k_w.shape), _full(v_w.shape), _full(gw.shape), _full(gb.shape)],
        out_specs=[rows(inner), rows(heads * dqk), rows(heads * dqk), rows(heads * dv), rows(LANES),
                   pl.BlockSpec((ng, tm), lambda i: (0, i))],
        out_shape=[jax.ShapeDtypeStruct((L, inner), BF16), jax.ShapeDtypeStruct((L, heads * dqk), BF16),
                   jax.ShapeDtypeStruct((L, heads * dqk), BF16), jax.ShapeDtypeStruct((L, heads * dv), BF16),
                   jax.ShapeDtypeStruct((L, LANES), F32), jax.ShapeDtypeStruct((ng, L), F32)],
        compiler_params=_cparams("parallel"),
        name="mlstm_pre",
    )(xo, xo, xo, conv_w, conv_b.reshape(1, inner), q_w.astype(BF16), k_w.astype(BF16), v_w.astype(BF16), gw, gb)


def _scan_kernel(*refs, heads, rev, final):
    if final:
        (q_ref, k_ref, v_ref, g_ref, gt_ref, c0_ref, n0_ref, m0_ref, hf_ref, xc_ref, o_ref, ng_ref, sk_ref,
         out_ref, cT_ref, nT_ref, mT_ref, C_sc, n_sc, m_sc) = refs
    else:
        (q_ref, k_ref, v_ref, g_ref, gt_ref, c0_ref, n0_ref, m0_ref,
         out_ref, cT_ref, nT_ref, mT_ref, C_sc, n_sc, m_sc) = refs
    ci = pl.program_id(0)

    @pl.when(ci == 0)
    def _():
        C_sc[...] = c0_ref[...]
        n_sc[...] = n0_ref[...]
        m_sc[...] = m0_ref[...]

    Lc = q_ref.shape[0]
    dqk = q_ref.shape[1] // heads
    dv = v_ref.shape[1] // heads
    G = g_ref[...]
    GT = gt_ref[...]
    r = lax.broadcasted_iota(jnp.int32, (Lc, Lc), 0)
    c = lax.broadcasted_iota(jnp.int32, (Lc, Lc), 1)
    mask = (c >= r) if rev else (c <= r)
    tri = jnp.where(mask, 1.0, 0.0)
    Bc = jnp.dot(tri, G, precision=HI, preferred_element_type=F32)
    BT = lax.dot_general(GT, tri, (((1,), (1,)), ((), ())), precision=HI, preferred_element_type=F32)
    tot = Bc[0:1, :] if rev else Bc[Lc - 1:Lc, :]
    base = 2 * heads if rev else 0
    for h in range(heads):
        ci_, cf_ = base + h, base + heads + h
        b_col = Bc[:, cf_:cf_ + 1]
        i_col = G[:, ci_:ci_ + 1]
        bT = BT[cf_:cf_ + 1, :]
        iT = GT[ci_:ci_ + 1, :]
        b_last = tot[:, cf_:cf_ + 1]
        m = m_sc[h][:, 0:1]
        logd = jnp.where(mask, b_col - bT + iT, -jnp.inf)
        inter = b_col + m
        mj = jnp.maximum(inter, jnp.max(logd, axis=-1, keepdims=True))
        dmat = jnp.exp(logd - mj)
        qh = q_ref[:, h * dqk:(h + 1) * dqk]
        kh = k_ref[:, h * dqk:(h + 1) * dqk]
        vh = v_ref[:, h * dv:(h + 1) * dv]
        s = lax.dot_general(qh, kh, (((1,), (1,)), ((), ())), preferred_element_type=F32) * dmat
        w_inter = jnp.exp(inter - mj)
        Ct = C_sc[h]
        nrow = n_sc[h]
        num = (jnp.dot(s.astype(BF16), vh, preferred_element_type=F32)
               + w_inter * jnp.dot(qh, Ct.astype(BF16), preferred_element_type=F32))
        den = (jnp.sum(s, axis=-1, keepdims=True)
               + w_inter * jnp.sum(qh.astype(F32) * nrow, axis=-1, keepdims=True))
        hout = num / jnp.maximum(jnp.abs(den), jnp.exp(-mj))
        g_col = b_last - b_col + i_col
        g_row = b_last - bT + iT
        m_new = jnp.maximum(b_last + m, jnp.max(g_row, axis=-1, keepdims=True))
        decay = jnp.exp(b_last + m - m_new)
        wg = jnp.exp(g_col - m_new)
        vw = (vh.astype(F32) * wg).astype(BF16)
        C_sc[h] = decay * Ct + lax.dot_general(kh, vw, (((0,), (0,)), ((), ())), preferred_element_type=F32)
        n_sc[h] = decay * nrow + jnp.sum(kh.astype(F32) * wg, axis=0, keepdims=True)
        m_sc[h] = jnp.broadcast_to(m_new, m_sc.shape[1:])
        cs = slice(h * dv, (h + 1) * dv)
        if final:
            hs = hf_ref[:, cs].astype(F32) + hout
            mu = jnp.mean(hs, axis=-1, keepdims=True)
            d = hs - mu
            var = jnp.mean(d * d, axis=-1, keepdims=True)
            hn = d * lax.rsqrt(var + EPS)
            y = (hn * ng_ref[:, cs] + sk_ref[:, cs] * xc_ref[:, cs].astype(F32)) * _sigmoid(o_ref[:, cs].astype(F32))
            out_ref[:, cs] = y.astype(out_ref.dtype)
        else:
            out_ref[:, cs] = hout.astype(out_ref.dtype)

    @pl.when(ci == pl.num_programs(0) - 1)
    def _():
        cT_ref[...] = C_sc[...]
        nT_ref[...] = n_sc[...]
        mT_ref[...] = m_sc[...]


def _mlstm_scan(q, k, v, g, gt, state, rev, final_args=None):
    L = q.shape[0]
    heads = M_HEADS
    dqk = q.shape[1] // heads
    dv = v.shape[1] // heads
    Lc = min(256, L)
    nc = L // Lc
    idx = (lambda i: (nc - 1 - i, 0)) if rev else (lambda i: (i, 0))
    idx_t = (lambda i: (0, nc - 1 - i)) if rev else (lambda i: (0, i))
    rows = lambda c, cb=0: pl.BlockSpec((Lc, c), lambda i: (idx(i)[0], cb))
    c0, n0, m0 = state
    final = final_args is not None
    in_specs = [rows(heads * dqk), rows(heads * dqk), rows(heads * dv), rows(LANES),
                pl.BlockSpec((gt.shape[0], Lc), idx_t), _full(c0.shape), _full(n0.shape), _full(m0.shape)]
    args = [q, k, v, g, gt, c0, n0, m0]
    if final:
        hf, xc, xo, norm_g, skip = final_args
        inner = heads * dv
        in_specs += [rows(inner), rows(inner), rows(inner, 1), _full((1, inner)), _full((1, inner))]
        args += [hf, xc, xo, norm_g.reshape(1, inner), skip.reshape(1, inner)]
    out_dtype = BF16
    return pl.pallas_call(
        functools.partial(_scan_kernel, heads=heads, rev=rev, final=final),
        grid=(nc,),
        in_specs=in_specs,
        out_specs=[rows(heads * dv), _full(c0.shape), _full(n0.shape), _full(m0.shape)],
        out_shape=[jax.ShapeDtypeStruct((L, heads * dv), out_dtype), jax.ShapeDtypeStruct(c0.shape, F32),
                   jax.ShapeDtypeStruct(n0.shape, F32), jax.ShapeDtypeStruct(m0.shape, F32)],
        scratch_shapes=[pltpu.VMEM(c0.shape, F32), pltpu.VMEM(n0.shape, F32), pltpu.VMEM(m0.shape, F32)],
        compiler_params=_cparams("arbitrary"),
        name="mlstm_scan_bwd" if rev else "mlstm_scan_fwd",
    )(*args)


def _mlstm_stream(xo, mp, st_f, st_b):
    (conv_w, conv_b, q_w, k_w, v_w, gate_w, gate_b, norm_g, skip) = mp
    xc, q, k, v, g, gt = _mlstm_pre(xo, conv_w, conv_b, q_w, k_w, v_w, gate_w, gate_b)
    hf, *sf = _mlstm_scan(q, k, v, g, gt, st_f, rev=False)
    y, *sb = _mlstm_scan(q, k, v, g, gt, st_b, rev=True, final_args=(hf, xc, xo, norm_g, skip))
    return y, tuple(sf), tuple(sb)


def _sconv_kernel(x_ref, p_ref, n_ref, w_ref, b_ref, o_ref):
    _cg_put(o_ref, _conv3(x_ref[...].astype(F32), p_ref, n_ref, w_ref[...], b_ref[...]))


def _sconv(zp, w, b):
    L, C = zp.shape
    tm = min(1024, L)
    tc = C // 3
    hb, nb = tm // HALO, L // HALO
    return pl.pallas_call(
        _sconv_kernel,
        grid=(L // tm, C // tc),
        in_specs=[pl.BlockSpec((tm, tc), lambda i, j: (i, j)),
                  pl.BlockSpec((HALO, tc), lambda i, j: (jnp.maximum(i * hb - 1, 0), j)),
                  pl.BlockSpec((HALO, tc), lambda i, j: (jnp.minimum((i + 1) * hb, nb - 1), j)),
                  pl.BlockSpec((3, tc), lambda i, j: (0, j)), pl.BlockSpec((1, tc), lambda i, j: (0, j))],
        out_specs=pl.BlockSpec((tc // LANES, tm, LANES), lambda i, j: (j, i, 0)),
        out_shape=jax.ShapeDtypeStruct((C // LANES, L, LANES), F32),
        compiler_params=_cparams("parallel", "parallel"),
        name="hyena_sconv",
    )(zp, zp, zp, w, b.reshape(1, C))


def _filt_kernel(w1_ref, b1_ref, fr_ref, w2_ref, b2_ref, w3_ref, dec_ref, f_ref, ss_ref, *, L):
    i = pl.program_id(0)
    tm = f_ref.shape[1]
    emb = w1_ref.shape[0]
    pos = (lax.broadcasted_iota(jnp.int32, (tm, LANES), 0) + i * tm).astype(F32)
    lane = lax.broadcasted_iota(jnp.int32, (tm, LANES), 1)
    band = jnp.where(lane <= H_BANDS, lane, lane - H_BANDS).astype(F32)
    ang = (2.0 * math.pi / L) * pos * band
    quarter = jnp.where(lane <= H_BANDS, 0.5 * math.pi, 0.0)
    feat = jnp.where(lane == 0, pos / max(L - 1, 1), jnp.where(lane < emb, jnp.sin(ang + quarter), 0.0))
    fr = fr_ref[...]
    z = jnp.sin(fr * (jnp.dot(feat, w1_ref[...], precision=HI, preferred_element_type=F32) + b1_ref[...]))
    z = jnp.sin(fr * (jnp.dot(z, w2_ref[...], precision=HI, preferred_element_type=F32) + b2_ref[...]))
    z = jnp.dot(z.astype(BF16), w3_ref[...].astype(BF16), preferred_element_type=F32)
    tau = jnp.abs(pos[:, 0:1] - (L // 2)) / (L / 2)
    filt = z * jnp.exp(-tau * dec_ref[...])
    _cg_put(f_ref, filt)

    @pl.when(i == 0)
    def _():
        ss_ref[...] = jnp.zeros_like(ss_ref)

    ss_ref[...] += jnp.sum(filt * filt, axis=0, keepdims=True)


def _hyena_filters(L, w1, b1, freq, w2, b2, w3, decay):
    emb, fh = w1.shape
    C2 = w3.shape[1]
    tm = min(256, L)
    w1p = jnp.zeros((LANES, fh), F32).at[:emb].set(w1)
    return pl.pallas_call(
        functools.partial(_filt_kernel, L=L),
        grid=(L // tm,),
        in_specs=[_full((LANES, fh)), _full((1, fh)), _full((1, fh)), _full((fh, fh)), _full((1, fh)),
                  _full((fh, C2)), _full((1, C2))],
        out_specs=[pl.BlockSpec((C2 // LANES, tm, LANES), lambda i: (0, i, 0)), _full((1, C2))],
        out_shape=[jax.ShapeDtypeStruct((C2 // LANES, L, LANES), F32), jax.ShapeDtypeStruct((1, C2), F32)],
        compiler_params=_cparams("arbitrary"),
        name="hyena_filters",
    )(w1p, b1.reshape(1, fh), freq.reshape(1, fh), w2, b2.reshape(1, fh), w3, decay.reshape(1, C2))


def _dft_tables(L):
    N = 2 * L
    N2 = FFT_N2
    N1 = N // N2
    k1 = jnp.arange(N1, dtype=jnp.int32)
    n1 = jnp.arange(N1 // 2, dtype=jnp.int32)
    n2 = jnp.arange(N2, dtype=jnp.int32)
    a = ((k1[:, None] * n1[None, :]) % N1).astype(F32) * (2.0 * math.pi / N1)
    b = ((n2[:, None] * k1[None, :]) % N).astype(F32) * (2.0 * math.pi / N)
    cos_a, sin_a, cos_b, sin_b = jnp.cos(a)[None], jnp.sin(a)[None], jnp.cos(b)[:, :, None], jnp.sin(b)[:, :, None]
    ca = (cos_b * cos_a - sin_b * sin_a).astype(BF16)
    sa = (sin_b * cos_a + cos_b * sin_a).astype(BF16)
    cd, sd = jnp.swapaxes(ca, 1, 2), jnp.swapaxes(sa, 1, 2)
    ph2 = (n2[:, None] * n2[None, :]) % N2
    ang2 = ph2.astype(F32) * (2.0 * math.pi / N2)
    c2, s2 = jnp.cos(ang2), jnp.sin(ang2)
    fwd = jnp.block([[c2, s2], [-s2, c2]]).astype(BF16)
    inv = jnp.block([[c2, -s2], [s2, c2]]).astype(BF16)
    return (ca, sa), (cd, sd), (fwd, inv)


def _fft_a_kernel(x_ref, c_ref, s_ref, a_ref):
    for r in range(SUB):
        x = _cg_load(x_ref, r).astype(BF16)
        re = jnp.dot(c_ref[r], x, preferred_element_type=F32)
        im = -jnp.dot(s_ref[r], x, preferred_element_type=F32)
        _cg_store(a_ref, r, _pack(re, im))


def _fft_a(x, gc, gblock, nblocks, tabs):
    ca, sa = tabs
    N2, N1, H1 = ca.shape
    x4 = x.reshape(x.shape[0], H1, N2, LANES)
    tspec = pl.BlockSpec((SUB, N1, H1), lambda j, cb: (j, 0, 0))
    return pl.pallas_call(
        _fft_a_kernel,
        grid=(N2 // SUB, nblocks),
        in_specs=[pl.BlockSpec((gc, H1, SUB, LANES), lambda j, cb: (gblock + cb, 0, j, 0)), tspec, tspec],
        out_specs=pl.BlockSpec((gc, N1, SUB, LANES), lambda j, cb: (cb, 0, j, 0)),
        out_shape=jax.ShapeDtypeStruct((gc * nblocks, N1, N2, LANES), U32),
        compiler_params=_cparams("parallel", "parallel"),
        name="fft_stage_a",
    )(x4, ca, sa)


def _cdft(t_ref, xr, xi):
    n = xr.shape[0]
    out = jnp.dot(t_ref[...], jnp.concatenate([xr, xi], axis=0), preferred_element_type=F32)
    return out[:n], out[n:]


def _fft_h_kernel(a_ref, f_ref, sc_ref, h_ref):
    for t in range(FFT_KB):
        xr, xi = _cdft(f_ref, *_unpack(_cg_cat(a_ref.at[:, t])))
        k1 = pl.program_id(0) * FFT_KB + t
        sc = sc_ref[...] * jnp.where((k1 & 2) == 0, 1.0, -1.0)
        if t % 2 == 0:
            _cg_put(h_ref.at[:, t], _pack(xr * sc, xi * sc))
        else:
            _cg_put(h_ref.at[:, t], _pack(xi * (-sc), xr * sc))


def _fft_h(a, gc, tab2, scale):
    G, N1, N2, _ = a.shape
    blk = pl.BlockSpec((gc, FFT_KB, N2, LANES), lambda j, cb: (cb, j, 0, 0))
    return pl.pallas_call(
        _fft_h_kernel,
        grid=(N1 // FFT_KB, G // gc),
        in_specs=[blk, _full(tab2[0].shape), pl.BlockSpec((1, gc * LANES), lambda j, cb: (0, cb))],
        out_specs=blk,
        out_shape=jax.ShapeDtypeStruct(a.shape, U32),
        compiler_params=_cparams("parallel", "parallel"),
        name="fft_filter_spectrum",
    )(a, tab2[0], scale)


def _fft_b_kernel(a_ref, h_ref, f_ref, i_ref, p_ref):
    for t in range(FFT_KB):
        xr, xi = _cdft(f_ref, *_unpack(_cg_cat(a_ref.at[:, t])))
        hr, hi = _unpack(_cg_cat(h_ref.at[:, t]))
        hr, hi = hr.astype(F32), hi.astype(F32)
        yr = (xr * hr - xi * hi).astype(BF16)
        yi = (xr * hi + xi * hr).astype(BF16)
        _cg_put(p_ref.at[:, t], _pack(*_cdft(i_ref, yr, yi)))


def _fft_b(a, h, hblock, tab2):
    G, N1, N2, _ = a.shape
    blk = pl.BlockSpec((G, FFT_KB, N2, LANES), lambda j: (0, j, 0, 0))
    hblk = pl.BlockSpec((G, FFT_KB, N2, LANES), lambda j: (hblock, j, 0, 0))
    return pl.pallas_call(
        _fft_b_kernel,
        grid=(N1 // FFT_KB,),
        in_specs=[blk, hblk, _full(tab2[0].shape), _full(tab2[1].shape)],
        out_specs=blk,
        out_shape=jax.ShapeDtypeStruct(a.shape, U32),
        compiler_params=_cparams("parallel"),
        name="fft_stage_b",
    )(a, h, tab2[0], tab2[1])


def _fft_d_kernel(p_ref, c_ref, s_ref, g_ref, v_ref, fb_ref, o_ref):
    for r in range(SUB):
        pr, pi = _unpack(_cg_load(p_ref, r))
        y = (jnp.dot(c_ref[r], pr, preferred_element_type=F32)
             - jnp.dot(s_ref[r], pi, preferred_element_type=F32))
        v = _cg_load(v_ref, r)
        _cg_store(o_ref, r, _cg_load(g_ref, r) * (y + v * fb_ref[...]))


def _fft_d(p, tabs, gsrc, gblock, vsrc, vblock, fbias):
    cd, sd = tabs
    N2, H1, N1 = cd.shape
    G = p.shape[0]
    L = H1 * N2
    tspec = pl.BlockSpec((SUB, H1, N1), lambda j: (j, 0, 0))
    sig = lambda gb: pl.BlockSpec((G, H1, SUB, LANES), lambda j: (gb, 0, j, 0))
    out = pl.pallas_call(
        _fft_d_kernel,
        grid=(N2 // SUB,),
        in_specs=[pl.BlockSpec((G, N1, SUB, LANES), lambda j: (0, 0, j, 0)), tspec, tspec, sig(gblock), sig(vblock),
                  _full((1, G * LANES))],
        out_specs=sig(0),
        out_shape=jax.ShapeDtypeStruct((G, H1, N2, LANES), F32),
        compiler_params=_cparams("parallel"),
        name="fft_stage_d",
    )(p, cd, sd, gsrc.reshape(gsrc.shape[0], H1, N2, LANES), vsrc.reshape(vsrc.shape[0], H1, N2, LANES), fbias)
    return out.reshape(G, L, LANES)


def _hyena_long(z, filt, ssq, fbias, tables):
    L = z.shape[1]
    gc = z.shape[0] // 3
    C = gc * LANES
    N = 2 * L
    taba, tabd, tab2 = tables
    scale = lax.rsqrt(ssq + H_DECAY_EPS) * (1.0 / N)
    h = _fft_h(_fft_a(filt, gc, 0, 2, taba), gc, tab2, scale)
    fb = fbias.reshape(1, 2 * C)
    p = _fft_b(_fft_a(z, gc, 2, 1, taba), h, 0, tab2)
    y1 = _fft_d(p, tabd, z, 0, z, 2, fb[:, :C])
    p = _fft_b(_fft_a(y1, gc, 0, 1, taba), h, 1, tab2)
    return _fft_d(p, tabd, z, 1, y1, 0, fb[:, C:])


def _short_conv_kernel(z_ref, f_ref, ss_ref, fb_ref, cf_ref, sf_ref, cd_ref, sd_ref, o_ref):
    L = z_ref.shape[1]
    G = o_ref.shape[0]
    C = G * LANES
    N = 2 * L

    def dft(xb):
        return (jnp.dot(cf_ref[...], xb, preferred_element_type=F32),
                -jnp.dot(sf_ref[...], xb, preferred_element_type=F32))

    def conv(u, blk, gate):
        sl = slice(blk * C, (blk + 1) * C)
        ur, ui = dft(u.astype(BF16))
        fr, fi = dft(_cg_cat(f_ref.at[blk * G:(blk + 1) * G]).astype(BF16))
        q = lax.broadcasted_iota(jnp.int32, fr.shape, 0) & 3
        fr, fi = (jnp.where(q == 0, fr, jnp.where(q == 1, -fi, jnp.where(q == 2, -fr, fi))),
                  jnp.where(q == 0, fi, jnp.where(q == 1, fr, jnp.where(q == 2, -fi, -fr))))
        yr = (ur * fr - ui * fi).astype(BF16)
        yi = (ur * fi + ui * fr).astype(BF16)
        y = (jnp.dot(cd_ref[...], yr, preferred_element_type=F32)
             - jnp.dot(sd_ref[...], yi, preferred_element_type=F32))
        y = y * (lax.rsqrt(ss_ref[:, sl] + H_DECAY_EPS) * (1.0 / N))
        return gate * (y + u * fb_ref[:, sl])

    x1 = _cg_cat(z_ref.at[0:G])
    x2 = _cg_cat(z_ref.at[G:2 * G])
    v = _cg_cat(z_ref.at[2 * G:3 * G])
    _cg_put(o_ref, conv(conv(v, 0, x1), 1, x2))


def _hyena_short(z, filt, ssq, fbias):
    L = z.shape[1]
    G = z.shape[0] // 3
    C = G * LANES
    N = 2 * L
    k = jnp.arange(N, dtype=jnp.int32)
    n = jnp.arange(L, dtype=jnp.int32)
    ang = ((k[:, None] * n[None, :]) % N).astype(F32) * (2.0 * math.pi / N)
    cf, sf = jnp.cos(ang).astype(BF16), jnp.sin(ang).astype(BF16)
    args = (z, filt, ssq, fbias.reshape(1, 2 * C), cf, sf, cf.T, sf.T)
    return pl.pallas_call(
        _short_conv_kernel,
        grid=(1,),
        in_specs=[_full(a.shape) for a in args],
        out_specs=_full((G, L, LANES)),
        out_shape=jax.ShapeDtypeStruct((G, L, LANES), F32),
        compiler_params=_cparams("arbitrary"),
        name="hyena_short_conv",
    )(*args)


def _router_kernel(x_ref, g_ref, mod_ref, rw_ref, rb_ref, c0_ref, xt_ref, rt_ref, cnt_ref, carry):
    i = pl.program_id(0)

    @pl.when(i == 0)
    def _():
        carry[...] = c0_ref[...]

    u = _norm_mod(x_ref[...], g_ref[...], mod_ref[...], 3)
    _tt_store(xt_ref, u)
    u_hi = u.astype(BF16)
    u_lo = (u - u_hi.astype(F32)).astype(BF16)
    logits = jnp.dot(jnp.concatenate([u_hi, u_hi, u_lo], axis=1), rw_ref[...], preferred_element_type=F32) + rb_ref[...]
    tm = u.shape[0]
    lane = lax.broadcasted_iota(jnp.int32, logits.shape, 1).astype(F32)
    ninf = -jnp.inf

    def top(vals):
        mx = jnp.max(vals, axis=-1, keepdims=True)
        ix = jnp.min(jnp.where(vals == mx, lane, float(LANES)), axis=-1, keepdims=True)
        return mx, ix

    lg = jnp.where(lane < N_GROUPS, logits, ninf)
    gmax, gidx = top(lg)
    g_w = 1.0 / jnp.sum(jnp.exp(lg - gmax), axis=-1, keepdims=True)
    lo = N_GROUPS + EXP_PER_GROUP * gidx
    le = jnp.where((lane >= lo) & (lane < lo + EXP_PER_GROUP), logits, ninf)
    m1, i1 = top(le)
    m2, i2 = top(jnp.where(lane == i1, ninf, le))
    r = jnp.exp(m2 - m1)
    w1 = g_w / (1.0 + r)
    w2 = w1 * r
    e1 = i1 - N_GROUPS
    e2 = i2 - N_GROUPS
    oh1 = lane == e1
    oh2 = lane == e2
    both = jnp.where(oh1 | oh2, 1.0, 0.0)
    rr = lax.broadcasted_iota(jnp.int32, (tm, tm), 0)
    cc = lax.broadcasted_iota(jnp.int32, (tm, tm), 1)
    earlier = jnp.where(cc < rr, 1.0, 0.0).astype(BF16)
    before = jnp.dot(earlier, both.astype(BF16), preferred_element_type=F32) + carry[...]
    rank1 = jnp.sum(jnp.where(oh1, before, 0.0), axis=-1, keepdims=True)
    rank2 = jnp.sum(jnp.where(oh2, before, 0.0), axis=-1, keepdims=True)
    carry[...] += jnp.sum(both, axis=0, keepdims=True)
    cnt_ref[...] = carry[...]
    out = jnp.zeros(logits.shape, F32)
    for j, col in enumerate((e1, e2, w1, w2, rank1, rank2)):
        out = jnp.where(lane == j, col, out)
    rt_ref[...] = out


def _router(h, g, mod, rw, rb, counts0):
    L, D = h.shape
    tm = min(256, L)
    return pl.pallas_call(
        _router_kernel,
        grid=(L // tm,),
        in_specs=[pl.BlockSpec((tm, D), lambda i: (i, 0)), _full((1, D)), _full(mod.shape),
                  _full(rw.shape), _full(rb.shape), _full((1, LANES))],
        out_specs=[pl.BlockSpec((tm, D // LANES, LANES), lambda i: (i, 0, 0)),
                   pl.BlockSpec((tm, LANES), lambda i: (i, 0)), _full((1, LANES))],
        out_shape=[jax.ShapeDtypeStruct((L, D // LANES, LANES), F32), jax.ShapeDtypeStruct((L, LANES), F32),
                   jax.ShapeDtypeStruct((1, LANES), F32)],
        scratch_shapes=[pltpu.VMEM((1, LANES), F32)],
        compiler_params=_cparams("arbitrary"),
        name="moe_router",
    )(h, g.reshape(1, D), mod, rw, rb, counts0)


def _dispatch_kernel(dest_ref, *refs):
    *x_refs, xb_in_ref, xb_ref, sem = refs
    del xb_in_ref
    tm = x_refs[0].shape[0]
    i = pl.program_id(0)
    a0 = 2 * tm * i

    def copy_rows(x_ref):
        def issue(r, carry):
            for k in range(2):
                pltpu.make_async_copy(x_ref.at[r], xb_ref.at[dest_ref[a0 + 2 * r + k]], sem).start()
            return carry

        def wait(r, carry):
            pltpu.make_async_copy(x_ref.at[0], xb_ref.at[0], sem).wait()
            return carry

        lax.fori_loop(0, tm, issue, 0, unroll=DMA_UNROLL)
        lax.fori_loop(0, 2 * tm, wait, 0, unroll=DMA_UNROLL)

    if len(x_refs) == 1:
        copy_rows(x_refs[0])
    else:
        pl.when(i == 0)(lambda: copy_rows(x_refs[0]))
        pl.when(i > 0)(lambda: copy_rows(x_refs[1]))


def _dispatch(xts, dest, xb):
    G = xts[0].shape[1]
    tm = min(256, xts[-1].shape[0])
    blk = (tm, G, LANES)
    if len(xts) == 2:
        assert xts[0].shape[0] == tm, "context stream must be exactly one tile"
        x_specs = [pl.BlockSpec(blk, lambda i, d: (0, 0, 0)),
                   pl.BlockSpec(blk, lambda i, d: (jnp.maximum(i - 1, 0), 0, 0))]
    else:
        x_specs = [pl.BlockSpec(blk, lambda i, d: (i, 0, 0))]
    T = sum(x.shape[0] for x in xts)
    return pl.pallas_call(
        _dispatch_kernel,
        grid_spec=pltpu.PrefetchScalarGridSpec(
            num_scalar_prefetch=1,
            grid=(T // tm,),
            in_specs=x_specs + [pl.BlockSpec(memory_space=pl.ANY)],
            out_specs=pl.BlockSpec(memory_space=pl.ANY),
            scratch_shapes=[pltpu.SemaphoreType.DMA(())]),
        out_shape=jax.ShapeDtypeStruct(xb.shape, F32),
        input_output_aliases={1 + len(xts): 0},
        compiler_params=_cparams("arbitrary"),
        name="moe_dispatch",
    )(dest, *xts, xb)


def _expert_kernel(be_ref, nb_ref, x_ref, w13_ref, w2_ref, o_ref, w13b, w2b):
    b = pl.program_id(0)
    live = b < nb_ref[0]
    new_expert = (b == 0) | (be_ref[b] != be_ref[jnp.maximum(b - 1, 0)])

    @pl.when(jnp.logical_not(live))
    def _():
        o_ref[...] = jnp.zeros_like(o_ref)

    @pl.when(live & new_expert)
    def _():
        w13b[...] = w13_ref[0].astype(BF16)
        w2b[...] = w2_ref[0].astype(BF16)

    @pl.when(live)
    def _():
        de = w2b.shape[0]
        x = _tt_load(x_ref).astype(BF16)
        hcat = jnp.dot(x, w13b[...], preferred_element_type=F32)
        a = (_silu(hcat[:, :de]) * hcat[:, de:]).astype(BF16)
        _tt_store(o_ref, jnp.dot(a, w2b[...], preferred_element_type=F32))


def _experts(xb, blk_exp, n_used, w13, w2):
    R, G, _ = xb.shape
    D = G * LANES
    de = w2.shape[1]
    nb = R // MOE_BLOCK
    blk = pl.BlockSpec((MOE_BLOCK, G, LANES), lambda b, be, nu: (b, 0, 0))
    return pl.pallas_call(
        _expert_kernel,
        grid_spec=pltpu.PrefetchScalarGridSpec(
            num_scalar_prefetch=2,
            grid=(nb,),
            in_specs=[blk,
                      pl.BlockSpec((1, D, 2 * de), lambda b, be, nu: (be[b], 0, 0)),
                      pl.BlockSpec((1, de, D), lambda b, be, nu: (be[b], 0, 0))],
            out_specs=blk,
            scratch_shapes=[pltpu.VMEM((D, 2 * de), BF16), pltpu.VMEM((de, D), BF16)]),
        out_shape=jax.ShapeDtypeStruct(xb.shape, F32),
        compiler_params=_cparams("arbitrary"),
        name="moe_experts",
    )(blk_exp, n_used, xb, w13, w2)


def _combine_kernel(dest_ref, h_ref, rt_ref, mod_ref, fg_ref, yb_ref, o_ref, buf, sem, *, a_base, final):
    tm = h_ref.shape[0]
    i = pl.program_id(0)
    slot = i % 2

    def fetch(tile, sl):
        a0 = a_base + 2 * tm * tile

        def issue(r, carry):
            for k in range(2):
                pltpu.make_async_copy(yb_ref.at[dest_ref[a0 + 2 * r + k]], buf.at[sl, k, r], sem.at[sl]).start()
            return carry

        lax.fori_loop(0, tm, issue, 0, unroll=DMA_UNROLL)

    @pl.when(i == 0)
    def _():
        fetch(0, 0)

    @pl.when(i + 1 < pl.num_programs(0))
    def _():
        fetch(i + 1, 1 - slot)

    def wait(r, carry):
        pltpu.make_async_copy(yb_ref.at[0], buf.at[slot, 0, 0], sem.at[slot]).wait()
        return carry

    lax.fori_loop(0, 2 * tm, wait, 0, unroll=DMA_UNROLL)
    rt = rt_ref[...]
    y = rt[:, 2:3] * _tt_load(buf.at[slot, 0]) + rt[:, 3:4] * _tt_load(buf.at[slot, 1])
    out = h_ref[...] + mod_ref[5:6, :] * y
    if final:
        out = out * lax.rsqrt(jnp.mean(out * out, axis=-1, keepdims=True) + EPS) * fg_ref[...]
    o_ref[...] = out


def _combine(h, yb, dest, row0, route, mod, final_g=None):
    L, D = h.shape
    final = final_g is not None
    fg = (final_g if final else jnp.ones((D,), F32)).reshape(1, D)
    tm = min(256, L)
    return pl.pallas_call(
        functools.partial(_combine_kernel, a_base=2 * row0, final=final),
        grid_spec=pltpu.PrefetchScalarGridSpec(
            num_scalar_prefetch=1,
            grid=(L // tm,),
            in_specs=[pl.BlockSpec((tm, D), lambda i, d: (i, 0)), pl.BlockSpec((tm, LANES), lambda i, d: (i, 0)),
                      pl.BlockSpec(mod.shape, lambda i, d: (0, 0)), pl.BlockSpec((1, D), lambda i, d: (0, 0)),
                      pl.BlockSpec(memory_space=pl.ANY)],
            out_specs=pl.BlockSpec((tm, D), lambda i, d: (i, 0)),
            scratch_shapes=[pltpu.VMEM((2, 2, tm, D // LANES, LANES), F32), pltpu.SemaphoreType.DMA((2,))]),
        out_shape=jax.ShapeDtypeStruct((L, D), F32),
        compiler_params=_cparams("arbitrary"),
        name="moe_combine",
    )(dest, h, route, mod, fg, yb)


def _moe(streams, g, rw, rb, w13, w2, layer, xb, final_g=None):
    counts = jnp.zeros((1, LANES), F32)
    xts, routes = [], []
    for h, mod in streams:
        xt, rt, counts = _router(h, g, mod, rw, rb, counts)
        xts.append(xt)
        routes.append(rt)
    route = jnp.concatenate(routes, axis=0) if len(routes) > 1 else routes[0]
    T = route.shape[0]
    A = 2 * T
    cnt = counts[0, :N_EXPERTS].astype(jnp.int32)
    padded = (cnt + MOE_BLOCK - 1) // MOE_BLOCK * MOE_BLOCK
    p_ends = jnp.cumsum(padded)
    p_starts = p_ends - padded
    experts = route[:, 0:2].astype(jnp.int32)
    onehot = experts[:, :, None] == jnp.arange(N_EXPERTS, dtype=jnp.int32)
    dest = (jnp.sum(jnp.where(onehot, p_starts, 0), axis=-1) + route[:, 4:6].astype(jnp.int32)).reshape(A)
    n_blocks = xb.shape[0] // MOE_BLOCK
    assert n_blocks >= -(-A // MOE_BLOCK) + N_EXPERTS
    blk_start = jnp.arange(n_blocks, dtype=jnp.int32) * MOE_BLOCK
    blk_exp = jnp.minimum(jnp.sum(p_ends[None, :] <= blk_start[:, None], axis=1), N_EXPERTS - 1).astype(jnp.int32)
    blk_exp = blk_exp + layer * N_EXPERTS
    n_used = (p_ends[-1:] // MOE_BLOCK).astype(jnp.int32)
    xb = _dispatch(xts, dest, xb)
    yb = _experts(xb, blk_exp, n_used, w13, w2)
    outs, row0 = [], 0
    for j, ((h, mod), rt) in enumerate(zip(streams, routes)):
        fg = final_g if j == len(streams) - 1 else None
        outs.append(_combine(h, yb, dest, row0, rt, mod, fg))
        row0 += h.shape[0]
    return outs, xb


def _zero_state(heads, dqk, dv):
    return (jnp.zeros((heads, dqk, dv), F32), jnp.zeros((heads, 1, dqk), F32), jnp.zeros((heads, 1, LANES), F32))


def kernel(x, c, ctx, c_ctx, ada_w, ada_b, norm1_g, norm2_g, rg_w, rg_b, re_w, re_b, moe_w13, moe_w2, m_in_w, m_conv_w, m_conv_b, m_q_w, m_k_w, m_v_w, m_gate_w, m_gate_b, m_norm_g, m_skip, m_out_w, h_in_w, h_in_b, h_sc_w, h_sc_b, h_f_w1, h_f_b1, h_f_freq, h_f_w2, h_f_b2, h_f_w3, h_decay, h_fbias, h_out_w, h_out_b, final_g):
    B, S, D = x.shape
    assert B == 1, "kernel is written for the single-sequence problem shape"
    depth = ada_w.shape[0]
    hl, hc = x[0], ctx[0]
    cc = jnp.zeros((8, D), F32).at[0].set(c[0]).at[1].set(c_ctx)
    mods = _ada(cc, ada_w, ada_b).reshape(depth, 8, 6, D)
    pad8 = lambda m: jnp.concatenate([m, jnp.zeros((2, D), F32)], axis=0)
    nrt = N_GROUPS + N_EXPERTS
    zeros_d = jnp.zeros((D,), F32)
    tables = _dft_tables(S) if depth > 1 else None
    w13_all = moe_w13.reshape((-1,) + moe_w13.shape[2:])
    w2_all = moe_w2.reshape((-1,) + moe_w2.shape[2:])
    n_rows = (-(-2 * (S + hc.shape[0]) // MOE_BLOCK) + N_EXPERTS) * MOE_BLOCK
    xb = jnp.zeros((n_rows, D // LANES, LANES), F32)

    for i in range(depth):
        kind, slot = i % N_MIXERS, i // N_MIXERS
        col_major = (slot % 2) == 1
        ctx_live = any(j % N_MIXERS == 0 for j in range(i + 1, depth))
        ml, mc = pad8(mods[i, 0]), pad8(mods[i, 1])

        if kind == 0:
            w_in = m_in_w[slot].astype(BF16)
            w_out = m_out_w[slot].astype(BF16)
            inner = w_out.shape[0]
            mp = (m_conv_w[slot], m_conv_b[slot], m_q_w[slot], m_k_w[slot], m_v_w[slot], m_gate_w[slot],
                  m_gate_b[slot], m_norm_g[slot], m_skip[slot])
            heads, _, dqk = m_q_w[slot].shape
            dv = m_v_w[slot].shape[2]
            zb = jnp.zeros((w_in.shape[1],), F32)
            xo_c = _norm_proj(hc, norm1_g[i], mc, w_in, zb, False)
            yc, st_f, st_b = _mlstm_stream(xo_c, mp, _zero_state(heads, dqk, dv), _zero_state(heads, dqk, dv))
            xo_l = _norm_proj(hl, norm1_g[i], ml, w_in, zb, col_major)
            yl, _, _ = _mlstm_stream(xo_l, mp, st_f, st_b)
            hl = _out_proj(yl, w_out, zeros_d, hl, ml, 2, col_major)
            if ctx_live:
                hc = _out_proj(yc, w_out, zeros_d, hc, mc, 2, False)
        else:
            w_in = h_in_w[slot].astype(BF16)
            w_out = h_out_w[slot].astype(BF16)
            fp = (h_f_w1[slot], h_f_b1[slot], h_f_freq[slot], h_f_w2[slot], h_f_b2[slot], h_f_w3[slot], h_decay[slot])
            zl = _sconv(_norm_proj(hl, norm1_g[i], ml, w_in, h_in_b[slot], col_major), h_sc_w[slot], h_sc_b[slot])
            filt, ssq = _hyena_filters(S, *fp)
            yl = _hyena_long(zl, filt, ssq, h_fbias[slot], tables)
            hl = _out_proj(yl, w_out, h_out_b[slot], hl, ml, 2, col_major)
            if ctx_live:
                zc = _sconv(_norm_proj(hc, norm1_g[i], mc, w_in, h_in_b[slot], False), h_sc_w[slot], h_sc_b[slot])
                filt_c, ssq_c = _hyena_filters(hc.shape[0], *fp)
                yc = _hyena_short(zc, filt_c, ssq_c, h_fbias[slot])
                hc = _out_proj(yc, w_out, h_out_b[slot], hc, mc, 2, False)

        rw = jnp.zeros((D, LANES), F32).at[:, :N_GROUPS].set(rg_w[i]).at[:, N_GROUPS:nrt].set(re_w[i])
        rw_hi = rw.astype(BF16)
        rw_lo = (rw - rw_hi.astype(F32)).astype(BF16)
        rw = jnp.concatenate([rw_hi, rw_lo, rw_hi], axis=0)
        rb = jnp.zeros((1, LANES), F32).at[0, :N_GROUPS].set(rg_b[i]).at[0, N_GROUPS:nrt].set(re_b[i])
        fg = final_g if i == depth - 1 else None
        if ctx_live:
            (hc, hl), xb = _moe([(hc, mc), (hl, ml)], norm2_g[i], rw, rb, w13_all, w2_all, i, xb, fg)
        else:
            (hl,), xb = _moe([(hl, ml)], norm2_g[i], rw, rb, w13_all, w2_all, i, xb, fg)

    return hl[None]
```
